```python
import math
import jax, jax.numpy as jnp
from jax import lax
import numpy as np

D_MODEL = 1024
BATCH = 2
SEQ = 8192
DEPTH = 4

N_MIXERS = 2
N_CONV_LAYERS = (DEPTH + N_MIXERS - 1) // N_MIXERS
N_NSA_LAYERS = DEPTH // N_MIXERS
D_FF = 2816
CONV_WIDTH = 3
NSA_HEADS = 16
HEAD_DIM = D_MODEL // NSA_HEADS
NSA_GROUPS = 4
HEADS_PER_GROUP = NSA_HEADS // NSA_GROUPS
KV_DIM = NSA_GROUPS * HEAD_DIM
CMP_BLOCK = 32
CMP_STRIDE = 16
CMP_HIDDEN = 256
SEL_BLOCK = 64
SEL_TOP_N = 16
WINDOW = 512
Q_BLOCK = 128
REL_BUCKETS = 32
REL_MAX_DIST = 128
NSA_IN_DIM = NSA_HEADS * HEAD_DIM + 6 * KV_DIM + 3 * NSA_HEADS
RMS_EPS = 1e-6
NEG_INF = -1e30
FORCE = 1e9
HALF = 0.5

kernel_name = "hybrid_conv_nsa_macaron_adaln"


def rms_norm(x, g):
    xf = x.astype(jnp.float32)
    y = xf * lax.rsqrt(jnp.mean(xf * xf, axis=-1, keepdims=True) + RMS_EPS)
    return (y * g.astype(jnp.float32)).astype(x.dtype)


def adaln(x, g, shift, scale):
    return rms_norm(x, g) * (1 + scale[:, None, :]) + shift[:, None, :]


def t5_bucket(dist):
    n = jnp.maximum(dist, 0)
    max_exact = REL_BUCKETS // 2
    nf = jnp.maximum(n, 1).astype(jnp.float32)
    large = max_exact + (jnp.log(nf / max_exact) / math.log(REL_MAX_DIST / max_exact)
                         * (REL_BUCKETS - max_exact)).astype(jnp.int32)
    large = jnp.minimum(large, REL_BUCKETS - 1)
    return jnp.where(n < max_exact, n, large)


def masked_softmax(logits, valid):
    p = jax.nn.softmax(jnp.where(valid, logits, NEG_INF), axis=-1)
    return jnp.where(valid, p, 0.0)


def swiglu(h, w_in, w_out):
    g, u = jnp.split(h @ w_in, 2, axis=-1)
    return (jax.nn.silu(g) * u) @ w_out


def short_conv_mixer(h, w_in, conv_w, w_out):
    b_gate, c_gate, v = jnp.split(h @ w_in, 3, axis=-1)
    u = c_gate * v
    y = lax.conv_general_dilated(
        u, conv_w[:, None, :].astype(u.dtype), window_strides=(1,),
        padding=[(CONV_WIDTH - 1, 0)], dimension_numbers=("NWC", "WIO", "NWC"),
        feature_group_count=D_MODEL)
    return (b_gate * y) @ w_out


def nsa_mixer(h, w_in, cmp_pos, cmp_w1, cmp_w2, q_gain, k_gain, w_out, rel_bias):
    bsz, seq, _ = h.shape
    G, Hg, hd = NSA_GROUPS, HEADS_PER_GROUP, HEAD_DIM
    n_cmp = (seq - CMP_BLOCK) // CMP_STRIDE + 1
    n_sel = seq // SEL_BLOCK
    n_top = min(SEL_TOP_N, n_sel)
    n_qblk = seq // Q_BLOCK
    scale = HEAD_DIM ** -0.5
    f32 = jnp.float32

    offs = [NSA_HEADS * HEAD_DIM + i * KV_DIM for i in range(7)]
    q, kc, vc, ks, vs, kw, vw, gl = jnp.split(h @ w_in, offs, axis=-1)
    q = rms_norm(q.reshape(bsz, seq, G, Hg, hd), q_gain).transpose(0, 2, 3, 1, 4)
    gates = jax.nn.sigmoid(gl.astype(f32)).reshape(bsz, seq, G, Hg, 3).transpose(0, 2, 3, 1, 4)

    def kv(t):
        return t.reshape(bsz, seq, G, hd)

    blk_idx = jnp.arange(n_cmp)[:, None] * CMP_STRIDE + jnp.arange(CMP_BLOCK)[None, :]

    def compress(t, w1, w2):
        blocks = kv(t)[:, blk_idx] + cmp_pos[None, None, :, None, :]
        flat = blocks.transpose(0, 3, 1, 2, 4).reshape(bsz, G, n_cmp, CMP_BLOCK * hd)
        return jax.nn.gelu(flat @ w1) @ w2

    k_cmp = rms_norm(compress(kc, cmp_w1[0], cmp_w2[0]), k_gain[0])
    v_cmp = compress(vc, cmp_w1[1], cmp_w2[1])
    k_sel = rms_norm(kv(ks), k_gain[1]).transpose(0, 2, 1, 3).reshape(bsz, G, n_sel, SEL_BLOCK, hd)
    v_sel = kv(vs).transpose(0, 2, 1, 3).reshape(bsz, G, n_sel, SEL_BLOCK, hd)
    pad = ((0, 0), (0, 0), (WINDOW, 0), (0, 0))
    k_win = jnp.pad(rms_norm(kv(kw), k_gain[2]).transpose(0, 2, 1, 3), pad)
    v_win = jnp.pad(kv(vw).transpose(0, 2, 1, 3), pad)

    rb = rel_bias.astype(f32).T.reshape(G, Hg, REL_BUCKETS)
    g_ix = jnp.arange(G)[None, :, None, None, None]
    h_ix = jnp.arange(Hg)[None, None, :, None, None]
    cmp_start = jnp.arange(n_cmp) * CMP_STRIDE
    cmp_end = cmp_start + CMP_BLOCK - 1
    sel_start = jnp.arange(n_sel) * SEL_BLOCK
    overlap = ((cmp_start[:, None] <= sel_start[None, :] + SEL_BLOCK - 1)
               & (cmp_end[:, None] >= sel_start[None, :])).astype(f32)
    sel_ids = jnp.arange(n_sel)
    gather = jax.vmap(jax.vmap(lambda kb, ib: kb[ib]))

    def block(qi):
        t0 = qi * Q_BLOCK
        t = t0 + jnp.arange(Q_BLOCK)
        qb = lax.dynamic_slice_in_dim(q, t0, Q_BLOCK, axis=3)
        gb = lax.dynamic_slice_in_dim(gates, t0, Q_BLOCK, axis=3)

        dist_c = t[:, None] - cmp_end[None, :]
        s_c = (jnp.einsum('bghqd,bgnd->bghqn', qb, k_cmp).astype(f32) * scale
               + rb[:, :, t5_bucket(dist_c)])
        p_c = masked_softmax(s_c, dist_c >= 0)
        o_c = jnp.einsum('bghqn,bgnd->bghqd', p_c.astype(v_cmp.dtype), v_cmp)

        imp = jnp.einsum('bghqn,ns->bgqs', p_c, overlap)
        cur = t // SEL_BLOCK
        forced = ((sel_ids[None, :] == 0) | (sel_ids[None, :] == cur[:, None])
                  | (sel_ids[None, :] == cur[:, None] - 1))
        imp = jnp.where(sel_start[None, :] > t[:, None], NEG_INF, imp)
        imp = jnp.where(forced, FORCE, imp)
        top = lax.top_k(imp, n_top)[1]
        kg = gather(k_sel, top).reshape(bsz, G, Q_BLOCK, n_top * SEL_BLOCK, hd)
        vg = gather(v_sel, top).reshape(bsz, G, Q_BLOCK, n_top * SEL_BLOCK, hd)
        pos_s = (top[..., None] * SEL_BLOCK + jnp.arange(SEL_BLOCK)).reshape(bsz, G, Q_BLOCK, -1)
        dist_s = t[:, None] - pos_s
        s_s = (jnp.einsum('bghqd,bgqkd->bghqk', qb, kg).astype(f32) * scale
               + rb[g_ix, h_ix, t5_bucket(dist_s)[:, :, None]])
        p_s = masked_softmax(s_s, (dist_s >= 0)[:, :, None])
        o_s = jnp.einsum('bghqk,bgqkd->bghqd', p_s.astype(vg.dtype), vg)

        kwb = lax.dynamic_slice_in_dim(k_win, t0, WINDOW + Q_BLOCK, axis=2)
        vwb = lax.dynamic_slice_in_dim(v_win, t0, WINDOW + Q_BLOCK, axis=2)
        pos_w = t0 - WINDOW + jnp.arange(WINDOW + Q_BLOCK)
        dist_w = t[:, None] - pos_w[None, :]
        valid_w = (dist_w >= 0) & (dist_w < WINDOW) & (pos_w[None, :] >= 0)
        s_w = (jnp.einsum('bghqd,bgkd->bghqk', qb, kwb).astype(f32) * scale
               + rb[:, :, t5_bucket(dist_w)])
        p_w = masked_softmax(s_w, valid_w)
        o_w = jnp.einsum('bghqk,bgkd->bghqd', p_w.astype(vwb.dtype), vwb)

        out = gb[..., 0:1] * o_c + gb[..., 1:2] * o_s + gb[..., 2:3] * o_w
        return out.astype(h.dtype)

    o = lax.map(block, jnp.arange(n_qblk))
    o = o.transpose(1, 0, 4, 2, 3, 5).reshape(bsz, seq, NSA_HEADS * HEAD_DIM)
    return o @ w_out


def setup_inputs(seed: int = 0) -> dict:
    key = jax.random.key(seed)
    ks = jax.random.split(key, 20)
    f32 = jnp.float32

    def nrm(k, shape, s):
        return jax.random.normal(k, shape, dtype=f32) * s

    return {
        "x": nrm(ks[0], (BATCH, SEQ, D_MODEL), 1.0),
        "c": nrm(ks[1], (BATCH, D_MODEL), 1.0),
        "norm_g": 1.0 + nrm(ks[2], (DEPTH, 3, D_MODEL), 0.02),
        "ada_w": nrm(ks[3], (DEPTH, D_MODEL, 9 * D_MODEL), 0.5 * D_MODEL ** -0.5),
        "ada_b": nrm(ks[4], (DEPTH, 9 * D_MODEL), 0.02),
        "ffn_w_in": nrm(ks[5], (DEPTH, 2, D_MODEL, 2 * D_FF), D_MODEL ** -0.5),
        "ffn_w_out": nrm(ks[6], (DEPTH, 2, D_FF, D_MODEL), D_FF ** -0.5),
        "conv_w_in": nrm(ks[7], (N_CONV_LAYERS, D_MODEL, 3 * D_MODEL), D_MODEL ** -0.5),
        "conv_w": nrm(ks[8], (N_CONV_LAYERS, CONV_WIDTH, D_MODEL), CONV_WIDTH ** -0.5),
        "conv_w_out": nrm(ks[9], (N_CONV_LAYERS, D_MODEL, D_MODEL), D_MODEL ** -0.5),
        "nsa_w_in": nrm(ks[10], (N_NSA_LAYERS, D_MODEL, NSA_IN_DIM), D_MODEL ** -0.5),
        "nsa_cmp_pos": nrm(ks[11], (N_NSA_LAYERS, CMP_BLOCK, HEAD_DIM), 0.5),
        "nsa_cmp_w1": nrm(ks[12], (N_NSA_LAYERS, 2, CMP_BLOCK * HEAD_DIM, CMP_HIDDEN), (CMP_BLOCK * HEAD_DIM) ** -0.5),
        "nsa_cmp_w2": nrm(ks[13], (N_NSA_LAYERS, 2, CMP_HIDDEN, HEAD_DIM), CMP_HIDDEN ** -0.5),
        "nsa_q_gain": 1.0 + nrm(ks[14], (N_NSA_LAYERS, HEAD_DIM), 0.02),
        "nsa_k_gain": 1.0 + nrm(ks[15], (N_NSA_LAYERS, 3, HEAD_DIM), 0.02),
        "nsa_w_out": nrm(ks[16], (N_NSA_LAYERS, D_MODEL, D_MODEL), D_MODEL ** -0.5),
        "rel_bias": nrm(ks[17], (REL_BUCKETS, NSA_HEADS), 0.5),
    }


def reference(x, c, norm_g, ada_w, ada_b, ffn_w_in, ffn_w_out, conv_w_in, conv_w, conv_w_out,
              nsa_w_in, nsa_cmp_pos, nsa_cmp_w1, nsa_cmp_w2, nsa_q_gain, nsa_k_gain, nsa_w_out,
              rel_bias):
    cond = jax.nn.silu(c)
    bsz = c.shape[0]
    for i in range(DEPTH):
        mod = (cond @ ada_w[i] + ada_b[i]).reshape(bsz, 3, 3, D_MODEL)
        shift, scl, gate = mod[:, :, 0], mod[:, :, 1], mod[:, :, 2]

        h = adaln(x, norm_g[i, 0], shift[:, 0], scl[:, 0])
        x = x + HALF * gate[:, 0, None, :] * swiglu(h, ffn_w_in[i, 0], ffn_w_out[i, 0])

        h = adaln(x, norm_g[i, 1], shift[:, 1], scl[:, 1])
        j = i // N_MIXERS
        if i % N_MIXERS == 0:
            y = short_conv_mixer(h, conv_w_in[j], conv_w[j], conv_w_out[j])
        else:
            y = nsa_mixer(h, nsa_w_in[j], nsa_cmp_pos[j], nsa_cmp_w1[j], nsa_cmp_w2[j],
                          nsa_q_gain[j], nsa_k_gain[j], nsa_w_out[j], rel_bias)
        x = x + gate[:, 1, None, :] * y

        h = adaln(x, norm_g[i, 2], shift[:, 2], scl[:, 2])
        x = x + HALF * gate[:, 2, None, :] * swiglu(h, ffn_w_in[i, 1], ffn_w_out[i, 1])
    return x
```

```python
import functools
import math

import numpy as np
import jax
import jax.numpy as jnp
from jax import lax
from jax.experimental import pallas as pl
from jax.experimental.pallas import tpu as pltpu

F32 = jnp.float32
BF16 = jnp.bfloat16

D_MODEL = 1024
D_FF = 2816
HEADS = 16
HEAD_DIM = 64
GROUPS = 4
HEADS_PER_GROUP = 4
KV_DIM = GROUPS * HEAD_DIM
CMP_BLOCK = 32
CMP_STRIDE = 16
CMP_HIDDEN = 256
SEL_BLOCK = 64
SEL_TOP_N = 16
WINDOW = 512
Q_BLOCK = 128
REL_BUCKETS = 32
REL_MAX_DIST = 128
RMS_EPS = 1e-6
NEG = -1e30
FORCE = 1e9

LANES = 128
QL = HEADS_PER_GROUP * Q_BLOCK
FAR_KEYS = 1024
FAR_BLOCKS = FAR_KEYS // SEL_BLOCK
FF_CHUNK = 256
TOKEN_TILE = 512
VMEM_LIMIT = 56 * 1024 * 1024


def _dot(a, b):
    return jnp.dot(a, b, preferred_element_type=F32)


def _cparams(sem):
    return pltpu.CompilerParams(dimension_semantics=sem, vmem_limit_bytes=VMEM_LIMIT)


def _resident(shape, index_map):
    return pl.BlockSpec(shape, index_map, pipeline_mode=pl.Buffered(1))


def _adaln(x, g, mod_ref, sub):
    ms = jnp.mean(x * x, axis=-1, keepdims=True)
    shift = mod_ref[3 * sub:3 * sub + 1, :]
    scale = mod_ref[3 * sub + 1:3 * sub + 2, :]
    return (x * lax.rsqrt(ms + RMS_EPS) * g) * (1.0 + scale) + shift


def _split_bf16(v):
    hi = v.astype(BF16)
    lo = (v - hi.astype(F32)).astype(BF16)
    return hi, lo


def _ada_kernel(c_ref, w_ref, b_ref, o_ref):
    c = c_ref[...]
    cond = c * jax.nn.sigmoid(c)
    chi, clo = _split_bf16(cond)
    whi, wlo = _split_bf16(w_ref[...])
    o_ref[...] = _dot(chi, whi) + _dot(chi, wlo) + _dot(clo, whi) + b_ref[...]


def _ada_mod(c, ada_w, ada_b):
    depth, d, n = ada_w.shape
    bsz = c.shape[0]
    rows = 8
    tn = 1024
    c_pad = jnp.zeros((rows, d), F32).at[:bsz].set(c)
    out = pl.pallas_call(
        _ada_kernel,
        grid=(depth, n // tn),
        in_specs=[
            pl.BlockSpec((rows, d), lambda l, j: (0, 0)),
            pl.BlockSpec((None, d, tn), lambda l, j: (l, 0, j)),
            pl.BlockSpec((None, 1, tn), lambda l, j: (l, 0, j)),
        ],
        out_specs=pl.BlockSpec((None, rows, tn), lambda l, j: (l, 0, j)),
        out_shape=jax.ShapeDtypeStruct((depth, rows, n), F32),
        compiler_params=_cparams(("arbitrary", "arbitrary")),
        name="ada_mod",
    )(c_pad, ada_w, ada_b.reshape(depth, 1, n))
    return out[:, :bsz].reshape(depth, bsz, 9, d)


def _ffn_kernel(x_ref, mod_ref, g_ref, wg_ref, wu_ref, wo_ref, o_ref, h_scr, acc_scr, *, sub):
    x = x_ref[...]
    h_scr[...] = _adaln(x, g_ref[...], mod_ref, sub).astype(BF16)
    acc_scr[...] = jnp.zeros_like(acc_scr)

    def body(c, carry):
        h = h_scr[...]
        gg = _dot(h, wg_ref[c])
        uu = _dot(h, wu_ref[c])
        a = (gg * jax.nn.sigmoid(gg) * uu).astype(BF16)
        acc_scr[...] += _dot(a, wo_ref[c])
        return carry

    lax.fori_loop(0, wg_ref.shape[0], body, 0)
    gate = mod_ref[3 * sub + 2:3 * sub + 3, :]
    o_ref[...] = x + (0.5 * gate) * acc_scr[...]


def _ffn(x, mod, g, w_in, w_out, sub):
    bsz, seq, d = x.shape
    tm = TOKEN_TILE
    nchunk = D_FF // FF_CHUNK
    wg = w_in[:, :D_FF].astype(BF16).reshape(d, nchunk, FF_CHUNK).transpose(1, 0, 2)
    wu = w_in[:, D_FF:].astype(BF16).reshape(d, nchunk, FF_CHUNK).transpose(1, 0, 2)
    wo = w_out.astype(BF16).reshape(nchunk, FF_CHUNK, d)
    return pl.pallas_call(
        functools.partial(_ffn_kernel, sub=sub),
        grid=(bsz, seq // tm),
        in_specs=[
            pl.BlockSpec((None, tm, d), lambda b, i: (b, i, 0)),
            pl.BlockSpec((None, 9, d), lambda b, i: (b, 0, 0)),
            pl.BlockSpec((1, d), lambda b, i: (0, 0)),
            _resident((nchunk, d, FF_CHUNK), lambda b, i: (0, 0, 0)),
            _resident((nchunk, d, FF_CHUNK), lambda b, i: (0, 0, 0)),
            _resident((nchunk, FF_CHUNK, d), lambda b, i: (0, 0, 0)),
        ],
        out_specs=pl.BlockSpec((None, tm, d), lambda b, i: (b, i, 0)),
        out_shape=jax.ShapeDtypeStruct(x.shape, F32),
        scratch_shapes=[pltpu.VMEM((tm, d), BF16), pltpu.VMEM((tm, d), F32)],
        compiler_params=_cparams(("arbitrary", "arbitrary")),
        name="ffn",
    )(x, mod, g.reshape(1, d), wg, wu, wo)


def _conv_kernel(x_ref, mod_ref, g_ref, win_ref, cw_ref, wout_ref, o_ref, ubuf):
    tm = x_ref.shape[0]

    @pl.when(pl.program_id(1) == 0)
    def _():
        ubuf[0:8, :] = jnp.zeros((8, ubuf.shape[1]), F32)

    x = x_ref[...]
    hb = _adaln(x, g_ref[...], mod_ref, 1).astype(BF16)
    bg = _dot(hb, win_ref[0])
    u = _dot(hb, win_ref[1]) * _dot(hb, win_ref[2])
    ubuf[8:8 + tm, :] = u
    y = cw_ref[2:3, :] * u + cw_ref[1:2, :] * ubuf[7:7 + tm, :] + cw_ref[0:1, :] * ubuf[6:6 + tm, :]
    ubuf[0:8, :] = ubuf[tm:tm + 8, :]
    out = _dot((bg * y).astype(BF16), wout_ref[...])
    o_ref[...] = x + mod_ref[5:6, :] * out


def _conv_mixer(x, mod, g, w_in, conv_w, w_out):
    bsz, seq, d = x.shape
    tm = TOKEN_TILE
    win = w_in.astype(BF16).reshape(d, 3, d).transpose(1, 0, 2)
    return pl.pallas_call(
        _conv_kernel,
        grid=(bsz, seq // tm),
        in_specs=[
            pl.BlockSpec((None, tm, d), lambda b, i: (b, i, 0)),
            pl.BlockSpec((None, 9, d), lambda b, i: (b, 0, 0)),
            pl.BlockSpec((1, d), lambda b, i: (0, 0)),
            _resident((3, d, d), lambda b, i: (0, 0, 0)),
            pl.BlockSpec((3, d), lambda b, i: (0, 0)),
            _resident((d, d), lambda b, i: (0, 0)),
        ],
        out_specs=pl.BlockSpec((None, tm, d), lambda b, i: (b, i, 0)),
        out_shape=jax.ShapeDtypeStruct(x.shape, F32),
        scratch_shapes=[pltpu.VMEM((tm + 8, d), F32)],
        compiler_params=_cparams(("arbitrary", "arbitrary")),
        name="conv_mixer",
    )(x, mod, g.reshape(1, d), win, conv_w, w_out.astype(BF16))


def _head_ms(sq, p_ref, p_first):
    hi, lo = _split_bf16(sq)
    if p_first:
        return _dot(p_ref[...], hi) + _dot(p_ref[...], lo)
    return _dot(hi, p_ref[...]) + _dot(lo, p_ref[...])


def _proj_kernel(x_ref, mod_ref, g_ref, w_ref, wT_ref, p_ref, qg_ref, kg_ref,
                 qT_ref, kvc_ref, ks_ref, vsT_ref, kw_ref, vwT_ref, gT_ref):
    tm = x_ref.shape[0]
    x = x_ref[...]
    hb = _adaln(x, g_ref[...], mod_ref, 1).astype(BF16)

    def col(i, width=KV_DIM):
        return _dot(hb, w_ref[:, i:i + width])

    def colT(i, width=KV_DIM):
        return lax.dot_general(wT_ref[i:i + width, :], hb, (((1,), (1,)), ((), ())),
                               preferred_element_type=F32)

    qg = jnp.concatenate([qg_ref[...]] * (tm // LANES), axis=1)
    for c in range(GROUPS):
        qc = colT(c * KV_DIM)
        qn = qc * lax.rsqrt(_head_ms(qc * qc, p_ref, True) + RMS_EPS) * qg
        qT_ref[c * KV_DIM:(c + 1) * KV_DIM, :] = qn.astype(BF16)
    vsT = colT(HEADS * HEAD_DIM).astype(BF16)
    vwT = colT(HEADS * HEAD_DIM + KV_DIM).astype(BF16)
    for g in range(GROUPS):
        vsT_ref[g] = vsT[g * HEAD_DIM:(g + 1) * HEAD_DIM, :]
        vwT_ref[g] = vwT[g * HEAD_DIM:(g + 1) * HEAD_DIM, :]
    gT_ref[...] = jax.nn.sigmoid(colT(HEADS * HEAD_DIM + 2 * KV_DIM, LANES))

    for kv in range(2):
        for half in range(2):
            kvc_ref[kv, half] = col(kv * KV_DIM + half * LANES, LANES)

    pos = pl.program_id(1) * tm + lax.broadcasted_iota(jnp.int32, (tm, HEAD_DIM), 0)
    lane = lax.broadcasted_iota(jnp.int32, (tm, HEAD_DIM), 1)
    onehot = jnp.where(lane == ((pos // SEL_BLOCK) % FAR_BLOCKS), 1.0, 0.0)
    zeros = jnp.zeros((tm, HEAD_DIM), F32)

    ks = col(2 * KV_DIM)
    ksn = ks * lax.rsqrt(_head_ms(ks * ks, p_ref, False) + RMS_EPS) * kg_ref[0:1, :]
    kw = col(3 * KV_DIM)
    kwn = kw * lax.rsqrt(_head_ms(kw * kw, p_ref, False) + RMS_EPS) * kg_ref[1:2, :]
    for g in range(GROUPS):
        sl = slice(g * HEAD_DIM, (g + 1) * HEAD_DIM)
        ks_ref[g] = jnp.concatenate([ksn[:, sl], onehot], axis=1).astype(BF16)
        kw_ref[g] = jnp.concatenate([kwn[:, sl], zeros], axis=1).astype(BF16)


def _nsa_proj(x, mod, g, w_in, q_gain, k_gain):
    bsz, seq, d = x.shape
    tm = TOKEN_TILE
    nq = HEADS * HEAD_DIM
    q, kc, vc, ks, vs, kw, vw, gl = jnp.split(w_in, [nq + i * KV_DIM for i in range(7)], axis=1)
    gl = gl.reshape(d, GROUPS, 3 * HEADS_PER_GROUP)
    gl = jnp.pad(gl, ((0, 0), (0, GROUPS), (0, 4))).reshape(d, LANES)
    w = jnp.concatenate([kc, vc, ks, kw], axis=1).astype(BF16)
    wT = jnp.concatenate([q, vs, vw, gl], axis=1).T.astype(BF16)
    hid = np.arange(KV_DIM) // HEAD_DIM
    pmat = jnp.asarray((hid[:, None] == hid[None, :]) / HEAD_DIM, BF16)
    qg = jnp.tile(q_gain, HEADS_PER_GROUP) * HEAD_DIM ** -0.5
    qg = jnp.broadcast_to(qg[:, None], (KV_DIM, LANES))
    kg = jnp.stack([jnp.tile(k_gain[1], GROUPS), jnp.tile(k_gain[2], GROUPS)])
    kv_spec = pl.BlockSpec((None, GROUPS, tm, LANES), lambda b, i: (b, 0, i, 0))
    vT_spec = pl.BlockSpec((None, GROUPS, HEAD_DIM, tm), lambda b, i: (b, 0, 0, i))
    kv_shape = jax.ShapeDtypeStruct((bsz, GROUPS, seq, LANES), BF16)
    vT_shape = jax.ShapeDtypeStruct((bsz, GROUPS, HEAD_DIM, seq), BF16)
    return pl.pallas_call(
        _proj_kernel,
        grid=(bsz, seq // tm),
        in_specs=[
            pl.BlockSpec((None, tm, d), lambda b, i: (b, i, 0)),
            pl.BlockSpec((None, 9, d), lambda b, i: (b, 0, 0)),
            pl.BlockSpec((1, d), lambda b, i: (0, 0)),
            _resident(w.shape, lambda b, i: (0, 0)),
            _resident(wT.shape, lambda b, i: (0, 0)),
            pl.BlockSpec((KV_DIM, KV_DIM), lambda b, i: (0, 0)),
            pl.BlockSpec((KV_DIM, LANES), lambda b, i: (0, 0)),
            pl.BlockSpec((2, KV_DIM), lambda b, i: (0, 0)),
        ],
        out_specs=[
            pl.BlockSpec((None, nq, tm), lambda b, i: (b, 0, i)),
            pl.BlockSpec((2, 2, None, tm, LANES), lambda b, i: (0, 0, b, i, 0)),
            kv_spec, vT_spec, kv_spec, vT_spec,
            pl.BlockSpec((None, LANES, tm), lambda b, i: (b, 0, i)),
        ],
        out_shape=[
            jax.ShapeDtypeStruct((bsz, nq, seq), BF16),
            jax.ShapeDtypeStruct((2, 2, bsz, seq, LANES), F32),
            kv_shape, vT_shape, kv_shape, vT_shape,
            jax.ShapeDtypeStruct((bsz, LANES, seq), F32),
        ],
        compiler_params=_cparams(("arbitrary", "arbitrary")),
        name="nsa_proj",
    )(x, mod, g.reshape(1, d), w, wT, pmat, qg, kg)


def _gelu_tanh(x):
    return 0.5 * x * (1.0 + jnp.tanh(math.sqrt(2.0 / math.pi) * (x + 0.044715 * (x * x * x))))


def _compress_kernel(kv_ref, pos_ref, w1_ref, w2_ref, kg_ref, o_ref, acc_lo, acc_hi, *, is_key):
    nc = acc_lo.shape[1]
    half = CMP_BLOCK // 2
    acc_lo[...] = jnp.zeros_like(acc_lo)
    acc_hi[...] = jnp.zeros_like(acc_hi)
    for l in range(half):
        xl = [kv_ref[j, pl.ds(l, nc, stride=CMP_STRIDE), :] for j in range(2)]
        for g in range(GROUPS):
            xg = xl[g // 2][:, (g % 2) * HEAD_DIM:(g % 2 + 1) * HEAD_DIM]
            lo = (xg + pos_ref[l:l + 1, :]).astype(BF16)
            hi = (xg + pos_ref[half + l:half + l + 1, :]).astype(BF16)
            acc_lo[g] += _dot(lo, w1_ref[l * HEAD_DIM:(l + 1) * HEAD_DIM, :])
            acc_hi[g] += _dot(hi, w1_ref[(half + l) * HEAD_DIM:(half + l + 1) * HEAD_DIM, :])
    row = lax.broadcasted_iota(jnp.int32, (nc, CMP_HIDDEN), 0)
    for g in range(GROUPS):
        nxt = jnp.where(row < nc - 1, pltpu.roll(acc_hi[g], nc - 1, 0), 0.0)
        hdn = _gelu_tanh(acc_lo[g] + nxt).astype(BF16)
        if is_key:
            out = _dot(hdn, w2_ref[...])
            ms = jnp.sum(out * out, axis=-1, keepdims=True) * (1.0 / HEAD_DIM)
            o_ref[g] = (out * lax.rsqrt(ms + RMS_EPS) * kg_ref[...]).astype(BF16)
        else:
            outT = lax.dot_general(w2_ref[...], hdn, (((1,), (1,)), ((), ())),
                                   preferred_element_type=F32)
            o_ref[g] = outT.astype(BF16)


def _compress(kv, pos, w1, w2, k_gain, is_key):
    _, bsz, seq, _ = kv.shape
    nc = seq // CMP_STRIDE
    kg = jnp.pad(k_gain, (0, LANES - HEAD_DIM)).reshape(1, LANES)
    if is_key:
        w2p = jnp.pad(w2, ((0, 0), (0, LANES - HEAD_DIM))).astype(BF16)
    else:
        w2p = w2.T.astype(BF16)
    if is_key:
        out_spec = pl.BlockSpec((None, GROUPS, nc, LANES), lambda b: (b, 0, 0, 0))
        out_shape = jax.ShapeDtypeStruct((bsz, GROUPS, nc, LANES), BF16)
    else:
        out_spec = pl.BlockSpec((None, GROUPS, HEAD_DIM, nc), lambda b: (b, 0, 0, 0))
        out_shape = jax.ShapeDtypeStruct((bsz, GROUPS, HEAD_DIM, nc), BF16)
    return pl.pallas_call(
        functools.partial(_compress_kernel, is_key=is_key),
        grid=(bsz,),
        in_specs=[
            pl.BlockSpec((2, None, seq, LANES), lambda b: (0, b, 0, 0)),
            pl.BlockSpec((CMP_BLOCK, HEAD_DIM), lambda b: (0, 0)),
            pl.BlockSpec((CMP_BLOCK * HEAD_DIM, CMP_HIDDEN), lambda b: (0, 0)),
            pl.BlockSpec(w2p.shape, lambda b: (0, 0)),
            pl.BlockSpec((1, LANES), lambda b: (0, 0)),
        ],
        out_specs=out_spec,
        out_shape=out_shape,
        scratch_shapes=[pltpu.VMEM((GROUPS, nc, CMP_HIDDEN), F32),
                        pltpu.VMEM((GROUPS, nc, CMP_HIDDEN), F32)],
        compiler_params=_cparams(("arbitrary",)),
        name="compress_k" if is_key else "compress_v",
    )(kv, pos, w1.astype(BF16), w2p, kg)


def _t5_bucket_np(dist):
    n = np.maximum(dist, 0)
    max_exact = REL_BUCKETS // 2
    nf = np.maximum(n, 1).astype(np.float32)
    large = max_exact + (np.log(nf / max_exact) / math.log(REL_MAX_DIST / max_exact)
                         * (REL_BUCKETS - max_exact)).astype(np.int32)
    large = np.minimum(large, REL_BUCKETS - 1)
    return np.where(n < max_exact, n, large)


def _bias_tables(rel_bias):
    rb = rel_bias.astype(F32).T
    rel = (rb - rb[:, REL_BUCKETS - 1:]).reshape(GROUPS, HEADS_PER_GROUP, REL_BUCKETS)
    ql = np.arange(Q_BLOCK)[None, :]
    kl = np.arange(Q_BLOCK)[:, None]

    def table(dist):
        vals = rel[:, :, _t5_bucket_np(dist)]
        vals = jnp.where(jnp.asarray(dist >= 0), vals, NEG)
        return vals.transpose(0, 2, 1, 3).reshape(GROUPS, dist.shape[0], QL)

    near = jnp.stack([table(ql - kl + Q_BLOCK), table(ql - kl)], axis=1)
    win0 = np.where(ql - kl + WINDOW < WINDOW, 0.0, NEG).astype(np.float32)
    win0 = jnp.asarray(np.tile(win0, (1, HEADS_PER_GROUP)))
    r = np.arange(16)[:, None]
    cmp = table(ql - CMP_STRIDE * (r - 8) - (CMP_BLOCK - 1))
    return near, win0, cmp


def _attn_kernel(qT_ref, kc_ref, vcT_ref, ks_ref, vsT_ref, kw_ref, vwT_ref, gate_ref,
                 near_ref, win0_ref, cmpt_ref, ov_ref, o_ref,
                 rhs_z, rhs_f, selb_full, selb_far, sc_scr, sw_scr, acc_scr, m_scr, l_scr):
    qi = pl.program_id(2)
    t0 = pl.multiple_of(qi * Q_BLOCK, Q_BLOCK)
    nc = kc_ref.shape[0]
    ns = ov_ref.shape[0]

    q4 = qT_ref[...]
    qt = jnp.concatenate([q4[h * HEAD_DIM:(h + 1) * HEAD_DIM, :] for h in range(HEADS_PER_GROUP)], axis=1)
    rhs_z[0:HEAD_DIM, :] = qt
    rhs_z[HEAD_DIM:, :] = jnp.zeros((LANES - HEAD_DIM, QL), BF16)
    rhs_f[0:HEAD_DIM, :] = qt
    rhs_f[HEAD_DIM:, :] = jnp.zeros((LANES - HEAD_DIM, QL), BF16)

    def tile4(v):
        return jnp.concatenate([v] * HEADS_PER_GROUP, axis=1)

    sc_scr[...] = _dot(kc_ref[...], rhs_z[...])

    @pl.when(qi == 0)
    def _():
        sc_scr[0:8, :] += cmpt_ref[8:16, :]

    @pl.when(qi > 0)
    def _():
        st = pl.multiple_of(8 * qi - 8, 8)
        sc_scr[pl.ds(st, 16), :] += cmpt_ref[...]

    row = lax.broadcasted_iota(jnp.int32, (nc, QL), 0)
    s = sc_scr[...] + jnp.where(row < 8 * qi + 8, 0.0, NEG)
    m = jnp.max(s, axis=0, keepdims=True)
    p = jnp.exp(s - m)
    l = jnp.sum(p, axis=0, keepdims=True)
    inv = jnp.where(m > 0.1 * NEG, 1.0 / l, 0.0)
    o_c = _dot(vcT_ref[...], p.astype(BF16)) * inv
    pn = p * inv
    psum = (pn[:, 0:Q_BLOCK] + pn[:, Q_BLOCK:2 * Q_BLOCK]
            + pn[:, 2 * Q_BLOCK:3 * Q_BLOCK] + pn[:, 3 * Q_BLOCK:4 * Q_BLOCK])
    phi, plo = _split_bf16(psum)
    imp = _dot(ov_ref[...], phi) + _dot(ov_ref[...], plo)

    jrow = lax.broadcasted_iota(jnp.int32, (ns, Q_BLOCK), 0)
    qlane = lax.broadcasted_iota(jnp.int32, (ns, Q_BLOCK), 1)
    cur = 2 * qi + jnp.where(qlane >= SEL_BLOCK, 1, 0)
    future = jrow > cur
    forced = (jrow == 0) | (jrow == cur) | (jrow == cur - 1)
    v = jnp.where(future, NEG, imp)
    v = jnp.where(forced, FORCE, v)
    rowf = jrow.astype(F32)
    picked = jnp.zeros((ns, Q_BLOCK), F32)
    for _ in range(SEL_TOP_N):
        mx = jnp.max(v, axis=0, keepdims=True)
        idx = jnp.min(jnp.where(v == mx, rowf, float(ns)), axis=0, keepdims=True)
        hit = rowf == idx
        picked = jnp.where(hit, 1.0, picked)
        v = jnp.where(hit, -jnp.inf, v)
    sb = jnp.where((picked > 0.0) & jnp.logical_not(future), 0.0, NEG)
    selb_full[...] = tile4(sb).astype(BF16)
    selb_far[...] = tile4(jnp.where(jrow < 2 * qi - 2, sb, NEG)).astype(BF16)

    def online(s, vT):
        m_old = m_scr[...]
        m_new = jnp.maximum(m_old, jnp.max(s, axis=0, keepdims=True))
        alpha = jnp.exp(m_old - m_new)
        p = jnp.exp(s - m_new)
        l_scr[...] = alpha * l_scr[...] + jnp.sum(p, axis=0, keepdims=True)
        acc_scr[...] = alpha * acc_scr[...] + _dot(vT, p.astype(BF16))
        m_scr[...] = m_new

    s = _dot(ks_ref[pl.ds(t0, Q_BLOCK), :], rhs_z[...]) + near_ref[1]
    m = jnp.max(s, axis=0, keepdims=True)
    p = jnp.exp(s - m)
    m_scr[...] = m
    l_scr[...] = jnp.sum(p, axis=0, keepdims=True)
    acc_scr[...] = _dot(vsT_ref[:, pl.ds(t0, Q_BLOCK)], p.astype(BF16))

    @pl.when(qi >= 1)
    def _():
        tp = pl.multiple_of(t0 - Q_BLOCK, Q_BLOCK)
        r0 = pl.multiple_of(((2 * qi - 2) // FAR_BLOCKS) * FAR_BLOCKS, FAR_BLOCKS)
        rhs_f[HEAD_DIM:HEAD_DIM + FAR_BLOCKS, :] = selb_full[pl.ds(r0, FAR_BLOCKS), :]
        s = _dot(ks_ref[pl.ds(tp, Q_BLOCK), :], rhs_f[...]) + near_ref[0]
        online(s, vsT_ref[:, pl.ds(tp, Q_BLOCK)])

    def far_body(jt, carry):
        r0 = pl.multiple_of(jt * FAR_BLOCKS, FAR_BLOCKS)
        k0 = pl.multiple_of(jt * FAR_KEYS, FAR_KEYS)
        rhs_f[HEAD_DIM:HEAD_DIM + FAR_BLOCKS, :] = selb_far[pl.ds(r0, FAR_BLOCKS), :]
        s = _dot(ks_ref[pl.ds(k0, FAR_KEYS), :], rhs_f[...])
        online(s, vsT_ref[:, pl.ds(k0, FAR_KEYS)])
        return carry

    n_far = jnp.maximum(2 * qi - 2 + FAR_BLOCKS - 1, 0) // FAR_BLOCKS
    lax.fori_loop(0, n_far, far_body, 0)
    o_s = acc_scr[...] / l_scr[...]

    n_win = WINDOW // Q_BLOCK + 1
    starts = []
    for a in range(n_win):
        start = t0 - Q_BLOCK * (n_win - 1 - a)
        st = pl.multiple_of(jnp.maximum(start, 0), Q_BLOCK)
        starts.append(st)
        s = _dot(kw_ref[pl.ds(st, Q_BLOCK), :], rhs_z[...])
        if a == 0:
            s = s + win0_ref[...]
        elif a == n_win - 2:
            s = s + near_ref[0]
        elif a == n_win - 1:
            s = s + near_ref[1]
        sw_scr[a * Q_BLOCK:(a + 1) * Q_BLOCK, :] = jnp.where(start >= 0, s, NEG)
    s = sw_scr[...]
    m = jnp.max(s, axis=0, keepdims=True)
    p = jnp.exp(s - m)
    l = jnp.sum(p, axis=0, keepdims=True)
    pb = p.astype(BF16)
    o_w = _dot(vwT_ref[:, pl.ds(starts[0], Q_BLOCK)], pb[0:Q_BLOCK, :])
    for a in range(1, n_win):
        o_w += _dot(vwT_ref[:, pl.ds(starts[a], Q_BLOCK)], pb[a * Q_BLOCK:(a + 1) * Q_BLOCK, :])
    o_w = o_w / l

    for h in range(HEADS_PER_GROUP):
        sl = slice(h * Q_BLOCK, (h + 1) * Q_BLOCK)
        o_h = (gate_ref[3 * h:3 * h + 1, :] * o_c[:, sl]
               + gate_ref[3 * h + 1:3 * h + 2, :] * o_s[:, sl]
               + gate_ref[3 * h + 2:3 * h + 3, :] * o_w[:, sl])
        o_ref[h * HEAD_DIM:(h + 1) * HEAD_DIM, :] = o_h.astype(BF16)


def _nsa_attention(qT, kcmp, vcmpT, ks, vsT, kw, vwT, gT, rel_bias):
    bsz, _, seq = qT.shape
    assert seq % FAR_KEYS == 0
    nq = seq // Q_BLOCK
    nc = seq // CMP_STRIDE
    ns = seq // SEL_BLOCK
    near, win0, cmpt = _bias_tables(rel_bias)
    cs = np.arange(nc) * CMP_STRIDE
    ss = np.arange(ns) * SEL_BLOCK
    ov = (cs[None, :] <= ss[:, None] + SEL_BLOCK - 1) & (cs[None, :] + CMP_BLOCK - 1 >= ss[:, None])
    ov = jnp.asarray(ov, BF16)
    gates = gT.reshape(bsz, 2 * GROUPS, 16, seq)
    per_bg = lambda b, g, i: (b, g, 0, 0)
    return pl.pallas_call(
        _attn_kernel,
        grid=(bsz, GROUPS, nq),
        in_specs=[
            pl.BlockSpec((None, KV_DIM, Q_BLOCK), lambda b, g, i: (b, g, i)),
            pl.BlockSpec((None, None, nc, LANES), per_bg),
            pl.BlockSpec((None, None, HEAD_DIM, nc), per_bg),
            pl.BlockSpec((None, None, seq, LANES), per_bg),
            pl.BlockSpec((None, None, HEAD_DIM, seq), per_bg),
            pl.BlockSpec((None, None, seq, LANES), per_bg),
            pl.BlockSpec((None, None, HEAD_DIM, seq), per_bg),
            pl.BlockSpec((None, None, 16, Q_BLOCK), lambda b, g, i: (b, g, 0, i)),
            pl.BlockSpec((None, 2, Q_BLOCK, QL), lambda b, g, i: (g, 0, 0, 0)),
            pl.BlockSpec((Q_BLOCK, QL), lambda b, g, i: (0, 0)),
            pl.BlockSpec((None, 16, QL), lambda b, g, i: (g, 0, 0)),
            pl.BlockSpec((ns, nc), lambda b, g, i: (0, 0)),
        ],
        out_specs=pl.BlockSpec((None, KV_DIM, Q_BLOCK), lambda b, g, i: (b, g, i)),
        out_shape=jax.ShapeDtypeStruct((bsz, HEADS * HEAD_DIM, seq), BF16),
        scratch_shapes=[
            pltpu.VMEM((LANES, QL), BF16),
            pltpu.VMEM((LANES, QL), BF16),
            pltpu.VMEM((ns, QL), BF16),
            pltpu.VMEM((ns, QL), BF16),
            pltpu.VMEM((nc, QL), F32),
            pltpu.VMEM((WINDOW + Q_BLOCK, QL), F32),
            pltpu.VMEM((HEAD_DIM, QL), F32),
            pltpu.VMEM((1, QL), F32),
            pltpu.VMEM((1, QL), F32),
        ],
        compiler_params=_cparams(("arbitrary", "arbitrary", "arbitrary")),
        name="nsa_attention",
    )(qT, kcmp, vcmpT, ks, vsT, kw, vwT, gates, near, win0, cmpt, ov)


def _oproj_kernel(x_ref, mod_ref, oT_ref, w_ref, o_ref):
    y = _dot(oT_ref[...].T, w_ref[...])
    o_ref[...] = x_ref[...] + mod_ref[5:6, :] * y


def _nsa_out(x, mod, oT, w_out):
    bsz, seq, d = x.shape
    tm = TOKEN_TILE
    return pl.pallas_call(
        _oproj_kernel,
        grid=(bsz, seq // tm),
        in_specs=[
            pl.BlockSpec((None, tm, d), lambda b, i: (b, i, 0)),
            pl.BlockSpec((None, 9, d), lambda b, i: (b, 0, 0)),
            pl.BlockSpec((None, d, tm), lambda b, i: (b, 0, i)),
            _resident((d, d), lambda b, i: (0, 0)),
        ],
        out_specs=pl.BlockSpec((None, tm, d), lambda b, i: (b, i, 0)),
        out_shape=jax.ShapeDtypeStruct(x.shape, F32),
        compiler_params=_cparams(("arbitrary", "arbitrary")),
        name="nsa_out",
    )(x, mod, oT, w_out.astype(BF16))


def _nsa_mixer(x, mod, g, w_in, cmp_pos, cmp_w1, cmp_w2, q_gain, k_gain, w_out, rel_bias):
    qT, kvc, ks, vsT, kw, vwT, gT = _nsa_proj(x, mod, g, w_in, q_gain, k_gain)
    kcmp = _compress(kvc[0], cmp_pos, cmp_w1[0], cmp_w2[0], k_gain[0], True)
    vcmpT = _compress(kvc[1], cmp_pos, cmp_w1[1], cmp_w2[1], k_gain[0], False)
    oT = _nsa_attention(qT, kcmp, vcmpT, ks, vsT, kw, vwT, gT, rel_bias)
    return _nsa_out(x, mod, oT, w_out)


def kernel(x, c, norm_g, ada_w, ada_b, ffn_w_in, ffn_w_out, conv_w_in, conv_w, conv_w_out,
           nsa_w_in, nsa_cmp_pos, nsa_cmp_w1, nsa_cmp_w2, nsa_q_gain, nsa_k_gain, nsa_w_out,
           rel_bias):
    depth = ada_w.shape[0]
    mods = _ada_mod(c, ada_w, ada_b)
    for i in range(depth):
        mod = mods[i]
        x = _ffn(x, mod, norm_g[i, 0], ffn_w_in[i, 0], ffn_w_out[i, 0], 0)
        j = i // 2
        if i % 2 == 0:
            x = _conv_mixer(x, mod, norm_g[i, 1], conv_w_in[j], conv_w[j], conv_w_out[j])
        else:
            x = _nsa_mixer(x, mod, norm_g[i, 1], nsa_w_in[j], nsa_cmp_pos[j], nsa_cmp_w1[j],
                           nsa_cmp_w2[j], nsa_q_gain[j], nsa_k_gain[j], nsa_w_out[j], rel_bias)
        x = _ffn(x, mod, norm_g[i, 2], ffn_w_in[i, 1], ffn_w_out[i, 1], 2)
    return x
```

```python
import functools
import math

import numpy as np
import jax
import jax.numpy as jnp
from jax import lax
from jax.experimental import pallas as pl
from jax.experimental.pallas import tpu as pltpu

F32 = jnp.float32
BF16 = jnp.bfloat16

D_MODEL = 1024
D_FF = 2816
HEADS = 16
HEAD_DIM = 64
GROUPS = 4
HEADS_PER_GROUP = 4
KV_DIM = GROUPS * HEAD_DIM
CMP_BLOCK = 32
CMP_STRIDE = 16
CMP_HIDDEN = 256
SEL_BLOCK = 64
SEL_TOP_N = 16
WINDOW = 512
Q_BLOCK = 128
REL_BUCKETS = 32
REL_MAX_DIST = 128
RMS_EPS = 1e-6
NEG = -1e30
FORCE = 1e9

LANES = 128
QL = HEADS_PER_GROUP * Q_BLOCK
FAR_KEYS = 1024
FAR_BLOCKS = FAR_KEYS // SEL_BLOCK
FAR_SUB = 256
V_ROWS = HEAD_DIM + 16
LOG2E = math.log2(math.e)
FF_CHUNK = 256
TOKEN_TILE = 512
VMEM_LIMIT = 56 * 1024 * 1024


def _dot(a, b):
    return jnp.dot(a, b, preferred_element_type=F32)


def _cparams(sem):
    return pltpu.CompilerParams(dimension_semantics=sem, vmem_limit_bytes=VMEM_LIMIT)


def _resident(shape, index_map):
    return pl.BlockSpec(shape, index_map, pipeline_mode=pl.Buffered(1))


def _adaln(x, g, mod_ref, sub):
    ms = jnp.mean(x * x, axis=-1, keepdims=True)
    shift = mod_ref[3 * sub:3 * sub + 1, :]
    scale = mod_ref[3 * sub + 1:3 * sub + 2, :]
    return (x * lax.rsqrt(ms + RMS_EPS) * g) * (1.0 + scale) + shift


def _split_bf16(v):
    hi = v.astype(BF16)
    lo = (v - hi.astype(F32)).astype(BF16)
    return hi, lo


def _ada_kernel(c_ref, w_ref, b_ref, o_ref):
    c = c_ref[...]
    cond = c * jax.nn.sigmoid(c)
    chi, clo = _split_bf16(cond)
    whi, wlo = _split_bf16(w_ref[...])
    o_ref[...] = _dot(chi, whi) + _dot(chi, wlo) + _dot(clo, whi) + b_ref[...]


def _ada_mod(c, ada_w, ada_b):
    depth, d, n = ada_w.shape
    bsz = c.shape[0]
    rows = 8
    tn = 1024
    c_pad = jnp.zeros((rows, d), F32).at[:bsz].set(c)
    out = pl.pallas_call(
        _ada_kernel,
        grid=(depth, n // tn),
        in_specs=[
            pl.BlockSpec((rows, d), lambda l, j: (0, 0)),
            pl.BlockSpec((None, d, tn), lambda l, j: (l, 0, j)),
            pl.BlockSpec((None, 1, tn), lambda l, j: (l, 0, j)),
        ],
        out_specs=pl.BlockSpec((None, rows, tn), lambda l, j: (l, 0, j)),
        out_shape=jax.ShapeDtypeStruct((depth, rows, n), F32),
        compiler_params=_cparams(("arbitrary", "arbitrary")),
        name="ada_mod",
    )(c_pad, ada_w, ada_b.reshape(depth, 1, n))
    return out[:, :bsz].reshape(depth, bsz, 9, d)


def _ffn_kernel(x_ref, mod_ref, g_ref, wg_ref, wu_ref, wo_ref, o_ref, h_scr, acc_scr, *, sub):
    x = x_ref[...]
    h_scr[...] = _adaln(x, g_ref[...], mod_ref, sub).astype(BF16)
    acc_scr[...] = jnp.zeros_like(acc_scr)

    def body(c, carry):
        h = h_scr[...]
        gg = _dot(h, wg_ref[c])
        uu = _dot(h, wu_ref[c])
        a = (gg * jax.nn.sigmoid(gg) * uu).astype(BF16)
        acc_scr[...] += _dot(a, wo_ref[c])
        return carry

    lax.fori_loop(0, wg_ref.shape[0], body, 0)
    gate = mod_ref[3 * sub + 2:3 * sub + 3, :]
    o_ref[...] = x + (0.5 * gate) * acc_scr[...]


def _ffn(x, mod, g, w_in, w_out, sub):
    bsz, seq, d = x.shape
    tm = TOKEN_TILE
    nchunk = D_FF // FF_CHUNK
    wg = w_in[:, :D_FF].astype(BF16).reshape(d, nchunk, FF_CHUNK).transpose(1, 0, 2)
    wu = w_in[:, D_FF:].astype(BF16).reshape(d, nchunk, FF_CHUNK).transpose(1, 0, 2)
    wo = w_out.astype(BF16).reshape(nchunk, FF_CHUNK, d)
    return pl.pallas_call(
        functools.partial(_ffn_kernel, sub=sub),
        grid=(bsz, seq // tm),
        in_specs=[
            pl.BlockSpec((None, tm, d), lambda b, i: (b, i, 0)),
            pl.BlockSpec((None, 9, d), lambda b, i: (b, 0, 0)),
            pl.BlockSpec((1, d), lambda b, i: (0, 0)),
            _resident((nchunk, d, FF_CHUNK), lambda b, i: (0, 0, 0)),
            _resident((nchunk, d, FF_CHUNK), lambda b, i: (0, 0, 0)),
            _resident((nchunk, FF_CHUNK, d), lambda b, i: (0, 0, 0)),
        ],
        out_specs=pl.BlockSpec((None, tm, d), lambda b, i: (b, i, 0)),
        out_shape=jax.ShapeDtypeStruct(x.shape, F32),
        scratch_shapes=[pltpu.VMEM((tm, d), BF16), pltpu.VMEM((tm, d), F32)],
        compiler_params=_cparams(("arbitrary", "arbitrary")),
        name="ffn",
    )(x, mod, g.reshape(1, d), wg, wu, wo)


def _conv_kernel(x_ref, mod_ref, g_ref, win_ref, cw_ref, wout_ref, o_ref, ubuf):
    tm = x_ref.shape[0]

    @pl.when(pl.program_id(1) == 0)
    def _():
        ubuf[0:8, :] = jnp.zeros((8, ubuf.shape[1]), F32)

    x = x_ref[...]
    hb = _adaln(x, g_ref[...], mod_ref, 1).astype(BF16)
    bg = _dot(hb, win_ref[0])
    u = _dot(hb, win_ref[1]) * _dot(hb, win_ref[2])
    ubuf[8:8 + tm, :] = u
    y = cw_ref[2:3, :] * u + cw_ref[1:2, :] * ubuf[7:7 + tm, :] + cw_ref[0:1, :] * ubuf[6:6 + tm, :]
    ubuf[0:8, :] = ubuf[tm:tm + 8, :]
    out = _dot((bg * y).astype(BF16), wout_ref[...])
    o_ref[...] = x + mod_ref[5:6, :] * out


def _conv_mixer(x, mod, g, w_in, conv_w, w_out):
    bsz, seq, d = x.shape
    tm = TOKEN_TILE
    win = w_in.astype(BF16).reshape(d, 3, d).transpose(1, 0, 2)
    return pl.pallas_call(
        _conv_kernel,
        grid=(bsz, seq // tm),
        in_specs=[
            pl.BlockSpec((None, tm, d), lambda b, i: (b, i, 0)),
            pl.BlockSpec((None, 9, d), lambda b, i: (b, 0, 0)),
            pl.BlockSpec((1, d), lambda b, i: (0, 0)),
            _resident((3, d, d), lambda b, i: (0, 0, 0)),
            pl.BlockSpec((3, d), lambda b, i: (0, 0)),
            _resident((d, d), lambda b, i: (0, 0)),
        ],
        out_specs=pl.BlockSpec((None, tm, d), lambda b, i: (b, i, 0)),
        out_shape=jax.ShapeDtypeStruct(x.shape, F32),
        scratch_shapes=[pltpu.VMEM((tm + 8, d), F32)],
        compiler_params=_cparams(("arbitrary", "arbitrary")),
        name="conv_mixer",
    )(x, mod, g.reshape(1, d), win, conv_w, w_out.astype(BF16))


def _head_ms(sq, p_ref, p_first):
    hi, lo = _split_bf16(sq)
    if p_first:
        return _dot(p_ref[...], hi) + _dot(p_ref[...], lo)
    return _dot(hi, p_ref[...]) + _dot(lo, p_ref[...])


def _proj_kernel(x_ref, mod_ref, g_ref, w_ref, wT_ref, p_ref, qg_ref, kg_ref,
                 qT_ref, kvc_ref, ks_ref, vsT_ref, kw_ref, vwT_ref, gT_ref):
    tm = x_ref.shape[0]
    x = x_ref[...]
    hb = _adaln(x, g_ref[...], mod_ref, 1).astype(BF16)

    def col(i, width=KV_DIM):
        return _dot(hb, w_ref[:, i:i + width])

    def colT(i, width=KV_DIM):
        return lax.dot_general(wT_ref[i:i + width, :], hb, (((1,), (1,)), ((), ())),
                               preferred_element_type=F32)

    qg = jnp.concatenate([qg_ref[...]] * (tm // LANES), axis=1)
    for c in range(GROUPS):
        qc = colT(c * KV_DIM)
        qn = qc * lax.rsqrt(_head_ms(qc * qc, p_ref, True) + RMS_EPS) * qg
        qT_ref[c * KV_DIM:(c + 1) * KV_DIM, :] = qn.astype(BF16)
    vsT = colT(HEADS * HEAD_DIM).astype(BF16)
    vwT = colT(HEADS * HEAD_DIM + KV_DIM).astype(BF16)
    rid = lax.broadcasted_iota(jnp.int32, (V_ROWS - HEAD_DIM, tm), 0)
    ones_rows = jnp.where(rid == 0, 1.0, 0.0).astype(BF16)
    for g in range(GROUPS):
        vsT_ref[g] = jnp.concatenate([vsT[g * HEAD_DIM:(g + 1) * HEAD_DIM, :], ones_rows], axis=0)
        vwT_ref[g] = jnp.concatenate([vwT[g * HEAD_DIM:(g + 1) * HEAD_DIM, :], ones_rows], axis=0)
    gT_ref[...] = jax.nn.sigmoid(colT(HEADS * HEAD_DIM + 2 * KV_DIM, LANES))

    for kv in range(2):
        for half in range(2):
            kvc_ref[kv, half] = col(kv * KV_DIM + half * LANES, LANES)

    pos = pl.program_id(1) * tm + lax.broadcasted_iota(jnp.int32, (tm, HEAD_DIM), 0)
    lane = lax.broadcasted_iota(jnp.int32, (tm, HEAD_DIM), 1)
    onehot = jnp.where(lane == ((pos // SEL_BLOCK) % FAR_BLOCKS), 1.0, 0.0)
    zeros = jnp.zeros((tm, HEAD_DIM), F32)

    ks = col(2 * KV_DIM)
    ksn = ks * lax.rsqrt(_head_ms(ks * ks, p_ref, False) + RMS_EPS) * kg_ref[0:1, :]
    kw = col(3 * KV_DIM)
    kwn = kw * lax.rsqrt(_head_ms(kw * kw, p_ref, False) + RMS_EPS) * kg_ref[1:2, :]
    for g in range(GROUPS):
        sl = slice(g * HEAD_DIM, (g + 1) * HEAD_DIM)
        ks_ref[g] = jnp.concatenate([ksn[:, sl], onehot], axis=1).astype(BF16)
        kw_ref[g] = jnp.concatenate([kwn[:, sl], zeros], axis=1).astype(BF16)


def _nsa_proj(x, mod, g, w_in, q_gain, k_gain):
    bsz, seq, d = x.shape
    tm = TOKEN_TILE
    nq = HEADS * HEAD_DIM
    q, kc, vc, ks, vs, kw, vw, gl = jnp.split(w_in, [nq + i * KV_DIM for i in range(7)], axis=1)
    gl = gl.reshape(d, GROUPS, 3 * HEADS_PER_GROUP)
    gl = jnp.pad(gl, ((0, 0), (0, GROUPS), (0, 4))).reshape(d, LANES)
    w = jnp.concatenate([kc, vc, ks, kw], axis=1).astype(BF16)
    wT = jnp.concatenate([q, vs, vw, gl], axis=1).T.astype(BF16)
    hid = np.arange(KV_DIM) // HEAD_DIM
    pmat = jnp.asarray((hid[:, None] == hid[None, :]) / HEAD_DIM, BF16)
    qg = jnp.tile(q_gain, HEADS_PER_GROUP) * (HEAD_DIM ** -0.5 * LOG2E)
    qg = jnp.broadcast_to(qg[:, None], (KV_DIM, LANES))
    kg = jnp.stack([jnp.tile(k_gain[1], GROUPS), jnp.tile(k_gain[2], GROUPS)])
    kv_spec = pl.BlockSpec((None, GROUPS, tm, LANES), lambda b, i: (b, 0, i, 0))
    vT_spec = pl.BlockSpec((None, GROUPS, V_ROWS, tm), lambda b, i: (b, 0, 0, i))
    kv_shape = jax.ShapeDtypeStruct((bsz, GROUPS, seq, LANES), BF16)
    vT_shape = jax.ShapeDtypeStruct((bsz, GROUPS, V_ROWS, seq), BF16)
    return pl.pallas_call(
        _proj_kernel,
        grid=(bsz, seq // tm),
        in_specs=[
            pl.BlockSpec((None, tm, d), lambda b, i: (b, i, 0)),
            pl.BlockSpec((None, 9, d), lambda b, i: (b, 0, 0)),
            pl.BlockSpec((1, d), lambda b, i: (0, 0)),
            _resident(w.shape, lambda b, i: (0, 0)),
            _resident(wT.shape, lambda b, i: (0, 0)),
            pl.BlockSpec((KV_DIM, KV_DIM), lambda b, i: (0, 0)),
            pl.BlockSpec((KV_DIM, LANES), lambda b, i: (0, 0)),
            pl.BlockSpec((2, KV_DIM), lambda b, i: (0, 0)),
        ],
        out_specs=[
            pl.BlockSpec((None, nq, tm), lambda b, i: (b, 0, i)),
            pl.BlockSpec((2, 2, None, tm, LANES), lambda b, i: (0, 0, b, i, 0)),
            kv_spec, vT_spec, kv_spec, vT_spec,
            pl.BlockSpec((None, LANES, tm), lambda b, i: (b, 0, i)),
        ],
        out_shape=[
            jax.ShapeDtypeStruct((bsz, nq, seq), BF16),
            jax.ShapeDtypeStruct((2, 2, bsz, seq, LANES), F32),
            kv_shape, vT_shape, kv_shape, vT_shape,
            jax.ShapeDtypeStruct((bsz, LANES, seq), F32),
        ],
        compiler_params=_cparams(("arbitrary", "arbitrary")),
        name="nsa_proj",
    )(x, mod, g.reshape(1, d), w, wT, pmat, qg, kg)


def _gelu_tanh(x):
    return 0.5 * x * (1.0 + jnp.tanh(math.sqrt(2.0 / math.pi) * (x + 0.044715 * (x * x * x))))


def _compress_kernel(kv_ref, pos_ref, w1_ref, w2_ref, kg_ref, o_ref, acc_lo, acc_hi, *, is_key):
    nc = acc_lo.shape[1]
    half = CMP_BLOCK // 2
    acc_lo[...] = jnp.zeros_like(acc_lo)
    acc_hi[...] = jnp.zeros_like(acc_hi)
    for l in range(half):
        xl = [kv_ref[j, pl.ds(l, nc, stride=CMP_STRIDE), :] for j in range(2)]
        for g in range(GROUPS):
            xg = xl[g // 2][:, (g % 2) * HEAD_DIM:(g % 2 + 1) * HEAD_DIM]
            lo = (xg + pos_ref[l:l + 1, :]).astype(BF16)
            hi = (xg + pos_ref[half + l:half + l + 1, :]).astype(BF16)
            acc_lo[g] += _dot(lo, w1_ref[l * HEAD_DIM:(l + 1) * HEAD_DIM, :])
            acc_hi[g] += _dot(hi, w1_ref[(half + l) * HEAD_DIM:(half + l + 1) * HEAD_DIM, :])
    row = lax.broadcasted_iota(jnp.int32, (nc, CMP_HIDDEN), 0)
    for g in range(GROUPS):
        nxt = jnp.where(row < nc - 1, pltpu.roll(acc_hi[g], nc - 1, 0), 0.0)
        hdn = _gelu_tanh(acc_lo[g] + nxt).astype(BF16)
        if is_key:
            out = _dot(hdn, w2_ref[...])
            ms = jnp.sum(out * out, axis=-1, keepdims=True) * (1.0 / HEAD_DIM)
            o_ref[g] = (out * lax.rsqrt(ms + RMS_EPS) * kg_ref[...]).astype(BF16)
        else:
            outT = lax.dot_general(w2_ref[...], hdn, (((1,), (1,)), ((), ())),
                                   preferred_element_type=F32)
            o_ref[g] = outT.astype(BF16)


def _compress(kv, pos, w1, w2, k_gain, is_key):
    _, bsz, seq, _ = kv.shape
    nc = seq // CMP_STRIDE
    kg = jnp.pad(k_gain, (0, LANES - HEAD_DIM)).reshape(1, LANES)
    if is_key:
        w2p = jnp.pad(w2, ((0, 0), (0, LANES - HEAD_DIM))).astype(BF16)
    else:
        w2p = w2.T.astype(BF16)
    if is_key:
        out_spec = pl.BlockSpec((None, GROUPS, nc, LANES), lambda b: (b, 0, 0, 0))
        out_shape = jax.ShapeDtypeStruct((bsz, GROUPS, nc, LANES), BF16)
    else:
        out_spec = pl.BlockSpec((None, GROUPS, HEAD_DIM, nc), lambda b: (b, 0, 0, 0))
        out_shape = jax.ShapeDtypeStruct((bsz, GROUPS, HEAD_DIM, nc), BF16)
    return pl.pallas_call(
        functools.partial(_compress_kernel, is_key=is_key),
        grid=(bsz,),
        in_specs=[
            pl.BlockSpec((2, None, seq, LANES), lambda b: (0, b, 0, 0)),
            pl.BlockSpec((CMP_BLOCK, HEAD_DIM), lambda b: (0, 0)),
            pl.BlockSpec((CMP_BLOCK * HEAD_DIM, CMP_HIDDEN), lambda b: (0, 0)),
            pl.BlockSpec(w2p.shape, lambda b: (0, 0)),
            pl.BlockSpec((1, LANES), lambda b: (0, 0)),
        ],
        out_specs=out_spec,
        out_shape=out_shape,
        scratch_shapes=[pltpu.VMEM((GROUPS, nc, CMP_HIDDEN), F32),
                        pltpu.VMEM((GROUPS, nc, CMP_HIDDEN), F32)],
        compiler_params=_cparams(("arbitrary",)),
        name="compress_k" if is_key else "compress_v",
    )(kv, pos, w1.astype(BF16), w2p, kg)


def _t5_bucket_np(dist):
    n = np.maximum(dist, 0)
    max_exact = REL_BUCKETS // 2
    nf = np.maximum(n, 1).astype(np.float32)
    large = max_exact + (np.log(nf / max_exact) / math.log(REL_MAX_DIST / max_exact)
                         * (REL_BUCKETS - max_exact)).astype(np.int32)
    large = np.minimum(large, REL_BUCKETS - 1)
    return np.where(n < max_exact, n, large)


def _bias_tables(rel_bias):
    rb = rel_bias.astype(F32).T
    rel = ((rb - rb[:, REL_BUCKETS - 1:]) * LOG2E).reshape(GROUPS, HEADS_PER_GROUP, REL_BUCKETS)
    ql = np.arange(Q_BLOCK)[None, :]
    kl = np.arange(Q_BLOCK)[:, None]
    d_lo, d_hi = -2 * Q_BLOCK, 3 * Q_BLOCK
    dvec = np.arange(d_lo, d_hi)
    onehot = jnp.asarray(np.eye(REL_BUCKETS, dtype=np.float32)[_t5_bucket_np(dvec)])
    vec = jnp.einsum("ghb,db->ghd", rel, onehot, precision=lax.Precision.HIGHEST)
    vec = jnp.where(jnp.asarray(dvec >= 0), vec, NEG)

    def table(dist):
        rows = [vec[:, :, int(d0) - d_lo:int(d0) - d_lo + Q_BLOCK] for d0 in dist[:, 0]]
        return jnp.stack(rows, axis=1).reshape(GROUPS, dist.shape[0], QL)

    near = jnp.stack([table(ql - kl + Q_BLOCK), table(ql - kl)], axis=1)
    win0 = np.where(ql - kl + WINDOW < WINDOW, 0.0, NEG).astype(np.float32)
    win0 = jnp.asarray(np.tile(win0, (1, HEADS_PER_GROUP)))
    r = np.arange(16)[:, None]
    cmp = table(ql - CMP_STRIDE * (r - 8) - (CMP_BLOCK - 1))
    cmp_first = jnp.concatenate([cmp[:, 8:], jnp.zeros_like(cmp[:, 8:])], axis=1)
    return near, win0, jnp.stack([cmp, cmp_first], axis=1)


def _attn_kernel(qT_ref, kc_ref, vcT_ref, ks_ref, vsT_ref, kw_ref, vwT_ref, gate_ref,
                 near_ref, win0_ref, cmpt_ref, ov_ref, o_ref,
                 rhs_z, rhs_f, selb_full, selb_far, sc_scr, sw_scr, acc_scr, m_scr,
                 rhs_n, s_nxt, p_pend, a_pend):
    qi = pl.program_id(2)
    t0 = pl.multiple_of(qi * Q_BLOCK, Q_BLOCK)
    nc = kc_ref.shape[0]
    ns = ov_ref.shape[0]

    q4 = qT_ref[...]
    qt = jnp.concatenate([q4[h * HEAD_DIM:(h + 1) * HEAD_DIM, :] for h in range(HEADS_PER_GROUP)], axis=1)
    rhs_z[0:HEAD_DIM, :] = qt
    rhs_z[HEAD_DIM:, :] = jnp.zeros((LANES - HEAD_DIM, QL), BF16)
    rhs_f[0:HEAD_DIM, :] = qt
    rhs_f[HEAD_DIM:, :] = jnp.zeros((LANES - HEAD_DIM, QL), BF16)
    rhs_n[0:HEAD_DIM, :] = qt
    rhs_n[HEAD_DIM:, :] = jnp.zeros((LANES - HEAD_DIM, QL), BF16)

    def tile4(v):
        return jnp.concatenate([v] * HEADS_PER_GROUP, axis=1)

    sc_scr[...] = _dot(kc_ref[...], rhs_z[...])
    n_win = WINDOW // Q_BLOCK + 1
    starts = []
    for a in range(n_win):
        start = t0 - Q_BLOCK * (n_win - 1 - a)
        st = pl.multiple_of(jnp.maximum(start, 0), Q_BLOCK)
        starts.append(st)
        s = _dot(kw_ref[pl.ds(st, Q_BLOCK), :], rhs_z[...])
        if a == 0:
            s = s + win0_ref[...]
        elif a == n_win - 2:
            s = s + near_ref[0]
        elif a == n_win - 1:
            s = s + near_ref[1]
        sw_scr[a * Q_BLOCK:(a + 1) * Q_BLOCK, :] = jnp.where(start >= 0, s, NEG)
    s_diag = _dot(ks_ref[pl.ds(t0, Q_BLOCK), :], rhs_z[...]) + near_ref[1]

    st = pl.multiple_of(jnp.maximum(8 * qi - 8, 0), 8)
    sc_scr[pl.ds(st, 16), :] += jnp.where(qi == 0, cmpt_ref[1], cmpt_ref[0])
    row = lax.broadcasted_iota(jnp.int32, (nc, QL), 0)
    s = sc_scr[...] + jnp.where(row < 8 * qi + 8, 0.0, NEG)
    m = jnp.max(s, axis=0, keepdims=True)
    p = jnp.exp2(s - m)
    l = jnp.sum(p, axis=0, keepdims=True)
    inv = jnp.where(m > 0.1 * NEG, 1.0 / l, 0.0)
    o_c = _dot(vcT_ref[...], p.astype(BF16)) * inv
    pn = p * inv
    psum = (pn[:, 0:Q_BLOCK] + pn[:, Q_BLOCK:2 * Q_BLOCK]
            + pn[:, 2 * Q_BLOCK:3 * Q_BLOCK] + pn[:, 3 * Q_BLOCK:4 * Q_BLOCK])
    phi, plo = _split_bf16(psum)
    imp = _dot(ov_ref[...], phi) + _dot(ov_ref[...], plo)

    s = sw_scr[...]
    m = jnp.max(s, axis=0, keepdims=True)
    pb = jnp.exp2(s - m).astype(BF16)
    o_w = _dot(vwT_ref[:, pl.ds(starts[0], Q_BLOCK)], pb[0:Q_BLOCK, :])
    for a in range(1, n_win):
        o_w += _dot(vwT_ref[:, pl.ds(starts[a], Q_BLOCK)], pb[a * Q_BLOCK:(a + 1) * Q_BLOCK, :])
    o_w = o_w[0:HEAD_DIM, :] / o_w[HEAD_DIM:HEAD_DIM + 1, :]

    jrow = lax.broadcasted_iota(jnp.int32, (ns, Q_BLOCK), 0)
    qlane = lax.broadcasted_iota(jnp.int32, (ns, Q_BLOCK), 1)
    cur = 2 * qi + jnp.where(qlane >= SEL_BLOCK, 1, 0)
    future = jrow > cur
    forced = (jrow == 0) | (jrow == cur) | (jrow == cur - 1)
    v = jnp.where(future, NEG, imp)
    v = jnp.where(forced, FORCE, v)
    rowf = jrow.astype(F32)
    picked = jnp.zeros((ns, Q_BLOCK), F32)
    for _ in range(SEL_TOP_N):
        mx = jnp.max(v, axis=0, keepdims=True)
        idx = jnp.min(jnp.where(v == mx, rowf, float(ns)), axis=0, keepdims=True)
        hit = rowf == idx
        picked = jnp.where(hit, 1.0, picked)
        v = jnp.where(hit, -jnp.inf, v)
    sb = jnp.where((picked > 0.0) & jnp.logical_not(future), 0.0, NEG)
    selb_full[...] = tile4(sb).astype(BF16)
    selb_far[...] = tile4(jnp.where(jrow < 2 * qi - 2, sb, NEG)).astype(BF16)

    s = s_diag
    m = jnp.max(s, axis=0, keepdims=True)
    acc = _dot(vsT_ref[:, pl.ds(t0, Q_BLOCK)], jnp.exp2(s - m).astype(BF16))

    tp = pl.multiple_of(jnp.maximum(t0 - Q_BLOCK, 0), Q_BLOCK)
    r0 = pl.multiple_of((jnp.maximum(2 * qi - 2, 0) // FAR_BLOCKS) * FAR_BLOCKS, FAR_BLOCKS)
    rhs_f[HEAD_DIM:HEAD_DIM + FAR_BLOCKS, :] = selb_full[pl.ds(r0, FAR_BLOCKS), :]
    s = _dot(ks_ref[pl.ds(tp, Q_BLOCK), :], rhs_f[...]) + near_ref[0]
    s = jnp.where(qi >= 1, s, NEG)
    m_new = jnp.maximum(m, jnp.max(s, axis=0, keepdims=True))
    acc = jnp.exp2(m - m_new) * acc + _dot(vsT_ref[:, pl.ds(tp, Q_BLOCK)], jnp.exp2(s - m_new).astype(BF16))
    m_scr[...] = m_new
    acc_scr[...] = acc

    n_sub = FAR_KEYS // FAR_SUB
    last_tile = ks_ref.shape[0] // FAR_KEYS - 1

    def far_scores(jt, rhs_ref, u):
        k0 = pl.multiple_of(jt * FAR_KEYS + u * FAR_SUB, FAR_SUB)
        return _dot(ks_ref[pl.ds(k0, FAR_SUB), :], rhs_ref[...])

    def far_rhs(jt, rhs_ref):
        r0 = pl.multiple_of(jt * FAR_BLOCKS, FAR_BLOCKS)
        rhs_ref[HEAD_DIM:HEAD_DIM + FAR_BLOCKS, :] = selb_far[pl.ds(r0, FAR_BLOCKS), :]

    def far_pv(jt, u, p_bf16):
        k0 = pl.multiple_of(jt * FAR_KEYS + u * FAR_SUB, FAR_SUB)
        return _dot(vsT_ref[:, pl.ds(k0, FAR_SUB)], p_bf16)

    far_rhs(0, rhs_f)
    s_nxt[...] = far_scores(0, rhs_f, 0)
    p_pend[...] = jnp.zeros_like(p_pend)
    a_pend[...] = jnp.ones_like(a_pend)

    def far_body(jt, carry):
        jn = jnp.minimum(jt + 1, last_tile)
        jp = jnp.maximum(jt - 1, 0)
        far_rhs(jn, rhs_n)
        m, acc = m_scr[...], acc_scr[...]
        s_cur = s_nxt[...]
        pend = (jp, n_sub - 1, p_pend[...], a_pend[...])
        for u in range(n_sub):
            if u + 1 < n_sub:
                s_new = far_scores(jt, rhs_f, u + 1)
            else:
                s_nxt[...] = far_scores(jn, rhs_n, 0)
            acc = pend[3] * acc + far_pv(pend[0], pend[1], pend[2])
            m_new = jnp.maximum(m, jnp.max(s_cur, axis=0, keepdims=True))
            alpha = jnp.exp2(m - m_new)
            p = jnp.exp2(s_cur - m_new)
            m = m_new
            pend = (jt, u, p.astype(BF16), alpha)
            if u + 1 < n_sub:
                s_cur = s_new
        p_pend[...] = pend[2]
        a_pend[...] = pend[3]
        far_rhs(jn, rhs_f)
        m_scr[...] = m
        acc_scr[...] = acc
        return carry

    n_far = jnp.maximum(2 * qi - 2 + FAR_BLOCKS - 1, 0) // FAR_BLOCKS
    lax.fori_loop(0, n_far, far_body, 0)
    jl = jnp.maximum(n_far - 1, 0)
    acc = a_pend[...] * acc_scr[...] + far_pv(jl, n_sub - 1, p_pend[...])
    o_s = acc[0:HEAD_DIM, :] / acc[HEAD_DIM:HEAD_DIM + 1, :]

    for h in range(HEADS_PER_GROUP):
        sl = slice(h * Q_BLOCK, (h + 1) * Q_BLOCK)
        o_h = (gate_ref[3 * h:3 * h + 1, :] * o_c[:, sl]
               + gate_ref[3 * h + 1:3 * h + 2, :] * o_s[:, sl]
               + gate_ref[3 * h + 2:3 * h + 3, :] * o_w[:, sl])
        o_ref[h * HEAD_DIM:(h + 1) * HEAD_DIM, :] = o_h.astype(BF16)


def _nsa_attention(qT, kcmp, vcmpT, ks, vsT, kw, vwT, gT, tables):
    bsz, _, seq = qT.shape
    assert seq % FAR_KEYS == 0
    nq = seq // Q_BLOCK
    nc = seq // CMP_STRIDE
    ns = seq // SEL_BLOCK
    near, win0, cmpt = tables
    cs = np.arange(nc) * CMP_STRIDE
    ss = np.arange(ns) * SEL_BLOCK
    ov = (cs[None, :] <= ss[:, None] + SEL_BLOCK - 1) & (cs[None, :] + CMP_BLOCK - 1 >= ss[:, None])
    ov = jnp.asarray(ov, BF16)
    gates = gT.reshape(bsz, 2 * GROUPS, 16, seq)
    per_bg = lambda b, g, i: (b, g, 0, 0)
    return pl.pallas_call(
        _attn_kernel,
        grid=(bsz, GROUPS, nq),
        in_specs=[
            pl.BlockSpec((None, KV_DIM, Q_BLOCK), lambda b, g, i: (b, g, i)),
            pl.BlockSpec((None, None, nc, LANES), per_bg),
            pl.BlockSpec((None, None, HEAD_DIM, nc), per_bg),
            pl.BlockSpec((None, None, seq, LANES), per_bg),
            pl.BlockSpec((None, None, V_ROWS, seq), per_bg),
            pl.BlockSpec((None, None, seq, LANES), per_bg),
            pl.BlockSpec((None, None, V_ROWS, seq), per_bg),
            pl.BlockSpec((None, None, 16, Q_BLOCK), lambda b, g, i: (b, g, 0, i)),
            pl.BlockSpec((None, 2, Q_BLOCK, QL), lambda b, g, i: (g, 0, 0, 0)),
            pl.BlockSpec((Q_BLOCK, QL), lambda b, g, i: (0, 0)),
            pl.BlockSpec((None, 2, 16, QL), lambda b, g, i: (g, 0, 0, 0)),
            pl.BlockSpec((ns, nc), lambda b, g, i: (0, 0)),
        ],
        out_specs=pl.BlockSpec((None, KV_DIM, Q_BLOCK), lambda b, g, i: (b, g, i)),
        out_shape=jax.ShapeDtypeStruct((bsz, HEADS * HEAD_DIM, seq), BF16),
        scratch_shapes=[
            pltpu.VMEM((LANES, QL), BF16),
            pltpu.VMEM((LANES, QL), BF16),
            pltpu.VMEM((ns, QL), BF16),
            pltpu.VMEM((ns, QL), BF16),
            pltpu.VMEM((nc, QL), F32),
            pltpu.VMEM((WINDOW + Q_BLOCK, QL), F32),
            pltpu.VMEM((V_ROWS, QL), F32),
            pltpu.VMEM((1, QL), F32),
            pltpu.VMEM((LANES, QL), BF16),
            pltpu.VMEM((FAR_SUB, QL), F32),
            pltpu.VMEM((FAR_SUB, QL), BF16),
            pltpu.VMEM((1, QL), F32),
        ],
        compiler_params=_cparams(("arbitrary", "arbitrary", "arbitrary")),
        name="nsa_attention",
    )(qT, kcmp, vcmpT, ks, vsT, kw, vwT, gates, near, win0, cmpt, ov)


def _oproj_kernel(x_ref, mod_ref, oT_ref, w_ref, o_ref):
    y = _dot(oT_ref[...].T, w_ref[...])
    o_ref[...] = x_ref[...] + mod_ref[5:6, :] * y


def _nsa_out(x, mod, oT, w_out):
    bsz, seq, d = x.shape
    tm = TOKEN_TILE
    return pl.pallas_call(
        _oproj_kernel,
        grid=(bsz, seq // tm),
        in_specs=[
            pl.BlockSpec((None, tm, d), lambda b, i: (b, i, 0)),
            pl.BlockSpec((None, 9, d), lambda b, i: (b, 0, 0)),
            pl.BlockSpec((None, d, tm), lambda b, i: (b, 0, i)),
            _resident((d, d), lambda b, i: (0, 0)),
        ],
        out_specs=pl.BlockSpec((None, tm, d), lambda b, i: (b, i, 0)),
        out_shape=jax.ShapeDtypeStruct(x.shape, F32),
        compiler_params=_cparams(("arbitrary", "arbitrary")),
        name="nsa_out",
    )(x, mod, oT, w_out.astype(BF16))


def _nsa_mixer(x, mod, g, w_in, cmp_pos, cmp_w1, cmp_w2, q_gain, k_gain, w_out, tables):
    qT, kvc, ks, vsT, kw, vwT, gT = _nsa_proj(x, mod, g, w_in, q_gain, k_gain)
    kcmp = _compress(kvc[0], cmp_pos, cmp_w1[0], cmp_w2[0], k_gain[0], True)
    vcmpT = _compress(kvc[1], cmp_pos, cmp_w1[1], cmp_w2[1], k_gain[0], False)
    oT = _nsa_attention(qT, kcmp, vcmpT, ks, vsT, kw, vwT, gT, tables)
    return _nsa_out(x, mod, oT, w_out)


def kernel(x, c, norm_g, ada_w, ada_b, ffn_w_in, ffn_w_out, conv_w_in, conv_w, conv_w_out,
           nsa_w_in, nsa_cmp_pos, nsa_cmp_w1, nsa_cmp_w2, nsa_q_gain, nsa_k_gain, nsa_w_out,
           rel_bias):
    depth = ada_w.shape[0]
    mods = _ada_mod(c, ada_w, ada_b)
    tables = _bias_tables(rel_bias)
    for i in range(depth):
        mod = mods[i]
        x = _ffn(x, mod, norm_g[i, 0], ffn_w_in[i, 0], ffn_w_out[i, 0], 0)
        j = i // 2
        if i % 2 == 0:
            x = _conv_mixer(x, mod, norm_g[i, 1], conv_w_in[j], conv_w[j], conv_w_out[j])
        else:
            x = _nsa_mixer(x, mod, norm_g[i, 1], nsa_w_in[j], nsa_cmp_pos[j], nsa_cmp_w1[j],
                           nsa_cmp_w2[j], nsa_q_gain[j], nsa_k_gain[j], nsa_w_out[j], tables)
        x = _ffn(x, mod, norm_g[i, 2], ffn_w_in[i, 1], ffn_w_out[i, 1], 2)
    return x
```

```python
import functools
import math

import numpy as np
import jax
import jax.numpy as jnp
from jax import lax
from jax.experimental import pallas as pl
from jax.experimental.pallas import tpu as pltpu

F32 = jnp.float32
BF16 = jnp.bfloat16

D_MODEL = 1024
D_FF = 2816
HEADS = 16
HEAD_DIM = 64
GROUPS = 4
HEADS_PER_GROUP = 4
KV_DIM = GROUPS * HEAD_DIM
CMP_BLOCK = 32
CMP_STRIDE = 16
CMP_HIDDEN = 256
SEL_BLOCK = 64
SEL_TOP_N = 16
WINDOW = 512
Q_BLOCK = 128
REL_BUCKETS = 32
REL_MAX_DIST = 128
RMS_EPS = 1e-6
NEG = -1e30
FORCE = 1e9

LANES = 128
QL = HEADS_PER_GROUP * Q_BLOCK
FAR_KEYS = 1024
FAR_BLOCKS = FAR_KEYS // SEL_BLOCK
FAR_SUB = 256
V_ROWS = HEAD_DIM + 16
LOG2E = math.log2(math.e)
FF_CHUNK = 256
TOKEN_TILE = 512
FFN_TOKEN_TILE = 1024
VMEM_LIMIT = 56 * 1024 * 1024


def _dot(a, b):
    return jnp.dot(a, b, preferred_element_type=F32)


def _cparams(sem):
    return pltpu.CompilerParams(dimension_semantics=sem, vmem_limit_bytes=VMEM_LIMIT)


def _resident(shape, index_map):
    return pl.BlockSpec(shape, index_map, pipeline_mode=pl.Buffered(1))


def _adaln(x, g, mod_ref, sub):
    ms = jnp.mean(x * x, axis=-1, keepdims=True)
    shift = mod_ref[3 * sub:3 * sub + 1, :]
    scale = mod_ref[3 * sub + 1:3 * sub + 2, :]
    return (x * lax.rsqrt(ms + RMS_EPS) * g) * (1.0 + scale) + shift


def _split_bf16(v):
    hi = v.astype(BF16)
    lo = (v - hi.astype(F32)).astype(BF16)
    return hi, lo


def _ada_kernel(c_ref, w_ref, b_ref, o_ref):
    c = c_ref[...]
    cond = c * jax.nn.sigmoid(c)
    chi, clo = _split_bf16(cond)
    whi, wlo = _split_bf16(w_ref[...])
    o_ref[...] = _dot(chi, whi) + _dot(chi, wlo) + _dot(clo, whi) + b_ref[...]


def _ada_mod(c, ada_w, ada_b):
    depth, d, n = ada_w.shape
    bsz = c.shape[0]
    rows = 8
    tn = 1024
    c_pad = jnp.zeros((rows, d), F32).at[:bsz].set(c)
    out = pl.pallas_call(
        _ada_kernel,
        grid=(depth, n // tn),
        in_specs=[
            pl.BlockSpec((rows, d), lambda l, j: (0, 0)),
            pl.BlockSpec((None, d, tn), lambda l, j: (l, 0, j)),
            pl.BlockSpec((None, 1, tn), lambda l, j: (l, 0, j)),
        ],
        out_specs=pl.BlockSpec((None, rows, tn), lambda l, j: (l, 0, j)),
        out_shape=jax.ShapeDtypeStruct((depth, rows, n), F32),
        compiler_params=_cparams(("arbitrary", "arbitrary")),
        name="ada_mod",
    )(c_pad, ada_w, ada_b.reshape(depth, 1, n))
    return out[:, :bsz].reshape(depth, bsz, 9, d)


def _ffn_kernel(x_ref, mod_ref, g_ref, wg_ref, wu_ref, wo_ref, o_ref, h_scr, acc_scr, *, sub):
    n_half, hm, _ = h_scr.shape
    nchunk = wg_ref.shape[0]
    gate = 0.5 * mod_ref[3 * sub + 2:3 * sub + 3, :]
    for hh in range(n_half):
        rows = slice(hh * hm, (hh + 1) * hm)
        h_scr[hh] = _adaln(x_ref[rows, :], g_ref[...], mod_ref, sub).astype(BF16)

        def up(c):
            h = h_scr[hh]
            return _dot(h, wg_ref[c]), _dot(h, wu_ref[c])

        nxt = up(0)
        for c in range(nchunk):
            gg, uu = nxt
            if c + 1 < nchunk:
                nxt = up(c + 1)
            a = (gg * jax.nn.sigmoid(gg) * uu).astype(BF16)
            down = _dot(a, wo_ref[c])
            if c == 0:
                acc_scr[hh] = down
            else:
                acc_scr[hh] += down
        o_ref[rows, :] = x_ref[rows, :] + gate * acc_scr[hh]


def _ffn(x, mod, g, w_in, w_out, sub):
    bsz, seq, d = x.shape
    tm = FFN_TOKEN_TILE
    hm = tm // 2
    nchunk = D_FF // FF_CHUNK
    wg = w_in[:, :D_FF].astype(BF16).reshape(d, nchunk, FF_CHUNK).transpose(1, 0, 2)
    wu = w_in[:, D_FF:].astype(BF16).reshape(d, nchunk, FF_CHUNK).transpose(1, 0, 2)
    wo = w_out.astype(BF16).reshape(nchunk, FF_CHUNK, d)
    return pl.pallas_call(
        functools.partial(_ffn_kernel, sub=sub),
        grid=(bsz, seq // tm),
        in_specs=[
            pl.BlockSpec((None, tm, d), lambda b, i: (b, i, 0)),
            pl.BlockSpec((None, 9, d), lambda b, i: (b, 0, 0)),
            pl.BlockSpec((1, d), lambda b, i: (0, 0)),
            _resident((nchunk, d, FF_CHUNK), lambda b, i: (0, 0, 0)),
            _resident((nchunk, d, FF_CHUNK), lambda b, i: (0, 0, 0)),
            _resident((nchunk, FF_CHUNK, d), lambda b, i: (0, 0, 0)),
        ],
        out_specs=pl.BlockSpec((None, tm, d), lambda b, i: (b, i, 0)),
        out_shape=jax.ShapeDtypeStruct(x.shape, F32),
        scratch_shapes=[pltpu.VMEM((2, hm, d), BF16), pltpu.VMEM((2, hm, d), F32)],
        compiler_params=_cparams(("arbitrary", "arbitrary")),
        name="ffn",
    )(x, mod, g.reshape(1, d), wg, wu, wo)


def _conv_kernel(x_ref, mod_ref, g_ref, win_ref, cw_ref, wout_ref, o_ref, ubuf):
    tm = x_ref.shape[0]

    @pl.when(pl.program_id(1) == 0)
    def _():
        ubuf[0:8, :] = jnp.zeros((8, ubuf.shape[1]), F32)

    x = x_ref[...]
    hb = _adaln(x, g_ref[...], mod_ref, 1).astype(BF16)
    bg = _dot(hb, win_ref[0])
    u = _dot(hb, win_ref[1]) * _dot(hb, win_ref[2])
    ubuf[8:8 + tm, :] = u
    y = cw_ref[2:3, :] * u + cw_ref[1:2, :] * ubuf[7:7 + tm, :] + cw_ref[0:1, :] * ubuf[6:6 + tm, :]
    ubuf[0:8, :] = ubuf[tm:tm + 8, :]
    out = _dot((bg * y).astype(BF16), wout_ref[...])
    o_ref[...] = x + mod_ref[5:6, :] * out


def _conv_mixer(x, mod, g, w_in, conv_w, w_out):
    bsz, seq, d = x.shape
    tm = TOKEN_TILE
    win = w_in.astype(BF16).reshape(d, 3, d).transpose(1, 0, 2)
    return pl.pallas_call(
        _conv_kernel,
        grid=(bsz, seq // tm),
        in_specs=[
            pl.BlockSpec((None, tm, d), lambda b, i: (b, i, 0)),
            pl.BlockSpec((None, 9, d), lambda b, i: (b, 0, 0)),
            pl.BlockSpec((1, d), lambda b, i: (0, 0)),
            _resident((3, d, d), lambda b, i: (0, 0, 0)),
            pl.BlockSpec((3, d), lambda b, i: (0, 0)),
            _resident((d, d), lambda b, i: (0, 0)),
        ],
        out_specs=pl.BlockSpec((None, tm, d), lambda b, i: (b, i, 0)),
        out_shape=jax.ShapeDtypeStruct(x.shape, F32),
        scratch_shapes=[pltpu.VMEM((tm + 8, d), F32)],
        compiler_params=_cparams(("arbitrary", "arbitrary")),
        name="conv_mixer",
    )(x, mod, g.reshape(1, d), win, conv_w, w_out.astype(BF16))


def _head_ms(sq, p_ref, p_first):
    hi, lo = _split_bf16(sq)
    if p_first:
        return _dot(p_ref[...], hi) + _dot(p_ref[...], lo)
    return _dot(hi, p_ref[...]) + _dot(lo, p_ref[...])


def _proj_kernel(x_ref, mod_ref, g_ref, w_ref, wT_ref, p_ref, qg_ref, kg_ref,
                 qT_ref, kvc_ref, ks_ref, vsT_ref, kw_ref, vwT_ref, gT_ref):
    tm = x_ref.shape[0]
    x = x_ref[...]
    hb = _adaln(x, g_ref[...], mod_ref, 1).astype(BF16)

    def col(i, width=KV_DIM):
        return _dot(hb, w_ref[:, i:i + width])

    def colT(i, width=KV_DIM):
        return lax.dot_general(wT_ref[i:i + width, :], hb, (((1,), (1,)), ((), ())),
                               preferred_element_type=F32)

    qg = jnp.concatenate([qg_ref[...]] * (tm // LANES), axis=1)
    for c in range(GROUPS):
        qc = colT(c * KV_DIM)
        qn = qc * lax.rsqrt(_head_ms(qc * qc, p_ref, True) + RMS_EPS) * qg
        qT_ref[c * KV_DIM:(c + 1) * KV_DIM, :] = qn.astype(BF16)
    vsT = colT(HEADS * HEAD_DIM).astype(BF16)
    vwT = colT(HEADS * HEAD_DIM + KV_DIM).astype(BF16)
    rid = lax.broadcasted_iota(jnp.int32, (V_ROWS - HEAD_DIM, tm), 0)
    ones_rows = jnp.where(rid == 0, 1.0, 0.0).astype(BF16)
    for g in range(GROUPS):
        vsT_ref[g] = jnp.concatenate([vsT[g * HEAD_DIM:(g + 1) * HEAD_DIM, :], ones_rows], axis=0)
        vwT_ref[g] = jnp.concatenate([vwT[g * HEAD_DIM:(g + 1) * HEAD_DIM, :], ones_rows], axis=0)
    gT_ref[...] = jax.nn.sigmoid(colT(HEADS * HEAD_DIM + 2 * KV_DIM, LANES))

    for kv in range(2):
        for half in range(2):
            kvc_ref[kv, half] = col(kv * KV_DIM + half * LANES, LANES)

    pos = pl.program_id(1) * tm + lax.broadcasted_iota(jnp.int32, (tm, HEAD_DIM), 0)
    lane = lax.broadcasted_iota(jnp.int32, (tm, HEAD_DIM), 1)
    onehot = jnp.where(lane == ((pos // SEL_BLOCK) % FAR_BLOCKS), 1.0, 0.0)
    zeros = jnp.zeros((tm, HEAD_DIM), F32)

    ks = col(2 * KV_DIM)
    ksn = ks * lax.rsqrt(_head_ms(ks * ks, p_ref, False) + RMS_EPS) * kg_ref[0:1, :]
    kw = col(3 * KV_DIM)
    kwn = kw * lax.rsqrt(_head_ms(kw * kw, p_ref, False) + RMS_EPS) * kg_ref[1:2, :]
    for g in range(GROUPS):
        sl = slice(g * HEAD_DIM, (g + 1) * HEAD_DIM)
        ks_ref[g] = jnp.concatenate([ksn[:, sl], onehot], axis=1).astype(BF16)
        kw_ref[g] = jnp.concatenate([kwn[:, sl], zeros], axis=1).astype(BF16)


def _nsa_proj(x, mod, g, w_in, q_gain, k_gain):
    bsz, seq, d = x.shape
    tm = TOKEN_TILE
    nq = HEADS * HEAD_DIM
    q, kc, vc, ks, vs, kw, vw, gl = jnp.split(w_in, [nq + i * KV_DIM for i in range(7)], axis=1)
    gl = gl.reshape(d, GROUPS, 3 * HEADS_PER_GROUP)
    gl = jnp.pad(gl, ((0, 0), (0, GROUPS), (0, 4))).reshape(d, LANES)
    w = jnp.concatenate([kc, vc, ks, kw], axis=1).astype(BF16)
    wT = jnp.concatenate([q, vs, vw, gl], axis=1).T.astype(BF16)
    hid = np.arange(KV_DIM) // HEAD_DIM
    pmat = jnp.asarray((hid[:, None] == hid[None, :]) / HEAD_DIM, BF16)
    qg = jnp.tile(q_gain, HEADS_PER_GROUP) * (HEAD_DIM ** -0.5 * LOG2E)
    qg = jnp.broadcast_to(qg[:, None], (KV_DIM, LANES))
    kg = jnp.stack([jnp.tile(k_gain[1], GROUPS), jnp.tile(k_gain[2], GROUPS)])
    kv_spec = pl.BlockSpec((None, GROUPS, tm, LANES), lambda b, i: (b, 0, i, 0))
    vT_spec = pl.BlockSpec((None, GROUPS, V_ROWS, tm), lambda b, i: (b, 0, 0, i))
    kv_shape = jax.ShapeDtypeStruct((bsz, GROUPS, seq, LANES), BF16)
    vT_shape = jax.ShapeDtypeStruct((bsz, GROUPS, V_ROWS, seq), BF16)
    return pl.pallas_call(
        _proj_kernel,
        grid=(bsz, seq // tm),
        in_specs=[
            pl.BlockSpec((None, tm, d), lambda b, i: (b, i, 0)),
            pl.BlockSpec((None, 9, d), lambda b, i: (b, 0, 0)),
            pl.BlockSpec((1, d), lambda b, i: (0, 0)),
            _resident(w.shape, lambda b, i: (0, 0)),
            _resident(wT.shape, lambda b, i: (0, 0)),
            pl.BlockSpec((KV_DIM, KV_DIM), lambda b, i: (0, 0)),
            pl.BlockSpec((KV_DIM, LANES), lambda b, i: (0, 0)),
            pl.BlockSpec((2, KV_DIM), lambda b, i: (0, 0)),
        ],
        out_specs=[
            pl.BlockSpec((None, nq, tm), lambda b, i: (b, 0, i)),
            pl.BlockSpec((2, 2, None, tm, LANES), lambda b, i: (0, 0, b, i, 0)),
            kv_spec, vT_spec, kv_spec, vT_spec,
            pl.BlockSpec((None, LANES, tm), lambda b, i: (b, 0, i)),
        ],
        out_shape=[
            jax.ShapeDtypeStruct((bsz, nq, seq), BF16),
            jax.ShapeDtypeStruct((2, 2, bsz, seq, LANES), F32),
            kv_shape, vT_shape, kv_shape, vT_shape,
            jax.ShapeDtypeStruct((bsz, LANES, seq), F32),
        ],
        compiler_params=_cparams(("arbitrary", "arbitrary")),
        name="nsa_proj",
    )(x, mod, g.reshape(1, d), w, wT, pmat, qg, kg)


def _gelu_tanh(x):
    return 0.5 * x * (1.0 + jnp.tanh(math.sqrt(2.0 / math.pi) * (x + 0.044715 * (x * x * x))))


def _compress_kernel(kv_ref, pos_ref, w1_ref, w2_ref, kg_ref, o_ref, acc_lo, acc_hi, *, is_key):
    nc = acc_lo.shape[1]
    half = CMP_BLOCK // 2
    acc_lo[...] = jnp.zeros_like(acc_lo)
    acc_hi[...] = jnp.zeros_like(acc_hi)
    for l in range(half):
        xl = [kv_ref[j, pl.ds(l, nc, stride=CMP_STRIDE), :] for j in range(2)]
        for g in range(GROUPS):
            xg = xl[g // 2][:, (g % 2) * HEAD_DIM:(g % 2 + 1) * HEAD_DIM]
            lo = (xg + pos_ref[l:l + 1, :]).astype(BF16)
            hi = (xg + pos_ref[half + l:half + l + 1, :]).astype(BF16)
            acc_lo[g] += _dot(lo, w1_ref[l * HEAD_DIM:(l + 1) * HEAD_DIM, :])
            acc_hi[g] += _dot(hi, w1_ref[(half + l) * HEAD_DIM:(half + l + 1) * HEAD_DIM, :])
    row = lax.broadcasted_iota(jnp.int32, (nc, CMP_HIDDEN), 0)
    for g in range(GROUPS):
        nxt = jnp.where(row < nc - 1, pltpu.roll(acc_hi[g], nc - 1, 0), 0.0)
        hdn = _gelu_tanh(acc_lo[g] + nxt).astype(BF16)
        if is_key:
            out = _dot(hdn, w2_ref[...])
            ms = jnp.sum(out * out, axis=-1, keepdims=True) * (1.0 / HEAD_DIM)
            o_ref[g] = (out * lax.rsqrt(ms + RMS_EPS) * kg_ref[...]).astype(BF16)
        else:
            outT = lax.dot_general(w2_ref[...], hdn, (((1,), (1,)), ((), ())),
                                   preferred_element_type=F32)
            o_ref[g] = outT.astype(BF16)


def _compress(kv, pos, w1, w2, k_gain, is_key):
    _, bsz, seq, _ = kv.shape
    nc = seq // CMP_STRIDE
    kg = jnp.pad(k_gain, (0, LANES - HEAD_DIM)).reshape(1, LANES)
    if is_key:
        w2p = jnp.pad(w2, ((0, 0), (0, LANES - HEAD_DIM))).astype(BF16)
    else:
        w2p = w2.T.astype(BF16)
    if is_key:
        out_spec = pl.BlockSpec((None, GROUPS, nc, LANES), lambda b: (b, 0, 0, 0))
        out_shape = jax.ShapeDtypeStruct((bsz, GROUPS, nc, LANES), BF16)
    else:
        out_spec = pl.BlockSpec((None, GROUPS, HEAD_DIM, nc), lambda b: (b, 0, 0, 0))
        out_shape = jax.ShapeDtypeStruct((bsz, GROUPS, HEAD_DIM, nc), BF16)
    return pl.pallas_call(
        functools.partial(_compress_kernel, is_key=is_key),
        grid=(bsz,),
        in_specs=[
            pl.BlockSpec((2, None, seq, LANES), lambda b: (0, b, 0, 0)),
            pl.BlockSpec((CMP_BLOCK, HEAD_DIM), lambda b: (0, 0)),
            pl.BlockSpec((CMP_BLOCK * HEAD_DIM, CMP_HIDDEN), lambda b: (0, 0)),
            pl.BlockSpec(w2p.shape, lambda b: (0, 0)),
            pl.BlockSpec((1, LANES), lambda b: (0, 0)),
        ],
        out_specs=out_spec,
        out_shape=out_shape,
        scratch_shapes=[pltpu.VMEM((GROUPS, nc, CMP_HIDDEN), F32),
                        pltpu.VMEM((GROUPS, nc, CMP_HIDDEN), F32)],
        compiler_params=_cparams(("arbitrary",)),
        name="compress_k" if is_key else "compress_v",
    )(kv, pos, w1.astype(BF16), w2p, kg)


def _t5_bucket_np(dist):
    n = np.maximum(dist, 0)
    max_exact = REL_BUCKETS // 2
    nf = np.maximum(n, 1).astype(np.float32)
    large = max_exact + (np.log(nf / max_exact) / math.log(REL_MAX_DIST / max_exact)
                         * (REL_BUCKETS - max_exact)).astype(np.int32)
    large = np.minimum(large, REL_BUCKETS - 1)
    return np.where(n < max_exact, n, large)


def _bias_tables(rel_bias):
    rb = rel_bias.astype(F32).T
    rel = ((rb - rb[:, REL_BUCKETS - 1:]) * LOG2E).reshape(GROUPS, HEADS_PER_GROUP, REL_BUCKETS)
    ql = np.arange(Q_BLOCK)[None, :]
    kl = np.arange(Q_BLOCK)[:, None]
    d_lo, d_hi = -2 * Q_BLOCK, 3 * Q_BLOCK
    dvec = np.arange(d_lo, d_hi)
    onehot = jnp.asarray(np.eye(REL_BUCKETS, dtype=np.float32)[_t5_bucket_np(dvec)])
    vec = jnp.einsum("ghb,db->ghd", rel, onehot, precision=lax.Precision.HIGHEST)
    vec = jnp.where(jnp.asarray(dvec >= 0), vec, NEG)

    def table(dist):
        rows = [vec[:, :, int(d0) - d_lo:int(d0) - d_lo + Q_BLOCK] for d0 in dist[:, 0]]
        return jnp.stack(rows, axis=1).reshape(GROUPS, dist.shape[0], QL)

    near = jnp.stack([table(ql - kl + Q_BLOCK), table(ql - kl)], axis=1)
    win0 = np.where(ql - kl + WINDOW < WINDOW, 0.0, NEG).astype(np.float32)
    win0 = jnp.asarray(np.tile(win0, (1, HEADS_PER_GROUP)))
    r = np.arange(16)[:, None]
    cmp = table(ql - CMP_STRIDE * (r - 8) - (CMP_BLOCK - 1))
    cmp_first = jnp.concatenate([cmp[:, 8:], jnp.zeros_like(cmp[:, 8:])], axis=1)
    return near, win0, jnp.stack([cmp, cmp_first], axis=1)


def _attn_kernel(qT_ref, kc_ref, vcT_ref, ks_ref, vsT_ref, kw_ref, vwT_ref, gate_ref,
                 near_ref, win0_ref, cmpt_ref, ov_ref, o_ref,
                 rhs_z, rhs_f, sb_scr, selb_far, sc_scr, sw_scr, acc_scr, m_scr,
                 rhs_n, s_nxt, p_pend, a_pend, sn_scr):
    qi = pl.program_id(2)
    t0 = pl.multiple_of(qi * Q_BLOCK, Q_BLOCK)
    nc = kc_ref.shape[0]
    ns = ov_ref.shape[0]

    q4 = qT_ref[...]
    qt = jnp.concatenate([q4[h * HEAD_DIM:(h + 1) * HEAD_DIM, :] for h in range(HEADS_PER_GROUP)], axis=1)
    rhs_z[0:HEAD_DIM, :] = qt
    rhs_z[HEAD_DIM:, :] = jnp.zeros((LANES - HEAD_DIM, QL), BF16)
    rhs_f[0:HEAD_DIM, :] = qt
    rhs_f[HEAD_DIM:, :] = jnp.zeros((LANES - HEAD_DIM, QL), BF16)
    rhs_n[0:HEAD_DIM, :] = qt
    rhs_n[HEAD_DIM:, :] = jnp.zeros((LANES - HEAD_DIM, QL), BF16)

    def tile4(v):
        return jnp.concatenate([v] * HEADS_PER_GROUP, axis=1)

    sc_scr[...] = _dot(kc_ref[...], rhs_z[...])
    n_win = WINDOW // Q_BLOCK + 1
    starts = []
    for a in range(n_win):
        start = t0 - Q_BLOCK * (n_win - 1 - a)
        st = pl.multiple_of(jnp.maximum(start, 0), Q_BLOCK)
        starts.append(st)
        s = _dot(kw_ref[pl.ds(st, Q_BLOCK), :], rhs_z[...])
        if a == 0:
            s = s + win0_ref[...]
        elif a == n_win - 2:
            s = s + near_ref[0]
        elif a == n_win - 1:
            s = s + near_ref[1]
        sw_scr[a * Q_BLOCK:(a + 1) * Q_BLOCK, :] = jnp.where(start >= 0, s, NEG)
    tp = pl.multiple_of(jnp.maximum(t0 - Q_BLOCK, 0), Q_BLOCK)
    sn_scr[0:Q_BLOCK, :] = _dot(ks_ref[pl.ds(tp, Q_BLOCK), :], rhs_z[...]) + near_ref[0]
    sn_scr[Q_BLOCK:, :] = _dot(ks_ref[pl.ds(t0, Q_BLOCK), :], rhs_z[...]) + near_ref[1]
    s_nxt[...] = _dot(ks_ref[0:FAR_SUB, :], rhs_z[...])

    st = pl.multiple_of(jnp.maximum(8 * qi - 8, 0), 8)
    sc_scr[pl.ds(st, 16), :] += jnp.where(qi == 0, cmpt_ref[1], cmpt_ref[0])
    row = lax.broadcasted_iota(jnp.int32, (nc, QL), 0)
    s = sc_scr[...] + jnp.where(row < 8 * qi + 8, 0.0, NEG)
    m = jnp.max(s, axis=0, keepdims=True)
    p = jnp.exp2(s - m)
    l = jnp.sum(p, axis=0, keepdims=True)
    inv = jnp.where(m > 0.1 * NEG, 1.0 / l, 0.0)
    o_c = _dot(vcT_ref[...], p.astype(BF16)) * inv
    pn = p * inv
    psum = (pn[:, 0:Q_BLOCK] + pn[:, Q_BLOCK:2 * Q_BLOCK]
            + pn[:, 2 * Q_BLOCK:3 * Q_BLOCK] + pn[:, 3 * Q_BLOCK:4 * Q_BLOCK])
    phi, plo = _split_bf16(psum)
    imp = _dot(ov_ref[...], phi) + _dot(ov_ref[...], plo)

    jrow = lax.broadcasted_iota(jnp.int32, (ns, Q_BLOCK), 0)
    qlane = lax.broadcasted_iota(jnp.int32, (ns, Q_BLOCK), 1)
    cur = 2 * qi + jnp.where(qlane >= SEL_BLOCK, 1, 0)
    future = jrow > cur
    forced = (jrow == 0) | (jrow == cur) | (jrow == cur - 1)
    v = jnp.where(future, NEG, imp)
    v = jnp.where(forced, -jnp.inf, v)
    rowf = jrow.astype(F32)
    picked = jnp.where(forced, 1.0, 0.0)
    win = {"m": None, "o": None}

    def win_max(a):
        ma = jnp.max(sw_scr[a * Q_BLOCK:(a + 1) * Q_BLOCK, :], axis=0, keepdims=True)
        win["m"] = ma if win["m"] is None else jnp.maximum(win["m"], ma)

    def win_pv(a):
        pb = jnp.exp2(sw_scr[a * Q_BLOCK:(a + 1) * Q_BLOCK, :] - win["m"]).astype(BF16)
        pv = _dot(vwT_ref[:, pl.ds(starts[a], Q_BLOCK)], pb)
        win["o"] = pv if win["o"] is None else win["o"] + pv

    filler = ([functools.partial(win_max, a) for a in range(n_win)]
              + [functools.partial(win_pv, a) for a in range(n_win)])
    for r in range(SEL_TOP_N - 3):
        mx = jnp.max(v, axis=0, keepdims=True)
        idx = jnp.min(jnp.where(v == mx, rowf, float(ns)), axis=0, keepdims=True)
        hit = rowf == idx
        picked = jnp.where(hit, 1.0, picked)
        v = jnp.where(hit, -jnp.inf, v)
        if r < len(filler):
            filler[r]()
    o_w = win["o"][0:HEAD_DIM, :] / win["o"][HEAD_DIM:HEAD_DIM + 1, :]
    sb = jnp.where((picked > 0.0) & jnp.logical_not(future), 0.0, NEG)
    sb_scr[...] = sb
    sb_far = jnp.where(jrow < 2 * qi - 2, sb, NEG)
    selb_far[...] = tile4(sb_far).astype(BF16)

    s_nxt[...] += jnp.concatenate(
        [jnp.broadcast_to(tile4(sb_far[b:b + 1, :]), (SEL_BLOCK, QL)) for b in range(FAR_SUB // SEL_BLOCK)], axis=0)
    m_scr[...] = jnp.full(m_scr.shape, NEG, F32)
    acc_scr[...] = jnp.zeros_like(acc_scr)

    n_sub = FAR_KEYS // FAR_SUB
    last_tile = ks_ref.shape[0] // FAR_KEYS - 1

    def far_scores(jt, rhs_ref, u):
        k0 = pl.multiple_of(jt * FAR_KEYS + u * FAR_SUB, FAR_SUB)
        return _dot(ks_ref[pl.ds(k0, FAR_SUB), :], rhs_ref[...])

    def far_rhs(jt, rhs_ref):
        r0 = pl.multiple_of(jt * FAR_BLOCKS, FAR_BLOCKS)
        rhs_ref[HEAD_DIM:HEAD_DIM + FAR_BLOCKS, :] = selb_far[pl.ds(r0, FAR_BLOCKS), :]

    def far_pv(jt, u, p_bf16):
        k0 = pl.multiple_of(jt * FAR_KEYS + u * FAR_SUB, FAR_SUB)
        return _dot(vsT_ref[:, pl.ds(k0, FAR_SUB)], p_bf16)

    far_rhs(0, rhs_f)
    p_pend[...] = jnp.zeros_like(p_pend)
    a_pend[...] = jnp.ones_like(a_pend)

    def far_body(jt, carry):
        jn = jnp.minimum(jt + 1, last_tile)
        jp = jnp.maximum(jt - 1, 0)
        far_rhs(jn, rhs_n)
        m, acc = m_scr[...], acc_scr[...]
        s_cur = s_nxt[...]
        pend = (jp, n_sub - 1, p_pend[...], a_pend[...])
        for u in range(n_sub):
            if u + 1 < n_sub:
                s_new = far_scores(jt, rhs_f, u + 1)
            else:
                s_nxt[...] = far_scores(jn, rhs_n, 0)
            acc = pend[3] * acc + far_pv(pend[0], pend[1], pend[2])
            m_new = jnp.maximum(m, jnp.max(s_cur, axis=0, keepdims=True))
            alpha = jnp.exp2(m - m_new)
            p = jnp.exp2(s_cur - m_new)
            m = m_new
            pend = (jt, u, p.astype(BF16), alpha)
            if u + 1 < n_sub:
                s_cur = s_new
        p_pend[...] = pend[2]
        a_pend[...] = pend[3]
        far_rhs(jn, rhs_f)
        m_scr[...] = m
        acc_scr[...] = acc
        return carry

    n_far = jnp.maximum(2 * qi - 2 + FAR_BLOCKS - 1, 0) // FAR_BLOCKS
    lax.fori_loop(0, n_far, far_body, 0)
    jl = jnp.maximum(n_far - 1, 0)
    acc = a_pend[...] * acc_scr[...] + far_pv(jl, n_sub - 1, p_pend[...])

    sel_a = tile4(sb_scr[pl.ds(jnp.maximum(2 * qi - 2, 0), 1), :])
    sel_b = tile4(sb_scr[pl.ds(jnp.maximum(2 * qi - 1, 0), 1), :])
    krow = lax.broadcasted_iota(jnp.int32, (Q_BLOCK, QL), 0)
    s_p = sn_scr[0:Q_BLOCK, :] + jnp.where(krow < SEL_BLOCK, sel_a, sel_b)
    s_p = jnp.where(qi >= 1, s_p, NEG)
    s_d = sn_scr[Q_BLOCK:, :]
    m = m_scr[...]
    m_new = jnp.maximum(m, jnp.maximum(jnp.max(s_p, axis=0, keepdims=True), jnp.max(s_d, axis=0, keepdims=True)))
    acc = (jnp.exp2(m - m_new) * acc
           + _dot(vsT_ref[:, pl.ds(tp, Q_BLOCK)], jnp.exp2(s_p - m_new).astype(BF16))
           + _dot(vsT_ref[:, pl.ds(t0, Q_BLOCK)], jnp.exp2(s_d - m_new).astype(BF16)))
    o_s = acc[0:HEAD_DIM, :] / acc[HEAD_DIM:HEAD_DIM + 1, :]

    for h in range(HEADS_PER_GROUP):
        sl = slice(h * Q_BLOCK, (h + 1) * Q_BLOCK)
        o_h = (gate_ref[3 * h:3 * h + 1, :] * o_c[:, sl]
               + gate_ref[3 * h + 1:3 * h + 2, :] * o_s[:, sl]
               + gate_ref[3 * h + 2:3 * h + 3, :] * o_w[:, sl])
        o_ref[h * HEAD_DIM:(h + 1) * HEAD_DIM, :] = o_h.astype(BF16)


def _nsa_attention(qT, kcmp, vcmpT, ks, vsT, kw, vwT, gT, tables):
    bsz, _, seq = qT.shape
    assert seq % FAR_KEYS == 0
    nq = seq // Q_BLOCK
    nc = seq // CMP_STRIDE
    ns = seq // SEL_BLOCK
    near, win0, cmpt = tables
    cs = np.arange(nc) * CMP_STRIDE
    ss = np.arange(ns) * SEL_BLOCK
    ov = (cs[None, :] <= ss[:, None] + SEL_BLOCK - 1) & (cs[None, :] + CMP_BLOCK - 1 >= ss[:, None])
    ov = jnp.asarray(ov, BF16)
    gates = gT.reshape(bsz, 2 * GROUPS, 16, seq)
    per_bg = lambda b, g, i: (b, g, 0, 0)
    return pl.pallas_call(
        _attn_kernel,
        grid=(bsz, GROUPS, nq),
        in_specs=[
            pl.BlockSpec((None, KV_DIM, Q_BLOCK), lambda b, g, i: (b, g, i)),
            pl.BlockSpec((None, None, nc, LANES), per_bg),
            pl.BlockSpec((None, None, HEAD_DIM, nc), per_bg),
            pl.BlockSpec((None, None, seq, LANES), per_bg),
            pl.BlockSpec((None, None, V_ROWS, seq), per_bg),
            pl.BlockSpec((None, None, seq, LANES), per_bg),
            pl.BlockSpec((None, None, V_ROWS, seq), per_bg),
            pl.BlockSpec((None, None, 16, Q_BLOCK), lambda b, g, i: (b, g, 0, i)),
            pl.BlockSpec((None, 2, Q_BLOCK, QL), lambda b, g, i: (g, 0, 0, 0)),
            pl.BlockSpec((Q_BLOCK, QL), lambda b, g, i: (0, 0)),
            pl.BlockSpec((None, 2, 16, QL), lambda b, g, i: (g, 0, 0, 0)),
            pl.BlockSpec((ns, nc), lambda b, g, i: (0, 0)),
        ],
        out_specs=pl.BlockSpec((None, KV_DIM, Q_BLOCK), lambda b, g, i: (b, g, i)),
        out_shape=jax.ShapeDtypeStruct((bsz, HEADS * HEAD_DIM, seq), BF16),
        scratch_shapes=[
            pltpu.VMEM((LANES, QL), BF16),
            pltpu.VMEM((LANES, QL), BF16),
            pltpu.VMEM((ns, Q_BLOCK), F32),
            pltpu.VMEM((ns, QL), BF16),
            pltpu.VMEM((nc, QL), F32),
            pltpu.VMEM((WINDOW + Q_BLOCK, QL), F32),
            pltpu.VMEM((V_ROWS, QL), F32),
            pltpu.VMEM((1, QL), F32),
            pltpu.VMEM((LANES, QL), BF16),
            pltpu.VMEM((FAR_SUB, QL), F32),
            pltpu.VMEM((FAR_SUB, QL), BF16),
            pltpu.VMEM((1, QL), F32),
            pltpu.VMEM((2 * Q_BLOCK, QL), F32),
        ],
        compiler_params=_cparams(("arbitrary", "arbitrary", "arbitrary")),
        name="nsa_attention",
    )(qT, kcmp, vcmpT, ks, vsT, kw, vwT, gates, near, win0, cmpt, ov)


def _oproj_kernel(x_ref, mod_ref, oT_ref, w_ref, o_ref):
    y = _dot(oT_ref[...].T, w_ref[...])
    o_ref[...] = x_ref[...] + mod_ref[5:6, :] * y


def _nsa_out(x, mod, oT, w_out):
    bsz, seq, d = x.shape
    tm = TOKEN_TILE
    return pl.pallas_call(
        _oproj_kernel,
        grid=(bsz, seq // tm),
        in_specs=[
            pl.BlockSpec((None, tm, d), lambda b, i: (b, i, 0)),
            pl.BlockSpec((None, 9, d), lambda b, i: (b, 0, 0)),
            pl.BlockSpec((None, d, tm), lambda b, i: (b, 0, i)),
            _resident((d, d), lambda b, i: (0, 0)),
        ],
        out_specs=pl.BlockSpec((None, tm, d), lambda b, i: (b, i, 0)),
        out_shape=jax.ShapeDtypeStruct(x.shape, F32),
        compiler_params=_cparams(("arbitrary", "arbitrary")),
        name="nsa_out",
    )(x, mod, oT, w_out.astype(BF16))


def _nsa_mixer(x, mod, g, w_in, cmp_pos, cmp_w1, cmp_w2, q_gain, k_gain, w_out, tables):
    qT, kvc, ks, vsT, kw, vwT, gT = _nsa_proj(x, mod, g, w_in, q_gain, k_gain)
    kcmp = _compress(kvc[0], cmp_pos, cmp_w1[0], cmp_w2[0], k_gain[0], True)
    vcmpT = _compress(kvc[1], cmp_pos, cmp_w1[1], cmp_w2[1], k_gain[0], False)
    oT = _nsa_attention(qT, kcmp, vcmpT, ks, vsT, kw, vwT, gT, tables)
    return _nsa_out(x, mod, oT, w_out)


def kernel(x, c, norm_g, ada_w, ada_b, ffn_w_in, ffn_w_out, conv_w_in, conv_w, conv_w_out,
           nsa_w_in, nsa_cmp_pos, nsa_cmp_w1, nsa_cmp_w2, nsa_q_gain, nsa_k_gain, nsa_w_out,
           rel_bias):
    depth = ada_w.shape[0]
    mods = _ada_mod(c, ada_w, ada_b)
    tables = _bias_tables(rel_bias)
    for i in range(depth):
        mod = mods[i]
        x = _ffn(x, mod, norm_g[i, 0], ffn_w_in[i, 0], ffn_w_out[i, 0], 0)
        j = i // 2
        if i % 2 == 0:
            x = _conv_mixer(x, mod, norm_g[i, 1], conv_w_in[j], conv_w[j], conv_w_out[j])
        else:
            x = _nsa_mixer(x, mod, norm_g[i, 1], nsa_w_in[j], nsa_cmp_pos[j], nsa_cmp_w1[j],
                           nsa_cmp_w2[j], nsa_q_gain[j], nsa_k_gain[j], nsa_w_out[j], tables)
        x = _ffn(x, mod, norm_g[i, 2], ffn_w_in[i, 1], ffn_w_out[i, 1], 2)
    return x
```

```python
import functools
import math

import numpy as np
import jax
import jax.numpy as jnp
from jax import lax
from jax.experimental import pallas as pl
from jax.experimental.pallas import tpu as pltpu

F32 = jnp.float32
BF16 = jnp.bfloat16

D_MODEL = 1024
D_FF = 2816
HEADS = 16
HEAD_DIM = 64
GROUPS = 4
HEADS_PER_GROUP = 4
KV_DIM = GROUPS * HEAD_DIM
CMP_BLOCK = 32
CMP_STRIDE = 16
CMP_HIDDEN = 256
SEL_BLOCK = 64
SEL_TOP_N = 16
WINDOW = 512
Q_BLOCK = 128
REL_BUCKETS = 32
REL_MAX_DIST = 128
RMS_EPS = 1e-6
NEG = -1e30
FORCE = 1e9

LANES = 128
QL = HEADS_PER_GROUP * Q_BLOCK
FAR_KEYS = 1024
FAR_BLOCKS = FAR_KEYS // SEL_BLOCK
FAR_SUB = 256
V_ROWS = HEAD_DIM + 16
LOG2E = math.log2(math.e)
FF_CHUNK = 256
TOKEN_TILE = 512
FFN_TOKEN_TILE = 1024
VMEM_LIMIT = 56 * 1024 * 1024


def _dot(a, b):
    return jnp.dot(a, b, preferred_element_type=F32)


def _cparams(sem):
    return pltpu.CompilerParams(dimension_semantics=sem, vmem_limit_bytes=VMEM_LIMIT)


def _resident(shape, index_map):
    return pl.BlockSpec(shape, index_map, pipeline_mode=pl.Buffered(1))


def _adaln(x, g, mod_ref, sub):
    ms = jnp.mean(x * x, axis=-1, keepdims=True)
    shift = mod_ref[3 * sub:3 * sub + 1, :]
    scale = mod_ref[3 * sub + 1:3 * sub + 2, :]
    return (x * lax.rsqrt(ms + RMS_EPS) * g) * (1.0 + scale) + shift


def _split_bf16(v):
    hi = v.astype(BF16)
    lo = (v - hi.astype(F32)).astype(BF16)
    return hi, lo


def _ada_kernel(c_ref, w_ref, b_ref, o_ref):
    c = c_ref[...]
    cond = c * jax.nn.sigmoid(c)
    chi, clo = _split_bf16(cond)
    whi, wlo = _split_bf16(w_ref[...])
    o_ref[...] = _dot(chi, whi) + _dot(chi, wlo) + _dot(clo, whi) + b_ref[...]


def _ada_mod(c, ada_w, ada_b):
    depth, d, n = ada_w.shape
    bsz = c.shape[0]
    rows = 8
    tn = 1024
    c_pad = jnp.zeros((rows, d), F32).at[:bsz].set(c)
    out = pl.pallas_call(
        _ada_kernel,
        grid=(depth, n // tn),
        in_specs=[
            pl.BlockSpec((rows, d), lambda l, j: (0, 0)),
            pl.BlockSpec((None, d, tn), lambda l, j: (l, 0, j)),
            pl.BlockSpec((None, 1, tn), lambda l, j: (l, 0, j)),
        ],
        out_specs=pl.BlockSpec((None, rows, tn), lambda l, j: (l, 0, j)),
        out_shape=jax.ShapeDtypeStruct((depth, rows, n), F32),
        compiler_params=_cparams(("arbitrary", "arbitrary")),
        name="ada_mod",
    )(c_pad, ada_w, ada_b.reshape(depth, 1, n))
    return out[:, :bsz].reshape(depth, bsz, 9, d)


def _ffn_kernel(x_ref, mod_ref, g_ref, win_ref, wout_ref, o_ref, h_scr, acc_scr, *, sub):
    n_half, hm, _ = h_scr.shape
    nchunk = D_FF // FF_CHUNK
    gate = 0.5 * mod_ref[3 * sub + 2:3 * sub + 3, :]
    for hh in range(n_half):
        rows = slice(hh * hm, (hh + 1) * hm)
        h_scr[hh] = _adaln(x_ref[rows, :], g_ref[...], mod_ref, sub).astype(BF16)

        def up(c):
            h = h_scr[hh]
            lo = c * FF_CHUNK
            return (_dot(h, win_ref[:, lo:lo + FF_CHUNK]),
                    _dot(h, win_ref[:, D_FF + lo:D_FF + lo + FF_CHUNK]))

        nxt = up(0)
        for c in range(nchunk):
            gg, uu = nxt
            if c + 1 < nchunk:
                nxt = up(c + 1)
            a = (gg * jax.nn.sigmoid(gg) * uu).astype(BF16)
            down = _dot(a, wout_ref[c * FF_CHUNK:(c + 1) * FF_CHUNK, :])
            if c == 0:
                acc_scr[hh] = down
            else:
                acc_scr[hh] += down
        o_ref[rows, :] = x_ref[rows, :] + gate * acc_scr[hh]


def _ffn(x, mod, g, w_in_all, w_out_all, layer, slot, sub):
    bsz, seq, d = x.shape
    tm = FFN_TOKEN_TILE
    hm = tm // 2
    return pl.pallas_call(
        functools.partial(_ffn_kernel, sub=sub),
        grid=(bsz, seq // tm),
        in_specs=[
            pl.BlockSpec((None, tm, d), lambda b, i: (b, i, 0)),
            pl.BlockSpec((None, 9, d), lambda b, i: (b, 0, 0)),
            pl.BlockSpec((1, d), lambda b, i: (0, 0)),
            _resident((None, None, d, 2 * D_FF), lambda b, i: (layer, slot, 0, 0)),
            _resident((None, None, D_FF, d), lambda b, i: (layer, slot, 0, 0)),
        ],
        out_specs=pl.BlockSpec((None, tm, d), lambda b, i: (b, i, 0)),
        out_shape=jax.ShapeDtypeStruct(x.shape, F32),
        scratch_shapes=[pltpu.VMEM((2, hm, d), BF16), pltpu.VMEM((2, hm, d), F32)],
        compiler_params=_cparams(("arbitrary", "arbitrary")),
        name="ffn",
    )(x, mod, g.reshape(1, d), w_in_all, w_out_all)


def _conv_kernel(x_ref, mod_ref, g_ref, win_ref, cw_ref, wout_ref, o_ref, ubuf):
    tm = x_ref.shape[0]

    @pl.when(pl.program_id(1) == 0)
    def _():
        ubuf[0:8, :] = jnp.zeros((8, ubuf.shape[1]), F32)

    x = x_ref[...]
    hb = _adaln(x, g_ref[...], mod_ref, 1).astype(BF16)
    d = x_ref.shape[1]
    bg = _dot(hb, win_ref[:, 0:d])
    u = _dot(hb, win_ref[:, d:2 * d]) * _dot(hb, win_ref[:, 2 * d:3 * d])
    ubuf[8:8 + tm, :] = u
    y = cw_ref[2:3, :] * u + cw_ref[1:2, :] * ubuf[7:7 + tm, :] + cw_ref[0:1, :] * ubuf[6:6 + tm, :]
    ubuf[0:8, :] = ubuf[tm:tm + 8, :]
    out = _dot((bg * y).astype(BF16), wout_ref[...])
    o_ref[...] = x + mod_ref[5:6, :] * out


def _conv_mixer(x, mod, g, w_in_all, conv_w, w_out_all, layer):
    bsz, seq, d = x.shape
    tm = TOKEN_TILE
    return pl.pallas_call(
        _conv_kernel,
        grid=(bsz, seq // tm),
        in_specs=[
            pl.BlockSpec((None, tm, d), lambda b, i: (b, i, 0)),
            pl.BlockSpec((None, 9, d), lambda b, i: (b, 0, 0)),
            pl.BlockSpec((1, d), lambda b, i: (0, 0)),
            _resident((None, d, 3 * d), lambda b, i: (layer, 0, 0)),
            pl.BlockSpec((3, d), lambda b, i: (0, 0)),
            _resident((None, d, d), lambda b, i: (layer, 0, 0)),
        ],
        out_specs=pl.BlockSpec((None, tm, d), lambda b, i: (b, i, 0)),
        out_shape=jax.ShapeDtypeStruct(x.shape, F32),
        scratch_shapes=[pltpu.VMEM((tm + 8, d), F32)],
        compiler_params=_cparams(("arbitrary", "arbitrary")),
        name="conv_mixer",
    )(x, mod, g.reshape(1, d), w_in_all, conv_w, w_out_all)


def _head_ms(sq, p_ref):
    hi, lo = _split_bf16(sq)
    return _dot(hi, p_ref[...]) + _dot(lo, p_ref[...])


def _proj_kernel(x_ref, mod_ref, g_ref, w_ref, wT_ref, p_ref, qg_ref, kg_ref,
                 qT_ref, kvc_ref, ks_ref, vsT_ref, kw_ref, vwT_ref, gT_ref):
    tm = x_ref.shape[0]
    x = x_ref[...]
    hb = _adaln(x, g_ref[...], mod_ref, 1).astype(BF16)

    def col(i, width=KV_DIM):
        return _dot(hb, w_ref[:, i:i + width])

    def colT(i, width=KV_DIM):
        return lax.dot_general(wT_ref[i:i + width, :], hb, (((1,), (1,)), ((), ())),
                               preferred_element_type=F32)

    qg = jnp.concatenate([qg_ref[...]] * (tm // LANES), axis=1)
    for c in range(GROUPS):
        qc = colT(c * KV_DIM)
        ms = jnp.mean((qc * qc).reshape(HEADS_PER_GROUP, HEAD_DIM, tm), axis=1, keepdims=True)
        ms = jnp.broadcast_to(ms, (HEADS_PER_GROUP, HEAD_DIM, tm)).reshape(KV_DIM, tm)
        qn = qc * lax.rsqrt(ms + RMS_EPS) * qg
        qT_ref[c * KV_DIM:(c + 1) * KV_DIM, :] = qn.astype(BF16)
    vsT = colT(HEADS * HEAD_DIM).astype(BF16)
    vwT = colT(HEADS * HEAD_DIM + KV_DIM).astype(BF16)
    rid = lax.broadcasted_iota(jnp.int32, (V_ROWS - HEAD_DIM, tm), 0)
    ones_rows = jnp.where(rid == 0, 1.0, 0.0).astype(BF16)
    for g in range(GROUPS):
        vsT_ref[g] = jnp.concatenate([vsT[g * HEAD_DIM:(g + 1) * HEAD_DIM, :], ones_rows], axis=0)
        vwT_ref[g] = jnp.concatenate([vwT[g * HEAD_DIM:(g + 1) * HEAD_DIM, :], ones_rows], axis=0)
    gT_ref[...] = jax.nn.sigmoid(colT(HEADS * HEAD_DIM + 2 * KV_DIM, LANES))

    for kv in range(2):
        kvc = col(kv * KV_DIM)
        for half in range(2):
            kvc_ref[kv, half] = kvc[:, half * LANES:(half + 1) * LANES]

    pos = pl.program_id(1) * tm + lax.broadcasted_iota(jnp.int32, (tm, HEAD_DIM), 0)
    lane = lax.broadcasted_iota(jnp.int32, (tm, HEAD_DIM), 1)
    onehot = jnp.where(lane == ((pos // SEL_BLOCK) % FAR_BLOCKS), 1.0, 0.0)
    zeros = jnp.zeros((tm, HEAD_DIM), F32)

    ks = col(2 * KV_DIM)
    ksn = ks * lax.rsqrt(_head_ms(ks * ks, p_ref) + RMS_EPS) * kg_ref[0:1, :]
    kw = col(3 * KV_DIM)
    kwn = kw * lax.rsqrt(_head_ms(kw * kw, p_ref) + RMS_EPS) * kg_ref[1:2, :]
    for g in range(GROUPS):
        sl = slice(g * HEAD_DIM, (g + 1) * HEAD_DIM)
        ks_ref[g] = jnp.concatenate([ksn[:, sl], onehot], axis=1).astype(BF16)
        kw_ref[g] = jnp.concatenate([kwn[:, sl], zeros], axis=1).astype(BF16)


def _nsa_proj(x, mod, g, w_in, q_gain, k_gain):
    bsz, seq, d = x.shape
    tm = TOKEN_TILE
    nq = HEADS * HEAD_DIM
    q, kc, vc, ks, vs, kw, vw, gl = jnp.split(w_in, [nq + i * KV_DIM for i in range(7)], axis=1)
    gl = gl.reshape(d, GROUPS, 3 * HEADS_PER_GROUP)
    gl = jnp.pad(gl, ((0, 0), (0, GROUPS), (0, 4))).reshape(d, LANES)
    w = jnp.concatenate([kc, vc, ks, kw], axis=1).astype(BF16)
    wT = jnp.concatenate([q, vs, vw, gl], axis=1).T.astype(BF16)
    hid = np.arange(KV_DIM) // HEAD_DIM
    pmat = jnp.asarray((hid[:, None] == hid[None, :]) / HEAD_DIM, BF16)
    qg = jnp.tile(q_gain, HEADS_PER_GROUP) * (HEAD_DIM ** -0.5 * LOG2E)
    qg = jnp.broadcast_to(qg[:, None], (KV_DIM, LANES))
    kg = jnp.stack([jnp.tile(k_gain[1], GROUPS), jnp.tile(k_gain[2], GROUPS)])
    kv_spec = pl.BlockSpec((None, GROUPS, tm, LANES), lambda b, i: (b, 0, i, 0))
    vT_spec = pl.BlockSpec((None, GROUPS, V_ROWS, tm), lambda b, i: (b, 0, 0, i))
    kv_shape = jax.ShapeDtypeStruct((bsz, GROUPS, seq, LANES), BF16)
    vT_shape = jax.ShapeDtypeStruct((bsz, GROUPS, V_ROWS, seq), BF16)
    return pl.pallas_call(
        _proj_kernel,
        grid=(bsz, seq // tm),
        in_specs=[
            pl.BlockSpec((None, tm, d), lambda b, i: (b, i, 0)),
            pl.BlockSpec((None, 9, d), lambda b, i: (b, 0, 0)),
            pl.BlockSpec((1, d), lambda b, i: (0, 0)),
            _resident(w.shape, lambda b, i: (0, 0)),
            _resident(wT.shape, lambda b, i: (0, 0)),
            pl.BlockSpec((KV_DIM, KV_DIM), lambda b, i: (0, 0)),
            pl.BlockSpec((KV_DIM, LANES), lambda b, i: (0, 0)),
            pl.BlockSpec((2, KV_DIM), lambda b, i: (0, 0)),
        ],
        out_specs=[
            pl.BlockSpec((None, nq, tm), lambda b, i: (b, 0, i)),
            pl.BlockSpec((2, 2, None, tm, LANES), lambda b, i: (0, 0, b, i, 0)),
            kv_spec, vT_spec, kv_spec, vT_spec,
            pl.BlockSpec((None, LANES, tm), lambda b, i: (b, 0, i)),
        ],
        out_shape=[
            jax.ShapeDtypeStruct((bsz, nq, seq), BF16),
            jax.ShapeDtypeStruct((2, 2, bsz, seq, LANES), F32),
            kv_shape, vT_shape, kv_shape, vT_shape,
            jax.ShapeDtypeStruct((bsz, LANES, seq), F32),
        ],
        compiler_params=_cparams(("arbitrary", "arbitrary")),
        name="nsa_proj",
    )(x, mod, g.reshape(1, d), w, wT, pmat, qg, kg)


def _gelu_tanh(x):
    return 0.5 * x * (1.0 + jnp.tanh(math.sqrt(2.0 / math.pi) * (x + 0.044715 * (x * x * x))))


def _compress_kernel(kv_ref, pos_ref, w1_ref, w2_ref, kg_ref, o_ref, acc_lo, acc_hi, *, is_key):
    nc = acc_lo.shape[1]
    half = CMP_BLOCK // 2
    acc_lo[...] = jnp.zeros_like(acc_lo)
    acc_hi[...] = jnp.zeros_like(acc_hi)
    for l in range(half):
        xl = [kv_ref[j, pl.ds(l, nc, stride=CMP_STRIDE), :] for j in range(2)]
        for g in range(GROUPS):
            xg = xl[g // 2][:, (g % 2) * HEAD_DIM:(g % 2 + 1) * HEAD_DIM]
            lo = (xg + pos_ref[l:l + 1, :]).astype(BF16)
            hi = (xg + pos_ref[half + l:half + l + 1, :]).astype(BF16)
            acc_lo[g] += _dot(lo, w1_ref[l * HEAD_DIM:(l + 1) * HEAD_DIM, :])
            acc_hi[g] += _dot(hi, w1_ref[(half + l) * HEAD_DIM:(half + l + 1) * HEAD_DIM, :])
    row = lax.broadcasted_iota(jnp.int32, (nc, CMP_HIDDEN), 0)
    for g in range(GROUPS):
        nxt = jnp.where(row < nc - 1, pltpu.roll(acc_hi[g], nc - 1, 0), 0.0)
        hdn = _gelu_tanh(acc_lo[g] + nxt).astype(BF16)
        if is_key:
            out = _dot(hdn, w2_ref[...])
            ms = jnp.sum(out * out, axis=-1, keepdims=True) * (1.0 / HEAD_DIM)
            o_ref[g] = (out * lax.rsqrt(ms + RMS_EPS) * kg_ref[...]).astype(BF16)
        else:
            outT = lax.dot_general(w2_ref[...], hdn, (((1,), (1,)), ((), ())),
                                   preferred_element_type=F32)
            o_ref[g] = outT.astype(BF16)


def _compress(kvc, pos, w1, w2, k_gain, is_key):
    _, _, bsz, seq, _ = kvc.shape
    which = 0 if is_key else 1
    nc = seq // CMP_STRIDE
    kg = jnp.pad(k_gain, (0, LANES - HEAD_DIM)).reshape(1, LANES)
    if is_key:
        w2p = jnp.pad(w2, ((0, 0), (0, LANES - HEAD_DIM))).astype(BF16)
    else:
        w2p = w2.T.astype(BF16)
    if is_key:
        out_spec = pl.BlockSpec((None, GROUPS, nc, LANES), lambda b: (b, 0, 0, 0))
        out_shape = jax.ShapeDtypeStruct((bsz, GROUPS, nc, LANES), BF16)
    else:
        out_spec = pl.BlockSpec((None, GROUPS, HEAD_DIM, nc), lambda b: (b, 0, 0, 0))
        out_shape = jax.ShapeDtypeStruct((bsz, GROUPS, HEAD_DIM, nc), BF16)
    return pl.pallas_call(
        functools.partial(_compress_kernel, is_key=is_key),
        grid=(bsz,),
        in_specs=[
            pl.BlockSpec((None, 2, None, seq, LANES), lambda b: (which, 0, b, 0, 0)),
            pl.BlockSpec((CMP_BLOCK, HEAD_DIM), lambda b: (0, 0)),
            pl.BlockSpec((CMP_BLOCK * HEAD_DIM, CMP_HIDDEN), lambda b: (0, 0)),
            pl.BlockSpec(w2p.shape, lambda b: (0, 0)),
            pl.BlockSpec((1, LANES), lambda b: (0, 0)),
        ],
        out_specs=out_spec,
        out_shape=out_shape,
        scratch_shapes=[pltpu.VMEM((GROUPS, nc, CMP_HIDDEN), F32),
                        pltpu.VMEM((GROUPS, nc, CMP_HIDDEN), F32)],
        compiler_params=_cparams(("arbitrary",)),
        name="compress_k" if is_key else "compress_v",
    )(kvc, pos, w1.astype(BF16), w2p, kg)


def _t5_bucket_np(dist):
    n = np.maximum(dist, 0)
    max_exact = REL_BUCKETS // 2
    nf = np.maximum(n, 1).astype(np.float32)
    large = max_exact + (np.log(nf / max_exact) / math.log(REL_MAX_DIST / max_exact)
                         * (REL_BUCKETS - max_exact)).astype(np.int32)
    large = np.minimum(large, REL_BUCKETS - 1)
    return np.where(n < max_exact, n, large)


def _bias_tables(rel_bias):
    rb = rel_bias.astype(F32).T
    rel = ((rb - rb[:, REL_BUCKETS - 1:]) * LOG2E).reshape(GROUPS, HEADS_PER_GROUP, REL_BUCKETS)
    ql = np.arange(Q_BLOCK)[None, :]
    kl = np.arange(Q_BLOCK)[:, None]
    d_lo, d_hi = -2 * Q_BLOCK, 3 * Q_BLOCK
    dvec = np.arange(d_lo, d_hi)
    onehot = jnp.asarray(np.eye(REL_BUCKETS, dtype=np.float32)[_t5_bucket_np(dvec)])
    vec = jnp.einsum("ghb,db->ghd", rel, onehot, precision=lax.Precision.HIGHEST)
    vec = jnp.where(jnp.asarray(dvec >= 0), vec, NEG)

    def table(dist):
        rows = [vec[:, :, int(d0) - d_lo:int(d0) - d_lo + Q_BLOCK] for d0 in dist[:, 0]]
        return jnp.stack(rows, axis=1).reshape(GROUPS, dist.shape[0], QL)

    near = jnp.stack([table(ql - kl + Q_BLOCK), table(ql - kl)], axis=1)
    win0 = np.where(ql - kl + WINDOW < WINDOW, 0.0, NEG).astype(np.float32)
    win0 = jnp.asarray(np.tile(win0, (1, HEADS_PER_GROUP)))
    r = np.arange(16)[:, None]
    cmp = table(ql - CMP_STRIDE * (r - 8) - (CMP_BLOCK - 1))
    cmp_first = jnp.concatenate([cmp[:, 8:], jnp.zeros_like(cmp[:, 8:])], axis=1)
    return near, win0, jnp.stack([cmp, cmp_first], axis=1)


def _attn_kernel(qT_ref, kc_ref, vcT_ref, ks_ref, vsT_ref, kw_ref, vwT_ref, gate_ref,
                 near_ref, win0_ref, cmpt_ref, ov_ref, o_ref,
                 rhs_z, rhs_f, sb_scr, selb_far, sc_scr, sw_scr, acc_scr, m_scr,
                 rhs_n, s_nxt, p_pend, a_pend, sn_scr):
    qi = pl.program_id(2)
    t0 = pl.multiple_of(qi * Q_BLOCK, Q_BLOCK)
    nc = kc_ref.shape[0]
    ns = ov_ref.shape[0]

    q4 = qT_ref[...]
    qt = jnp.concatenate([q4[h * HEAD_DIM:(h + 1) * HEAD_DIM, :] for h in range(HEADS_PER_GROUP)], axis=1)
    rhs_z[0:HEAD_DIM, :] = qt
    rhs_z[HEAD_DIM:, :] = jnp.zeros((LANES - HEAD_DIM, QL), BF16)
    rhs_f[0:HEAD_DIM, :] = qt
    rhs_f[HEAD_DIM:, :] = jnp.zeros((LANES - HEAD_DIM, QL), BF16)
    rhs_n[0:HEAD_DIM, :] = qt
    rhs_n[HEAD_DIM:, :] = jnp.zeros((LANES - HEAD_DIM, QL), BF16)

    def tile4(v):
        return jnp.concatenate([v] * HEADS_PER_GROUP, axis=1)

    sc_scr[...] = _dot(kc_ref[...], rhs_z[...])
    n_win = WINDOW // Q_BLOCK + 1
    starts = []
    for a in range(n_win):
        start = t0 - Q_BLOCK * (n_win - 1 - a)
        st = pl.multiple_of(jnp.maximum(start, 0), Q_BLOCK)
        starts.append(st)
        s = _dot(kw_ref[pl.ds(st, Q_BLOCK), :], rhs_z[...])
        if a == 0:
            s = s + win0_ref[...]
        elif a == n_win - 2:
            s = s + near_ref[0]
        elif a == n_win - 1:
            s = s + near_ref[1]
        sw_scr[a * Q_BLOCK:(a + 1) * Q_BLOCK, :] = jnp.where(start >= 0, s, NEG)
    tp = pl.multiple_of(jnp.maximum(t0 - Q_BLOCK, 0), Q_BLOCK)
    sn_scr[0:Q_BLOCK, :] = _dot(ks_ref[pl.ds(tp, Q_BLOCK), :], rhs_z[...]) + near_ref[0]
    sn_scr[Q_BLOCK:, :] = _dot(ks_ref[pl.ds(t0, Q_BLOCK), :], rhs_z[...]) + near_ref[1]
    s_nxt[...] = _dot(ks_ref[0:FAR_SUB, :], rhs_z[...])

    st = pl.multiple_of(jnp.maximum(8 * qi - 8, 0), 8)
    sc_scr[pl.ds(st, 16), :] += jnp.where(qi == 0, cmpt_ref[1], cmpt_ref[0])
    row = lax.broadcasted_iota(jnp.int32, (nc, QL), 0)
    s = sc_scr[...] + jnp.where(row < 8 * qi + 8, 0.0, NEG)
    m = jnp.max(s, axis=0, keepdims=True)
    p = jnp.exp2(s - m)
    l = jnp.sum(p, axis=0, keepdims=True)
    inv = jnp.where(m > 0.1 * NEG, 1.0 / l, 0.0)
    o_c = _dot(vcT_ref[...], p.astype(BF16)) * inv
    pn = p * inv
    psum = (pn[:, 0:Q_BLOCK] + pn[:, Q_BLOCK:2 * Q_BLOCK]
            + pn[:, 2 * Q_BLOCK:3 * Q_BLOCK] + pn[:, 3 * Q_BLOCK:4 * Q_BLOCK])
    phi, plo = _split_bf16(psum)
    imp = _dot(ov_ref[...], phi) + _dot(ov_ref[...], plo)

    jrow = lax.broadcasted_iota(jnp.int32, (ns, Q_BLOCK), 0)
    qlane = lax.broadcasted_iota(jnp.int32, (ns, Q_BLOCK), 1)
    cur = 2 * qi + jnp.where(qlane >= SEL_BLOCK, 1, 0)
    future = jrow > cur
    forced = (jrow == 0) | (jrow == cur) | (jrow == cur - 1)
    v = jnp.where(future, NEG, imp)
    v = jnp.where(forced, -jnp.inf, v)
    rowf = jrow.astype(F32)
    picked = jnp.where(forced, 1.0, 0.0)
    win = {"m": None, "o": None}

    def win_max(a):
        ma = jnp.max(sw_scr[a * Q_BLOCK:(a + 1) * Q_BLOCK, :], axis=0, keepdims=True)
        win["m"] = ma if win["m"] is None else jnp.maximum(win["m"], ma)

    def win_pv(a):
        pb = jnp.exp2(sw_scr[a * Q_BLOCK:(a + 1) * Q_BLOCK, :] - win["m"]).astype(BF16)
        pv = _dot(vwT_ref[:, pl.ds(starts[a], Q_BLOCK)], pb)
        win["o"] = pv if win["o"] is None else win["o"] + pv

    filler = ([functools.partial(win_max, a) for a in range(n_win)]
              + [functools.partial(win_pv, a) for a in range(n_win)])
    for r in range(SEL_TOP_N - 3):
        mx = jnp.max(v, axis=0, keepdims=True)
        idx = jnp.min(jnp.where(v == mx, rowf, float(ns)), axis=0, keepdims=True)
        hit = rowf == idx
        picked = jnp.where(hit, 1.0, picked)
        v = jnp.where(hit, -jnp.inf, v)
        if r < len(filler):
            filler[r]()
    o_w = win["o"][0:HEAD_DIM, :] / win["o"][HEAD_DIM:HEAD_DIM + 1, :]
    sb = jnp.where((picked > 0.0) & jnp.logical_not(future), 0.0, NEG)
    sb_scr[...] = sb
    sb_far = jnp.where(jrow < 2 * qi - 2, sb, NEG)
    selb_far[...] = tile4(sb_far).astype(BF16)

    s_nxt[...] += jnp.concatenate(
        [jnp.broadcast_to(tile4(sb_far[b:b + 1, :]), (SEL_BLOCK, QL)) for b in range(FAR_SUB // SEL_BLOCK)], axis=0)
    m_scr[...] = jnp.full(m_scr.shape, NEG, F32)
    acc_scr[...] = jnp.zeros_like(acc_scr)

    n_sub = FAR_KEYS // FAR_SUB
    last_tile = ks_ref.shape[0] // FAR_KEYS - 1

    def far_scores(jt, rhs_ref, u):
        k0 = pl.multiple_of(jt * FAR_KEYS + u * FAR_SUB, FAR_SUB)
        return _dot(ks_ref[pl.ds(k0, FAR_SUB), :], rhs_ref[...])

    def far_rhs(jt, rhs_ref):
        r0 = pl.multiple_of(jt * FAR_BLOCKS, FAR_BLOCKS)
        rhs_ref[HEAD_DIM:HEAD_DIM + FAR_BLOCKS, :] = selb_far[pl.ds(r0, FAR_BLOCKS), :]

    def far_pv(jt, u, p_bf16):
        k0 = pl.multiple_of(jt * FAR_KEYS + u * FAR_SUB, FAR_SUB)
        return _dot(vsT_ref[:, pl.ds(k0, FAR_SUB)], p_bf16)

    far_rhs(0, rhs_f)
    p_pend[...] = jnp.zeros_like(p_pend)
    a_pend[...] = jnp.ones_like(a_pend)

    def far_body(jt, carry):
        jn = jnp.minimum(jt + 1, last_tile)
        jp = jnp.maximum(jt - 1, 0)
        far_rhs(jn, rhs_n)
        m, acc = m_scr[...], acc_scr[...]
        s_cur = s_nxt[...]
        pend = (jp, n_sub - 1, p_pend[...], a_pend[...])
        for u in range(n_sub):
            if u + 1 < n_sub:
                s_new = far_scores(jt, rhs_f, u + 1)
            else:
                s_nxt[...] = far_scores(jn, rhs_n, 0)
            acc = pend[3] * acc + far_pv(pend[0], pend[1], pend[2])
            m_new = jnp.maximum(m, jnp.max(s_cur, axis=0, keepdims=True))
            alpha = jnp.exp2(m - m_new)
            p = jnp.exp2(s_cur - m_new)
            m = m_new
            pend = (jt, u, p.astype(BF16), alpha)
            if u + 1 < n_sub:
                s_cur = s_new
        p_pend[...] = pend[2]
        a_pend[...] = pend[3]
        far_rhs(jn, rhs_f)
        m_scr[...] = m
        acc_scr[...] = acc
        return carry

    n_far = jnp.maximum(2 * qi - 2 + FAR_BLOCKS - 1, 0) // FAR_BLOCKS
    lax.fori_loop(0, n_far, far_body, 0)
    jl = jnp.maximum(n_far - 1, 0)
    acc = a_pend[...] * acc_scr[...] + far_pv(jl, n_sub - 1, p_pend[...])

    sel_a = tile4(sb_scr[pl.ds(jnp.maximum(2 * qi - 2, 0), 1), :])
    sel_b = tile4(sb_scr[pl.ds(jnp.maximum(2 * qi - 1, 0), 1), :])
    krow = lax.broadcasted_iota(jnp.int32, (Q_BLOCK, QL), 0)
    s_p = sn_scr[0:Q_BLOCK, :] + jnp.where(krow < SEL_BLOCK, sel_a, sel_b)
    s_p = jnp.where(qi >= 1, s_p, NEG)
    s_d = sn_scr[Q_BLOCK:, :]
    m = m_scr[...]
    m_new = jnp.maximum(m, jnp.maximum(jnp.max(s_p, axis=0, keepdims=True), jnp.max(s_d, axis=0, keepdims=True)))
    acc = (jnp.exp2(m - m_new) * acc
           + _dot(vsT_ref[:, pl.ds(tp, Q_BLOCK)], jnp.exp2(s_p - m_new).astype(BF16))
           + _dot(vsT_ref[:, pl.ds(t0, Q_BLOCK)], jnp.exp2(s_d - m_new).astype(BF16)))
    o_s = acc[0:HEAD_DIM, :] / acc[HEAD_DIM:HEAD_DIM + 1, :]

    for h in range(HEADS_PER_GROUP):
        sl = slice(h * Q_BLOCK, (h + 1) * Q_BLOCK)
        o_h = (gate_ref[3 * h:3 * h + 1, :] * o_c[:, sl]
               + gate_ref[3 * h + 1:3 * h + 2, :] * o_s[:, sl]
               + gate_ref[3 * h + 2:3 * h + 3, :] * o_w[:, sl])
        o_ref[h * HEAD_DIM:(h + 1) * HEAD_DIM, :] = o_h.astype(BF16)


def _nsa_attention(qT, kcmp, vcmpT, ks, vsT, kw, vwT, gT, tables):
    bsz, _, seq = qT.shape
    assert seq % FAR_KEYS == 0
    nq = seq // Q_BLOCK
    nc = seq // CMP_STRIDE
    ns = seq // SEL_BLOCK
    near, win0, cmpt = tables
    cs = np.arange(nc) * CMP_STRIDE
    ss = np.arange(ns) * SEL_BLOCK
    ov = (cs[None, :] <= ss[:, None] + SEL_BLOCK - 1) & (cs[None, :] + CMP_BLOCK - 1 >= ss[:, None])
    ov = jnp.asarray(ov, BF16)
    gates = gT.reshape(bsz, 2 * GROUPS, 16, seq)
    per_bg = lambda b, g, i: (b, g, 0, 0)
    return pl.pallas_call(
        _attn_kernel,
        grid=(bsz, GROUPS, nq),
        in_specs=[
            pl.BlockSpec((None, KV_DIM, Q_BLOCK), lambda b, g, i: (b, g, i)),
            pl.BlockSpec((None, None, nc, LANES), per_bg),
            pl.BlockSpec((None, None, HEAD_DIM, nc), per_bg),
            pl.BlockSpec((None, None, seq, LANES), per_bg),
            pl.BlockSpec((None, None, V_ROWS, seq), per_bg),
            pl.BlockSpec((None, None, seq, LANES), per_bg),
            pl.BlockSpec((None, None, V_ROWS, seq), per_bg),
            pl.BlockSpec((None, None, 16, Q_BLOCK), lambda b, g, i: (b, g, 0, i)),
            pl.BlockSpec((None, 2, Q_BLOCK, QL), lambda b, g, i: (g, 0, 0, 0)),
            pl.BlockSpec((Q_BLOCK, QL), lambda b, g, i: (0, 0)),
            pl.BlockSpec((None, 2, 16, QL), lambda b, g, i: (g, 0, 0, 0)),
            pl.BlockSpec((ns, nc), lambda b, g, i: (0, 0)),
        ],
        out_specs=pl.BlockSpec((None, KV_DIM, Q_BLOCK), lambda b, g, i: (b, g, i)),
        out_shape=jax.ShapeDtypeStruct((bsz, HEADS * HEAD_DIM, seq), BF16),
        scratch_shapes=[
            pltpu.VMEM((LANES, QL), BF16),
            pltpu.VMEM((LANES, QL), BF16),
            pltpu.VMEM((ns, Q_BLOCK), F32),
            pltpu.VMEM((ns, QL), BF16),
            pltpu.VMEM((nc, QL), F32),
            pltpu.VMEM((WINDOW + Q_BLOCK, QL), F32),
            pltpu.VMEM((V_ROWS, QL), F32),
            pltpu.VMEM((1, QL), F32),
            pltpu.VMEM((LANES, QL), BF16),
            pltpu.VMEM((FAR_SUB, QL), F32),
            pltpu.VMEM((FAR_SUB, QL), BF16),
            pltpu.VMEM((1, QL), F32),
            pltpu.VMEM((2 * Q_BLOCK, QL), F32),
        ],
        compiler_params=_cparams(("arbitrary", "arbitrary", "arbitrary")),
        name="nsa_attention",
    )(qT, kcmp, vcmpT, ks, vsT, kw, vwT, gates, near, win0, cmpt, ov)


def _oproj_kernel(x_ref, mod_ref, oT_ref, w_ref, o_ref):
    y = _dot(oT_ref[...].T, w_ref[...])
    o_ref[...] = x_ref[...] + mod_ref[5:6, :] * y


def _nsa_out(x, mod, oT, w_out):
    bsz, seq, d = x.shape
    tm = TOKEN_TILE
    return pl.pallas_call(
        _oproj_kernel,
        grid=(bsz, seq // tm),
        in_specs=[
            pl.BlockSpec((None, tm, d), lambda b, i: (b, i, 0)),
            pl.BlockSpec((None, 9, d), lambda b, i: (b, 0, 0)),
            pl.BlockSpec((None, d, tm), lambda b, i: (b, 0, i)),
            _resident((d, d), lambda b, i: (0, 0)),
        ],
        out_specs=pl.BlockSpec((None, tm, d), lambda b, i: (b, i, 0)),
        out_shape=jax.ShapeDtypeStruct(x.shape, F32),
        compiler_params=_cparams(("arbitrary", "arbitrary")),
        name="nsa_out",
    )(x, mod, oT, w_out.astype(BF16))


def _nsa_mixer(x, mod, g, w_in, cmp_pos, cmp_w1, cmp_w2, q_gain, k_gain, w_out, tables):
    qT, kvc, ks, vsT, kw, vwT, gT = _nsa_proj(x, mod, g, w_in, q_gain, k_gain)
    kcmp = _compress(kvc, cmp_pos, cmp_w1[0], cmp_w2[0], k_gain[0], True)
    vcmpT = _compress(kvc, cmp_pos, cmp_w1[1], cmp_w2[1], k_gain[0], False)
    oT = _nsa_attention(qT, kcmp, vcmpT, ks, vsT, kw, vwT, gT, tables)
    return _nsa_out(x, mod, oT, w_out)


def kernel(x, c, norm_g, ada_w, ada_b, ffn_w_in, ffn_w_out, conv_w_in, conv_w, conv_w_out,
           nsa_w_in, nsa_cmp_pos, nsa_cmp_w1, nsa_cmp_w2, nsa_q_gain, nsa_k_gain, nsa_w_out,
           rel_bias):
    depth = ada_w.shape[0]
    mods = _ada_mod(c, ada_w, ada_b)
    tables = _bias_tables(rel_bias)
    ffn_in = ffn_w_in.astype(BF16)
    ffn_out = ffn_w_out.astype(BF16)
    conv_in = conv_w_in.astype(BF16)
    conv_out = conv_w_out.astype(BF16)
    for i in range(depth):
        mod = mods[i]
        x = _ffn(x, mod, norm_g[i, 0], ffn_in, ffn_out, i, 0, 0)
        j = i // 2
        if i % 2 == 0:
            x = _conv_mixer(x, mod, norm_g[i, 1], conv_in, conv_w[j], conv_out, j)
        else:
            x = _nsa_mixer(x, mod, norm_g[i, 1], nsa_w_in[j], nsa_cmp_pos[j], nsa_cmp_w1[j],
                           nsa_cmp_w2[j], nsa_q_gain[j], nsa_k_gain[j], nsa_w_out[j], tables)
        x = _ffn(x, mod, norm_g[i, 2], ffn_in, ffn_out, i, 1, 2)
    return x
```

```python
import functools
import math

import numpy as np
import jax
import jax.numpy as jnp
from jax import lax
from jax.experimental import pallas as pl
from jax.experimental.pallas import tpu as pltpu

F32 = jnp.float32
BF16 = jnp.bfloat16

D_MODEL = 1024
D_FF = 2816
HEADS = 16
HEAD_DIM = 64
GROUPS = 4
HEADS_PER_GROUP = 4
KV_DIM = GROUPS * HEAD_DIM
CMP_BLOCK = 32
CMP_STRIDE = 16
CMP_HIDDEN = 256
SEL_BLOCK = 64
SEL_TOP_N = 16
WINDOW = 512
Q_BLOCK = 128
REL_BUCKETS = 32
REL_MAX_DIST = 128
RMS_EPS = 1e-6
NEG = -1e30
FORCE = 1e9

LANES = 128
QL = HEADS_PER_GROUP * Q_BLOCK
FAR_KEYS = 1024
FAR_BLOCKS = FAR_KEYS // SEL_BLOCK
FAR_SUB = 256
Q_TILES = 2
V_ROWS = HEAD_DIM + 16
LOG2E = math.log2(math.e)
FF_CHUNK = 256
TOKEN_TILE = 512
FFN_TOKEN_TILE = 1024
VMEM_LIMIT = 56 * 1024 * 1024


def _dot(a, b):
    return jnp.dot(a, b, preferred_element_type=F32)


def _cparams(sem):
    return pltpu.CompilerParams(dimension_semantics=sem, vmem_limit_bytes=VMEM_LIMIT)


def _resident(shape, index_map):
    return pl.BlockSpec(shape, index_map, pipeline_mode=pl.Buffered(1))


def _adaln(x, g, mod_ref, sub):
    ms = jnp.mean(x * x, axis=-1, keepdims=True)
    shift = mod_ref[3 * sub:3 * sub + 1, :]
    scale = mod_ref[3 * sub + 1:3 * sub + 2, :]
    return (x * lax.rsqrt(ms + RMS_EPS) * g) * (1.0 + scale) + shift


def _split_bf16(v):
    hi = v.astype(BF16)
    lo = (v - hi.astype(F32)).astype(BF16)
    return hi, lo


def _ada_kernel(c_ref, w_ref, b_ref, o_ref):
    c = c_ref[...]
    cond = c * jax.nn.sigmoid(c)
    chi, clo = _split_bf16(cond)
    whi, wlo = _split_bf16(w_ref[...])
    o_ref[...] = _dot(chi, whi) + _dot(chi, wlo) + _dot(clo, whi) + b_ref[...]


def _ada_mod(c, ada_w, ada_b):
    depth, d, n = ada_w.shape
    bsz = c.shape[0]
    rows = 8
    tn = 1024
    c_pad = jnp.zeros((rows, d), F32).at[:bsz].set(c)
    out = pl.pallas_call(
        _ada_kernel,
        grid=(depth, n // tn),
        in_specs=[
            pl.BlockSpec((rows, d), lambda l, j: (0, 0)),
            pl.BlockSpec((None, d, tn), lambda l, j: (l, 0, j)),
            pl.BlockSpec((None, 1, tn), lambda l, j: (l, 0, j)),
        ],
        out_specs=pl.BlockSpec((None, rows, tn), lambda l, j: (l, 0, j)),
        out_shape=jax.ShapeDtypeStruct((depth, rows, n), F32),
        compiler_params=_cparams(("arbitrary", "arbitrary")),
        name="ada_mod",
    )(c_pad, ada_w, ada_b.reshape(depth, 1, n))
    return out[:, :bsz].reshape(depth, bsz, 9, d)


def _ffn_kernel(x_ref, mod_ref, g_ref, win_ref, wout_ref, o_ref, h_scr, acc_scr, *, sub):
    n_half, hm, _ = h_scr.shape
    nchunk = D_FF // FF_CHUNK
    gate = 0.5 * mod_ref[3 * sub + 2:3 * sub + 3, :]
    for hh in range(n_half):
        rows = slice(hh * hm, (hh + 1) * hm)
        h_scr[hh] = _adaln(x_ref[rows, :], g_ref[...], mod_ref, sub).astype(BF16)

        def up(c):
            h = h_scr[hh]
            lo = c * FF_CHUNK
            return (_dot(h, win_ref[:, lo:lo + FF_CHUNK]),
                    _dot(h, win_ref[:, D_FF + lo:D_FF + lo + FF_CHUNK]))

        nxt = up(0)
        for c in range(nchunk):
            gg, uu = nxt
            if c + 1 < nchunk:
                nxt = up(c + 1)
            a = (gg * jax.nn.sigmoid(gg) * uu).astype(BF16)
            down = _dot(a, wout_ref[c * FF_CHUNK:(c + 1) * FF_CHUNK, :])
            if c == 0:
                acc_scr[hh] = down
            else:
                acc_scr[hh] += down
        o_ref[rows, :] = x_ref[rows, :] + gate * acc_scr[hh]


def _ffn(x, mod, g, w_in_all, w_out_all, layer, slot, sub):
    bsz, seq, d = x.shape
    tm = FFN_TOKEN_TILE
    hm = tm // 2
    return pl.pallas_call(
        functools.partial(_ffn_kernel, sub=sub),
        grid=(bsz, seq // tm),
        in_specs=[
            pl.BlockSpec((None, tm, d), lambda b, i: (b, i, 0)),
            pl.BlockSpec((None, 9, d), lambda b, i: (b, 0, 0)),
            pl.BlockSpec((1, d), lambda b, i: (0, 0)),
            _resident((None, None, d, 2 * D_FF), lambda b, i: (layer, slot, 0, 0)),
            _resident((None, None, D_FF, d), lambda b, i: (layer, slot, 0, 0)),
        ],
        out_specs=pl.BlockSpec((None, tm, d), lambda b, i: (b, i, 0)),
        out_shape=jax.ShapeDtypeStruct(x.shape, F32),
        scratch_shapes=[pltpu.VMEM((2, hm, d), BF16), pltpu.VMEM((2, hm, d), F32)],
        compiler_params=_cparams(("arbitrary", "arbitrary")),
        name="ffn",
    )(x, mod, g.reshape(1, d), w_in_all, w_out_all)


def _conv_kernel(x_ref, mod_ref, g_ref, win_ref, cw_ref, wout_ref, o_ref, ubuf):
    tm = x_ref.shape[0]

    @pl.when(pl.program_id(1) == 0)
    def _():
        ubuf[0:8, :] = jnp.zeros((8, ubuf.shape[1]), F32)

    x = x_ref[...]
    hb = _adaln(x, g_ref[...], mod_ref, 1).astype(BF16)
    d = x_ref.shape[1]
    bg = _dot(hb, win_ref[:, 0:d])
    u = _dot(hb, win_ref[:, d:2 * d]) * _dot(hb, win_ref[:, 2 * d:3 * d])
    ubuf[8:8 + tm, :] = u
    y = cw_ref[2:3, :] * u + cw_ref[1:2, :] * ubuf[7:7 + tm, :] + cw_ref[0:1, :] * ubuf[6:6 + tm, :]
    ubuf[0:8, :] = ubuf[tm:tm + 8, :]
    out = _dot((bg * y).astype(BF16), wout_ref[...])
    o_ref[...] = x + mod_ref[5:6, :] * out


def _conv_mixer(x, mod, g, w_in_all, conv_w, w_out_all, layer):
    bsz, seq, d = x.shape
    tm = TOKEN_TILE
    return pl.pallas_call(
        _conv_kernel,
        grid=(bsz, seq // tm),
        in_specs=[
            pl.BlockSpec((None, tm, d), lambda b, i: (b, i, 0)),
            pl.BlockSpec((None, 9, d), lambda b, i: (b, 0, 0)),
            pl.BlockSpec((1, d), lambda b, i: (0, 0)),
            _resident((None, d, 3 * d), lambda b, i: (layer, 0, 0)),
            pl.BlockSpec((3, d), lambda b, i: (0, 0)),
            _resident((None, d, d), lambda b, i: (layer, 0, 0)),
        ],
        out_specs=pl.BlockSpec((None, tm, d), lambda b, i: (b, i, 0)),
        out_shape=jax.ShapeDtypeStruct(x.shape, F32),
        scratch_shapes=[pltpu.VMEM((tm + 8, d), F32)],
        compiler_params=_cparams(("arbitrary", "arbitrary")),
        name="conv_mixer",
    )(x, mod, g.reshape(1, d), w_in_all, conv_w, w_out_all)


def _head_ms(sq, p_ref):
    hi, lo = _split_bf16(sq)
    return _dot(hi, p_ref[...]) + _dot(lo, p_ref[...])


def _proj_kernel(x_ref, mod_ref, g_ref, w_ref, wT_ref, p_ref, qg_ref, kg_ref,
                 qT_ref, kvc_ref, ks_ref, vsT_ref, kw_ref, vwT_ref, gT_ref):
    tm = x_ref.shape[0]
    x = x_ref[...]
    hb = _adaln(x, g_ref[...], mod_ref, 1).astype(BF16)

    def col(i, width=KV_DIM):
        return _dot(hb, w_ref[:, i:i + width])

    def colT(i, width=KV_DIM):
        return lax.dot_general(wT_ref[i:i + width, :], hb, (((1,), (1,)), ((), ())),
                               preferred_element_type=F32)

    qg = jnp.concatenate([qg_ref[...]] * (tm // LANES), axis=1)
    for c in range(GROUPS):
        qc = colT(c * KV_DIM)
        ms = jnp.mean((qc * qc).reshape(HEADS_PER_GROUP, HEAD_DIM, tm), axis=1, keepdims=True)
        ms = jnp.broadcast_to(ms, (HEADS_PER_GROUP, HEAD_DIM, tm)).reshape(KV_DIM, tm)
        qn = qc * lax.rsqrt(ms + RMS_EPS) * qg
        qT_ref[c * KV_DIM:(c + 1) * KV_DIM, :] = qn.astype(BF16)
    vsT = colT(HEADS * HEAD_DIM).astype(BF16)
    vwT = colT(HEADS * HEAD_DIM + KV_DIM).astype(BF16)
    rid = lax.broadcasted_iota(jnp.int32, (V_ROWS - HEAD_DIM, tm), 0)
    ones_rows = jnp.where(rid == 0, 1.0, 0.0).astype(BF16)
    for g in range(GROUPS):
        vsT_ref[g] = jnp.concatenate([vsT[g * HEAD_DIM:(g + 1) * HEAD_DIM, :], ones_rows], axis=0)
        vwT_ref[g] = jnp.concatenate([vwT[g * HEAD_DIM:(g + 1) * HEAD_DIM, :], ones_rows], axis=0)
    gT_ref[...] = jax.nn.sigmoid(colT(HEADS * HEAD_DIM + 2 * KV_DIM, LANES))

    for kv in range(2):
        kvc = col(kv * KV_DIM)
        for half in range(2):
            kvc_ref[kv, half] = kvc[:, half * LANES:(half + 1) * LANES]

    pos = pl.program_id(1) * tm + lax.broadcasted_iota(jnp.int32, (tm, HEAD_DIM), 0)
    lane = lax.broadcasted_iota(jnp.int32, (tm, HEAD_DIM), 1)
    onehot = jnp.where(lane == ((pos // SEL_BLOCK) % FAR_BLOCKS), 1.0, 0.0)
    zeros = jnp.zeros((tm, HEAD_DIM), F32)

    ks = col(2 * KV_DIM)
    ksn = ks * lax.rsqrt(_head_ms(ks * ks, p_ref) + RMS_EPS) * kg_ref[0:1, :]
    kw = col(3 * KV_DIM)
    kwn = kw * lax.rsqrt(_head_ms(kw * kw, p_ref) + RMS_EPS) * kg_ref[1:2, :]
    for g in range(GROUPS):
        sl = slice(g * HEAD_DIM, (g + 1) * HEAD_DIM)
        ks_ref[g] = jnp.concatenate([ksn[:, sl], onehot], axis=1).astype(BF16)
        kw_ref[g] = jnp.concatenate([kwn[:, sl], zeros], axis=1).astype(BF16)


def _nsa_proj(x, mod, g, w_in, q_gain, k_gain):
    bsz, seq, d = x.shape
    tm = TOKEN_TILE
    nq = HEADS * HEAD_DIM
    q, kc, vc, ks, vs, kw, vw, gl = jnp.split(w_in, [nq + i * KV_DIM for i in range(7)], axis=1)
    gl = gl.reshape(d, GROUPS, 3 * HEADS_PER_GROUP)
    gl = jnp.pad(gl, ((0, 0), (0, GROUPS), (0, 4))).reshape(d, LANES)
    w = jnp.concatenate([kc, vc, ks, kw], axis=1).astype(BF16)
    wT = jnp.concatenate([q, vs, vw, gl], axis=1).T.astype(BF16)
    hid = np.arange(KV_DIM) // HEAD_DIM
    pmat = jnp.asarray((hid[:, None] == hid[None, :]) / HEAD_DIM, BF16)
    qg = jnp.tile(q_gain, HEADS_PER_GROUP) * (HEAD_DIM ** -0.5 * LOG2E)
    qg = jnp.broadcast_to(qg[:, None], (KV_DIM, LANES))
    kg = jnp.stack([jnp.tile(k_gain[1], GROUPS), jnp.tile(k_gain[2], GROUPS)])
    kv_spec = pl.BlockSpec((None, GROUPS, tm, LANES), lambda b, i: (b, 0, i, 0))
    vT_spec = pl.BlockSpec((None, GROUPS, V_ROWS, tm), lambda b, i: (b, 0, 0, i))
    kv_shape = jax.ShapeDtypeStruct((bsz, GROUPS, seq, LANES), BF16)
    vT_shape = jax.ShapeDtypeStruct((bsz, GROUPS, V_ROWS, seq), BF16)
    return pl.pallas_call(
        _proj_kernel,
        grid=(bsz, seq // tm),
        in_specs=[
            pl.BlockSpec((None, tm, d), lambda b, i: (b, i, 0)),
            pl.BlockSpec((None, 9, d), lambda b, i: (b, 0, 0)),
            pl.BlockSpec((1, d), lambda b, i: (0, 0)),
            _resident(w.shape, lambda b, i: (0, 0)),
            _resident(wT.shape, lambda b, i: (0, 0)),
            pl.BlockSpec((KV_DIM, KV_DIM), lambda b, i: (0, 0)),
            pl.BlockSpec((KV_DIM, LANES), lambda b, i: (0, 0)),
            pl.BlockSpec((2, KV_DIM), lambda b, i: (0, 0)),
        ],
        out_specs=[
            pl.BlockSpec((None, nq, tm), lambda b, i: (b, 0, i)),
            pl.BlockSpec((2, 2, None, tm, LANES), lambda b, i: (0, 0, b, i, 0)),
            kv_spec, vT_spec, kv_spec, vT_spec,
            pl.BlockSpec((None, LANES, tm), lambda b, i: (b, 0, i)),
        ],
        out_shape=[
            jax.ShapeDtypeStruct((bsz, nq, seq), BF16),
            jax.ShapeDtypeStruct((2, 2, bsz, seq, LANES), F32),
            kv_shape, vT_shape, kv_shape, vT_shape,
            jax.ShapeDtypeStruct((bsz, LANES, seq), F32),
        ],
        compiler_params=_cparams(("arbitrary", "arbitrary")),
        name="nsa_proj",
    )(x, mod, g.reshape(1, d), w, wT, pmat, qg, kg)


def _gelu_tanh(x):
    return 0.5 * x * (1.0 + jnp.tanh(math.sqrt(2.0 / math.pi) * (x + 0.044715 * (x * x * x))))


def _compress_kernel(kv_ref, pos_ref, w1_ref, w2_ref, kg_ref, o_ref, acc_lo, acc_hi, *, is_key):
    nc = acc_lo.shape[1]
    half = CMP_BLOCK // 2
    acc_lo[...] = jnp.zeros_like(acc_lo)
    acc_hi[...] = jnp.zeros_like(acc_hi)
    for l in range(half):
        xl = [kv_ref[j, pl.ds(l, nc, stride=CMP_STRIDE), :] for j in range(2)]
        for g in range(GROUPS):
            xg = xl[g // 2][:, (g % 2) * HEAD_DIM:(g % 2 + 1) * HEAD_DIM]
            lo = (xg + pos_ref[l:l + 1, :]).astype(BF16)
            hi = (xg + pos_ref[half + l:half + l + 1, :]).astype(BF16)
            acc_lo[g] += _dot(lo, w1_ref[l * HEAD_DIM:(l + 1) * HEAD_DIM, :])
            acc_hi[g] += _dot(hi, w1_ref[(half + l) * HEAD_DIM:(half + l + 1) * HEAD_DIM, :])
    row = lax.broadcasted_iota(jnp.int32, (nc, CMP_HIDDEN), 0)
    for g in range(GROUPS):
        nxt = jnp.where(row < nc - 1, pltpu.roll(acc_hi[g], nc - 1, 0), 0.0)
        hdn = _gelu_tanh(acc_lo[g] + nxt).astype(BF16)
        if is_key:
            out = _dot(hdn, w2_ref[...])
            ms = jnp.sum(out * out, axis=-1, keepdims=True) * (1.0 / HEAD_DIM)
            o_ref[g] = (out * lax.rsqrt(ms + RMS_EPS) * kg_ref[...]).astype(BF16)
        else:
            outT = lax.dot_general(w2_ref[...], hdn, (((1,), (1,)), ((), ())),
                                   preferred_element_type=F32)
            rid = lax.broadcasted_iota(jnp.int32, (V_ROWS - HEAD_DIM, nc), 0)
            ones_rows = jnp.where(rid == 0, 1.0, 0.0).astype(BF16)
            o_ref[g] = jnp.concatenate([outT.astype(BF16), ones_rows], axis=0)


def _compress(kvc, pos, w1, w2, k_gain, is_key):
    _, _, bsz, seq, _ = kvc.shape
    which = 0 if is_key else 1
    nc = seq // CMP_STRIDE
    kg = jnp.pad(k_gain, (0, LANES - HEAD_DIM)).reshape(1, LANES)
    if is_key:
        w2p = jnp.pad(w2, ((0, 0), (0, LANES - HEAD_DIM))).astype(BF16)
    else:
        w2p = w2.T.astype(BF16)
    if is_key:
        out_spec = pl.BlockSpec((None, GROUPS, nc, LANES), lambda b: (b, 0, 0, 0))
        out_shape = jax.ShapeDtypeStruct((bsz, GROUPS, nc, LANES), BF16)
    else:
        out_spec = pl.BlockSpec((None, GROUPS, V_ROWS, nc), lambda b: (b, 0, 0, 0))
        out_shape = jax.ShapeDtypeStruct((bsz, GROUPS, V_ROWS, nc), BF16)
    return pl.pallas_call(
        functools.partial(_compress_kernel, is_key=is_key),
        grid=(bsz,),
        in_specs=[
            pl.BlockSpec((None, 2, None, seq, LANES), lambda b: (which, 0, b, 0, 0)),
            pl.BlockSpec((CMP_BLOCK, HEAD_DIM), lambda b: (0, 0)),
            pl.BlockSpec((CMP_BLOCK * HEAD_DIM, CMP_HIDDEN), lambda b: (0, 0)),
            pl.BlockSpec(w2p.shape, lambda b: (0, 0)),
            pl.BlockSpec((1, LANES), lambda b: (0, 0)),
        ],
        out_specs=out_spec,
        out_shape=out_shape,
        scratch_shapes=[pltpu.VMEM((GROUPS, nc, CMP_HIDDEN), F32),
                        pltpu.VMEM((GROUPS, nc, CMP_HIDDEN), F32)],
        compiler_params=_cparams(("arbitrary",)),
        name="compress_k" if is_key else "compress_v",
    )(kvc, pos, w1.astype(BF16), w2p, kg)


def _t5_bucket_np(dist):
    n = np.maximum(dist, 0)
    max_exact = REL_BUCKETS // 2
    nf = np.maximum(n, 1).astype(np.float32)
    large = max_exact + (np.log(nf / max_exact) / math.log(REL_MAX_DIST / max_exact)
                         * (REL_BUCKETS - max_exact)).astype(np.int32)
    large = np.minimum(large, REL_BUCKETS - 1)
    return np.where(n < max_exact, n, large)


def _bias_tables(rel_bias):
    rb = rel_bias.astype(F32).T
    rel = ((rb - rb[:, REL_BUCKETS - 1:]) * LOG2E).reshape(GROUPS, HEADS_PER_GROUP, REL_BUCKETS)
    ql = np.arange(Q_BLOCK)[None, :]
    kl = np.arange(Q_BLOCK)[:, None]
    d_lo, d_hi = -2 * Q_BLOCK, 3 * Q_BLOCK
    dvec = np.arange(d_lo, d_hi)
    onehot = jnp.asarray(np.eye(REL_BUCKETS, dtype=np.float32)[_t5_bucket_np(dvec)])
    vec = jnp.einsum("ghb,db->ghd", rel, onehot, precision=lax.Precision.HIGHEST)
    vec = jnp.where(jnp.asarray(dvec >= 0), vec, NEG)

    def table(dist):
        rows = [vec[:, :, int(d0) - d_lo:int(d0) - d_lo + Q_BLOCK] for d0 in dist[:, 0]]
        return jnp.stack(rows, axis=1).reshape(GROUPS, dist.shape[0], QL)

    near = jnp.stack([table(ql - kl + Q_BLOCK), table(ql - kl)], axis=1)
    win0 = np.where(ql - kl + WINDOW < WINDOW, 0.0, NEG).astype(np.float32)
    win0 = jnp.asarray(np.tile(win0, (1, HEADS_PER_GROUP)))
    r = np.arange(16)[:, None]
    cmp = table(ql - CMP_STRIDE * (r - 8) - (CMP_BLOCK - 1))
    cmp_first = jnp.concatenate([cmp[:, 8:], jnp.zeros_like(cmp[:, 8:])], axis=1)
    return near, win0, jnp.stack([cmp, cmp_first], axis=1)


def _attn_kernel(qT_ref, kc_ref, vcT_ref, ks_ref, vsT_ref, kw_ref, vwT_ref, gate_ref,
                 near_ref, win0_ref, cmpt_ref, ov_ref, o_ref,
                 rhs_z, rhs_f, rhs_n, sb_scr, sc_scr, sw_scr, sn_scr, s_nxt, p_pend, a_pend,
                 acc_scr, m_scr, *selb_far):
    nc = kc_ref.shape[0]
    ns = ov_ref.shape[0]
    n_win = WINDOW // Q_BLOCK + 1
    n_sub = FAR_KEYS // FAR_SUB
    last_tile = ks_ref.shape[0] // FAR_KEYS - 1
    tiles = range(Q_TILES)
    qis = [Q_TILES * pl.program_id(2) + t for t in tiles]
    t0s = [pl.multiple_of(qi * Q_BLOCK, Q_BLOCK) for qi in qis]
    tps = [pl.multiple_of(jnp.maximum(t0 - Q_BLOCK, 0), Q_BLOCK) for t0 in t0s]
    qcols = [slice(t * Q_BLOCK, (t + 1) * Q_BLOCK) for t in tiles]

    def tile4(v):
        return jnp.concatenate([v] * HEADS_PER_GROUP, axis=1)

    for t in tiles:
        q4 = qT_ref[:, qcols[t]]
        qt = jnp.concatenate([q4[h * HEAD_DIM:(h + 1) * HEAD_DIM, :] for h in range(HEADS_PER_GROUP)], axis=1)
        for rhs in (rhs_z, rhs_f, rhs_n):
            rhs[t, 0:HEAD_DIM, :] = qt
            rhs[t, HEAD_DIM:, :] = jnp.zeros((LANES - HEAD_DIM, QL), BF16)

    for t in tiles:
        sc_scr[t] = _dot(kc_ref[...], rhs_z[t])
    starts = []
    for t in tiles:
        starts.append([])
        for a in range(n_win):
            start = t0s[t] - Q_BLOCK * (n_win - 1 - a)
            st = pl.multiple_of(jnp.maximum(start, 0), Q_BLOCK)
            starts[t].append(st)
            s = _dot(kw_ref[pl.ds(st, Q_BLOCK), :], rhs_z[t])
            if a == 0:
                s = s + win0_ref[...]
            elif a == n_win - 2:
                s = s + near_ref[0]
            elif a == n_win - 1:
                s = s + near_ref[1]
            if a < n_win - 1:
                s = jnp.where(start >= 0, s, NEG)
            sw_scr[t, a * Q_BLOCK:(a + 1) * Q_BLOCK, :] = s
        sn_scr[t, 0:Q_BLOCK, :] = _dot(ks_ref[pl.ds(tps[t], Q_BLOCK), :], rhs_z[t]) + near_ref[0]
        sn_scr[t, Q_BLOCK:, :] = _dot(ks_ref[pl.ds(t0s[t], Q_BLOCK), :], rhs_z[t]) + near_ref[1]
        s_nxt[t] = _dot(ks_ref[0:FAR_SUB, :], rhs_z[t])

    o_c, imp = [], []
    row = lax.broadcasted_iota(jnp.int32, (nc, QL), 0)
    for t in tiles:
        qi = qis[t]
        st = pl.multiple_of(jnp.maximum(8 * qi - 8, 0), 8)
        sc_scr[t, pl.ds(st, 16), :] += jnp.where(qi == 0, cmpt_ref[1], cmpt_ref[0])
        s = jnp.where(row < 8 * qi + 8, sc_scr[t], NEG)
        m = jnp.max(s, axis=0, keepdims=True)
        p = jnp.exp2(s - m)
        pv = _dot(vcT_ref[...], p.astype(BF16))
        inv = jnp.where(m > 0.1 * NEG, 1.0 / pv[HEAD_DIM:HEAD_DIM + 1, :], 0.0)
        o_c.append(pv[0:HEAD_DIM, :] * inv)
        pn = p * inv
        psum = (pn[:, 0:Q_BLOCK] + pn[:, Q_BLOCK:2 * Q_BLOCK]
                + pn[:, 2 * Q_BLOCK:3 * Q_BLOCK] + pn[:, 3 * Q_BLOCK:4 * Q_BLOCK])
        phi, plo = _split_bf16(psum)
        imp.append(_dot(ov_ref[...], phi) + _dot(ov_ref[...], plo))

    jrow = lax.broadcasted_iota(jnp.int32, (ns, Q_BLOCK), 0)
    qlane = lax.broadcasted_iota(jnp.int32, (ns, Q_BLOCK), 1)
    rowf = jrow.astype(F32)
    future, vals = [], []
    for t in tiles:
        cur = 2 * qis[t] + jnp.where(qlane >= SEL_BLOCK, 1, 0)
        fut = jrow > cur
        forced = (jrow == 0) | (jrow == cur) | (jrow == cur - 1)
        future.append(fut)
        vals.append(jnp.where(forced, -jnp.inf, jnp.where(fut, NEG, imp[t])))
    win_m = [None] * Q_TILES
    win_o = [None] * Q_TILES

    def win_max(t, a):
        ma = jnp.max(sw_scr[t, a * Q_BLOCK:(a + 1) * Q_BLOCK, :], axis=0, keepdims=True)
        win_m[t] = ma if win_m[t] is None else jnp.maximum(win_m[t], ma)

    def win_pv(t, a):
        pb = jnp.exp2(sw_scr[t, a * Q_BLOCK:(a + 1) * Q_BLOCK, :] - win_m[t]).astype(BF16)
        pv = _dot(vwT_ref[:, pl.ds(starts[t][a], Q_BLOCK)], pb)
        win_o[t] = pv if win_o[t] is None else win_o[t] + pv

    filler = [functools.partial(win_max, t, a) for a in range(n_win) for t in tiles]
    filler += [functools.partial(win_pv, t, a) for a in range(n_win) for t in tiles]
    per_round = -(-len(filler) // (SEL_TOP_N - 3))
    for r in range(SEL_TOP_N - 3):
        for t in tiles:
            v = vals[t]
            mx = jnp.max(v, axis=0, keepdims=True)
            idx = jnp.min(jnp.where(v == mx, rowf, float(ns)), axis=0, keepdims=True)
            vals[t] = jnp.where(rowf == idx, -jnp.inf, v)
        for f in filler[r * per_round:(r + 1) * per_round]:
            f()

    o_w = []
    for t in tiles:
        o_w.append(win_o[t][0:HEAD_DIM, :] / win_o[t][HEAD_DIM:HEAD_DIM + 1, :])
        sb = jnp.where((vals[t] == -jnp.inf) & jnp.logical_not(future[t]), 0.0, NEG)
        sb_scr[t] = sb
        sb_far = jnp.where(jrow < 2 * qis[t] - 2, sb, NEG)
        selb_far[t][...] = tile4(sb_far).astype(BF16)
        s_nxt[t] += jnp.concatenate(
            [jnp.broadcast_to(tile4(sb_far[b:b + 1, :]), (SEL_BLOCK, QL)) for b in range(FAR_SUB // SEL_BLOCK)],
            axis=0)
        m_scr[t] = jnp.full((1, QL), NEG, F32)
        acc_scr[t] = jnp.zeros((V_ROWS, QL), F32)
        p_pend[t] = jnp.zeros((FAR_SUB, QL), BF16)
        a_pend[t] = jnp.ones((1, QL), F32)

    def far_scores(t, jt, rhs_ref, u):
        k0 = pl.multiple_of(jt * FAR_KEYS + u * FAR_SUB, FAR_SUB)
        return _dot(ks_ref[pl.ds(k0, FAR_SUB), :], rhs_ref[t])

    def far_rhs(t, jt, rhs_ref):
        r0 = pl.multiple_of(jt * FAR_BLOCKS, FAR_BLOCKS)
        rhs_ref[t, HEAD_DIM:HEAD_DIM + FAR_BLOCKS, :] = selb_far[t][pl.ds(r0, FAR_BLOCKS), :]

    def far_pv(jt, u, p_bf16):
        k0 = pl.multiple_of(jt * FAR_KEYS + u * FAR_SUB, FAR_SUB)
        return _dot(vsT_ref[:, pl.ds(k0, FAR_SUB)], p_bf16)

    for t in tiles:
        far_rhs(t, 0, rhs_f)

    def far_body(t, jt, carry):
        jn = jnp.minimum(jt + 1, last_tile)
        jp = jnp.maximum(jt - 1, 0)
        far_rhs(t, jn, rhs_n)
        m, acc, s_cur = m_scr[t], acc_scr[t], s_nxt[t]
        pend = (jp, n_sub - 1, p_pend[t], a_pend[t])
        for u in range(n_sub):
            if u + 1 < n_sub:
                s_new = far_scores(t, jt, rhs_f, u + 1)
            else:
                s_nxt[t] = far_scores(t, jn, rhs_n, 0)
            acc = pend[3] * acc + far_pv(pend[0], pend[1], pend[2])
            m_new = jnp.maximum(m, jnp.max(s_cur, axis=0, keepdims=True))
            alpha = jnp.exp2(m - m_new)
            p = jnp.exp2(s_cur - m_new)
            m = m_new
            pend = (jt, u, p.astype(BF16), alpha)
            if u + 1 < n_sub:
                s_cur = s_new
        p_pend[t] = pend[2]
        a_pend[t] = pend[3]
        far_rhs(t, jn, rhs_f)
        m_scr[t] = m
        acc_scr[t] = acc
        return carry

    n_fars = [jnp.maximum(2 * qi - 2 + FAR_BLOCKS - 1, 0) // FAR_BLOCKS for qi in qis]
    for t in tiles:
        lax.fori_loop(0, n_fars[t], functools.partial(far_body, t), 0)

    krow = lax.broadcasted_iota(jnp.int32, (Q_BLOCK, QL), 0)
    for t in tiles:
        qi = qis[t]
        acc = a_pend[t] * acc_scr[t] + far_pv(jnp.maximum(n_fars[t] - 1, 0), n_sub - 1, p_pend[t])
        sel_a = tile4(sb_scr[t, pl.ds(jnp.maximum(2 * qi - 2, 0), 1), :])
        sel_b = tile4(sb_scr[t, pl.ds(jnp.maximum(2 * qi - 1, 0), 1), :])
        s_p = sn_scr[t, 0:Q_BLOCK, :] + jnp.where(krow < SEL_BLOCK, sel_a, sel_b)
        s_p = jnp.where(qi >= 1, s_p, NEG)
        s_d = sn_scr[t, Q_BLOCK:, :]
        m = m_scr[t]
        m_new = jnp.maximum(m, jnp.maximum(jnp.max(s_p, axis=0, keepdims=True),
                                           jnp.max(s_d, axis=0, keepdims=True)))
        acc = (jnp.exp2(m - m_new) * acc
               + _dot(vsT_ref[:, pl.ds(tps[t], Q_BLOCK)], jnp.exp2(s_p - m_new).astype(BF16))
               + _dot(vsT_ref[:, pl.ds(t0s[t], Q_BLOCK)], jnp.exp2(s_d - m_new).astype(BF16)))
        o_s = acc[0:HEAD_DIM, :] / acc[HEAD_DIM:HEAD_DIM + 1, :]

        for h in range(HEADS_PER_GROUP):
            sl = slice(h * Q_BLOCK, (h + 1) * Q_BLOCK)
            o_h = (gate_ref[3 * h:3 * h + 1, qcols[t]] * o_c[t][:, sl]
                   + gate_ref[3 * h + 1:3 * h + 2, qcols[t]] * o_s[:, sl]
                   + gate_ref[3 * h + 2:3 * h + 3, qcols[t]] * o_w[t][:, sl])
            o_ref[h * HEAD_DIM:(h + 1) * HEAD_DIM, qcols[t]] = o_h.astype(BF16)


def _nsa_attention(qT, kcmp, vcmpT, ks, vsT, kw, vwT, gT, tables):
    bsz, _, seq = qT.shape
    assert seq % FAR_KEYS == 0
    nq = seq // Q_BLOCK
    nc = seq // CMP_STRIDE
    ns = seq // SEL_BLOCK
    near, win0, cmpt = tables
    cs = np.arange(nc) * CMP_STRIDE
    ss = np.arange(ns) * SEL_BLOCK
    ov = (cs[None, :] <= ss[:, None] + SEL_BLOCK - 1) & (cs[None, :] + CMP_BLOCK - 1 >= ss[:, None])
    ov = jnp.asarray(ov, BF16)
    gates = gT.reshape(bsz, 2 * GROUPS, 16, seq)
    per_bg = lambda b, g, i: (b, g, 0, 0)
    qw = Q_TILES * Q_BLOCK
    assert nq % Q_TILES == 0
    return pl.pallas_call(
        _attn_kernel,
        grid=(bsz, GROUPS, nq // Q_TILES),
        in_specs=[
            pl.BlockSpec((None, KV_DIM, qw), lambda b, g, i: (b, g, i)),
            pl.BlockSpec((None, None, nc, LANES), per_bg),
            pl.BlockSpec((None, None, V_ROWS, nc), per_bg),
            pl.BlockSpec((None, None, seq, LANES), per_bg),
            pl.BlockSpec((None, None, V_ROWS, seq), per_bg),
            pl.BlockSpec((None, None, seq, LANES), per_bg),
            pl.BlockSpec((None, None, V_ROWS, seq), per_bg),
            pl.BlockSpec((None, None, 16, qw), lambda b, g, i: (b, g, 0, i)),
            pl.BlockSpec((None, 2, Q_BLOCK, QL), lambda b, g, i: (g, 0, 0, 0)),
            pl.BlockSpec((Q_BLOCK, QL), lambda b, g, i: (0, 0)),
            pl.BlockSpec((None, 2, 16, QL), lambda b, g, i: (g, 0, 0, 0)),
            pl.BlockSpec((ns, nc), lambda b, g, i: (0, 0)),
        ],
        out_specs=pl.BlockSpec((None, KV_DIM, qw), lambda b, g, i: (b, g, i)),
        out_shape=jax.ShapeDtypeStruct((bsz, HEADS * HEAD_DIM, seq), BF16),
        scratch_shapes=[
            pltpu.VMEM((Q_TILES, LANES, QL), BF16),
            pltpu.VMEM((Q_TILES, LANES, QL), BF16),
            pltpu.VMEM((Q_TILES, LANES, QL), BF16),
            pltpu.VMEM((Q_TILES, ns, Q_BLOCK), F32),
            pltpu.VMEM((Q_TILES, nc, QL), F32),
            pltpu.VMEM((Q_TILES, WINDOW + Q_BLOCK, QL), F32),
            pltpu.VMEM((Q_TILES, 2 * Q_BLOCK, QL), F32),
            pltpu.VMEM((Q_TILES, FAR_SUB, QL), F32),
            pltpu.VMEM((Q_TILES, FAR_SUB, QL), BF16),
            pltpu.VMEM((Q_TILES, 1, QL), F32),
            pltpu.VMEM((Q_TILES, V_ROWS, QL), F32),
            pltpu.VMEM((Q_TILES, 1, QL), F32),
        ] + [pltpu.VMEM((ns, QL), BF16)] * Q_TILES,
        compiler_params=_cparams(("arbitrary", "arbitrary", "arbitrary")),
        name="nsa_attention",
    )(qT, kcmp, vcmpT, ks, vsT, kw, vwT, gates, near, win0, cmpt, ov)


def _oproj_kernel(x_ref, mod_ref, oT_ref, w_ref, o_ref):
    y = _dot(oT_ref[...].T, w_ref[...])
    o_ref[...] = x_ref[...] + mod_ref[5:6, :] * y


def _nsa_out(x, mod, oT, w_out):
    bsz, seq, d = x.shape
    tm = TOKEN_TILE
    return pl.pallas_call(
        _oproj_kernel,
        grid=(bsz, seq // tm),
        in_specs=[
            pl.BlockSpec((None, tm, d), lambda b, i: (b, i, 0)),
            pl.BlockSpec((None, 9, d), lambda b, i: (b, 0, 0)),
            pl.BlockSpec((None, d, tm), lambda b, i: (b, 0, i)),
            _resident((d, d), lambda b, i: (0, 0)),
        ],
        out_specs=pl.BlockSpec((None, tm, d), lambda b, i: (b, i, 0)),
        out_shape=jax.ShapeDtypeStruct(x.shape, F32),
        compiler_params=_cparams(("arbitrary", "arbitrary")),
        name="nsa_out",
    )(x, mod, oT, w_out.astype(BF16))


def _nsa_mixer(x, mod, g, w_in, cmp_pos, cmp_w1, cmp_w2, q_gain, k_gain, w_out, tables):
    qT, kvc, ks, vsT, kw, vwT, gT = _nsa_proj(x, mod, g, w_in, q_gain, k_gain)
    kcmp = _compress(kvc, cmp_pos, cmp_w1[0], cmp_w2[0], k_gain[0], True)
    vcmpT = _compress(kvc, cmp_pos, cmp_w1[1], cmp_w2[1], k_gain[0], False)
    oT = _nsa_attention(qT, kcmp, vcmpT, ks, vsT, kw, vwT, gT, tables)
    return _nsa_out(x, mod, oT, w_out)


def kernel(x, c, norm_g, ada_w, ada_b, ffn_w_in, ffn_w_out, conv_w_in, conv_w, conv_w_out,
           nsa_w_in, nsa_cmp_pos, nsa_cmp_w1, nsa_cmp_w2, nsa_q_gain, nsa_k_gain, nsa_w_out,
           rel_bias):
    depth = ada_w.shape[0]
    mods = _ada_mod(c, ada_w, ada_b)
    tables = _bias_tables(rel_bias)
    ffn_in = ffn_w_in.astype(BF16)
    ffn_out = ffn_w_out.astype(BF16)
    conv_in = conv_w_in.astype(BF16)
    conv_out = conv_w_out.astype(BF16)
    for i in range(depth):
        mod = mods[i]
        x = _ffn(x, mod, norm_g[i, 0], ffn_in, ffn_out, i, 0, 0)
        j = i // 2
        if i % 2 == 0:
            x = _conv_mixer(x, mod, norm_g[i, 1], conv_in, conv_w[j], conv_out, j)
        else:
            x = _nsa_mixer(x, mod, norm_g[i, 1], nsa_w_in[j], nsa_cmp_pos[j], nsa_cmp_w1[j],
                           nsa_cmp_w2[j], nsa_q_gain[j], nsa_k_gain[j], nsa_w_out[j], tables)
        x = _ffn(x, mod, norm_g[i, 2], ffn_in, ffn_out, i, 1, 2)
    return x
```

```python
import functools
import math

import numpy as np
import jax
import jax.numpy as jnp
from jax import lax
from jax.experimental import pallas as pl
from jax.experimental.pallas import tpu as pltpu

F32 = jnp.float32
BF16 = jnp.bfloat16

D_MODEL = 1024
D_FF = 2816
HEADS = 16
HEAD_DIM = 64
GROUPS = 4
HEADS_PER_GROUP = 4
KV_DIM = GROUPS * HEAD_DIM
CMP_BLOCK = 32
CMP_STRIDE = 16
CMP_HIDDEN = 256
SEL_BLOCK = 64
SEL_TOP_N = 16
WINDOW = 512
Q_BLOCK = 128
REL_BUCKETS = 32
REL_MAX_DIST = 128
RMS_EPS = 1e-6
NEG = -1e30
FORCE = 1e9

LANES = 128
QL = HEADS_PER_GROUP * Q_BLOCK
FAR_KEYS = 1024
FAR_BLOCKS = FAR_KEYS // SEL_BLOCK
FAR_SUB = 256
Q_TILES = 2
V_ROWS = HEAD_DIM + 16
LOG2E = math.log2(math.e)
FF_CHUNK = 256
TOKEN_TILE = 512
FFN_TOKEN_TILE = 1024
VMEM_LIMIT = 56 * 1024 * 1024


def _dot(a, b):
    return jnp.dot(a, b, preferred_element_type=F32)


def _cparams(sem):
    return pltpu.CompilerParams(dimension_semantics=sem, vmem_limit_bytes=VMEM_LIMIT)


def _resident(shape, index_map):
    return pl.BlockSpec(shape, index_map, pipeline_mode=pl.Buffered(1))


def _adaln(x, g, mod_ref, sub):
    ms = jnp.mean(x * x, axis=-1, keepdims=True)
    shift = mod_ref[3 * sub:3 * sub + 1, :]
    scale = mod_ref[3 * sub + 1:3 * sub + 2, :]
    return (x * lax.rsqrt(ms + RMS_EPS) * g) * (1.0 + scale) + shift


def _split_bf16(v):
    hi = v.astype(BF16)
    lo = (v - hi.astype(F32)).astype(BF16)
    return hi, lo


def _ada_kernel(c_ref, w_ref, b_ref, o_ref):
    c = c_ref[...]
    cond = c * jax.nn.sigmoid(c)
    chi, clo = _split_bf16(cond)
    whi, wlo = _split_bf16(w_ref[...])
    o_ref[...] = _dot(chi, whi) + _dot(chi, wlo) + _dot(clo, whi) + b_ref[...]


def _ada_mod(c, ada_w, ada_b):
    depth, d, n = ada_w.shape
    bsz = c.shape[0]
    rows = 8
    tn = 1024
    c_pad = jnp.zeros((rows, d), F32).at[:bsz].set(c)
    out = pl.pallas_call(
        _ada_kernel,
        grid=(depth, n // tn),
        in_specs=[
            pl.BlockSpec((rows, d), lambda l, j: (0, 0)),
            pl.BlockSpec((None, d, tn), lambda l, j: (l, 0, j)),
            pl.BlockSpec((None, 1, tn), lambda l, j: (l, 0, j)),
        ],
        out_specs=pl.BlockSpec((None, rows, tn), lambda l, j: (l, 0, j)),
        out_shape=jax.ShapeDtypeStruct((depth, rows, n), F32),
        compiler_params=_cparams(("arbitrary", "arbitrary")),
        name="ada_mod",
    )(c_pad, ada_w, ada_b.reshape(depth, 1, n))
    return out[:, :bsz].reshape(depth, bsz, 9, d)


def _ffn_kernel(x_ref, mod_ref, g_ref, win_ref, wout_ref, o_ref, h_scr, acc_scr, *, sub):
    n_half, hm, _ = h_scr.shape
    nchunk = D_FF // FF_CHUNK
    gate = 0.5 * mod_ref[3 * sub + 2:3 * sub + 3, :]
    for hh in range(n_half):
        rows = slice(hh * hm, (hh + 1) * hm)
        h_scr[hh] = _adaln(x_ref[rows, :], g_ref[...], mod_ref, sub).astype(BF16)

        def up(c):
            h = h_scr[hh]
            lo = c * FF_CHUNK
            return (_dot(h, win_ref[:, lo:lo + FF_CHUNK]),
                    _dot(h, win_ref[:, D_FF + lo:D_FF + lo + FF_CHUNK]))

        nxt = up(0)
        for c in range(nchunk):
            gg, uu = nxt
            if c + 1 < nchunk:
                nxt = up(c + 1)
            a = (gg * jax.nn.sigmoid(gg) * uu).astype(BF16)
            down = _dot(a, wout_ref[c * FF_CHUNK:(c + 1) * FF_CHUNK, :])
            if c == 0:
                acc_scr[hh] = down
            else:
                acc_scr[hh] += down
        o_ref[rows, :] = x_ref[rows, :] + gate * acc_scr[hh]


def _ffn(x, mod, g, w_in_all, w_out_all, layer, slot, sub):
    bsz, seq, d = x.shape
    tm = FFN_TOKEN_TILE
    hm = tm // 2
    return pl.pallas_call(
        functools.partial(_ffn_kernel, sub=sub),
        grid=(bsz, seq // tm),
        in_specs=[
            pl.BlockSpec((None, tm, d), lambda b, i: (b, i, 0)),
            pl.BlockSpec((None, 9, d), lambda b, i: (b, 0, 0)),
            pl.BlockSpec((1, d), lambda b, i: (0, 0)),
            _resident((None, None, d, 2 * D_FF), lambda b, i: (layer, slot, 0, 0)),
            _resident((None, None, D_FF, d), lambda b, i: (layer, slot, 0, 0)),
        ],
        out_specs=pl.BlockSpec((None, tm, d), lambda b, i: (b, i, 0)),
        out_shape=jax.ShapeDtypeStruct(x.shape, F32),
        scratch_shapes=[pltpu.VMEM((2, hm, d), BF16), pltpu.VMEM((2, hm, d), F32)],
        compiler_params=_cparams(("arbitrary", "arbitrary")),
        name="ffn",
    )(x, mod, g.reshape(1, d), w_in_all, w_out_all)


def _conv_kernel(x_ref, mod_ref, g_ref, win_ref, cw_ref, wout_ref, o_ref, ubuf):
    tm, d = x_ref.shape
    hm = tm // 2

    @pl.when(pl.program_id(1) == 0)
    def _():
        ubuf[0:8, :] = jnp.zeros((8, d), F32)

    for hh in range(2):
        r0 = hh * hm
        x = x_ref[r0:r0 + hm, :]
        hb = _adaln(x, g_ref[...], mod_ref, 1).astype(BF16)
        bg = _dot(hb, win_ref[:, 0:d])
        u = _dot(hb, win_ref[:, d:2 * d]) * _dot(hb, win_ref[:, 2 * d:3 * d])
        ubuf[8 + r0:8 + r0 + hm, :] = u
        y = (cw_ref[2:3, :] * u + cw_ref[1:2, :] * ubuf[7 + r0:7 + r0 + hm, :]
             + cw_ref[0:1, :] * ubuf[6 + r0:6 + r0 + hm, :])
        out = _dot((bg * y).astype(BF16), wout_ref[...])
        o_ref[r0:r0 + hm, :] = x + mod_ref[5:6, :] * out
    ubuf[0:8, :] = ubuf[tm:tm + 8, :]


def _conv_mixer(x, mod, g, w_in_all, conv_w, w_out_all, layer):
    bsz, seq, d = x.shape
    tm = 2 * TOKEN_TILE
    return pl.pallas_call(
        _conv_kernel,
        grid=(bsz, seq // tm),
        in_specs=[
            pl.BlockSpec((None, tm, d), lambda b, i: (b, i, 0)),
            pl.BlockSpec((None, 9, d), lambda b, i: (b, 0, 0)),
            pl.BlockSpec((1, d), lambda b, i: (0, 0)),
            _resident((None, d, 3 * d), lambda b, i: (layer, 0, 0)),
            pl.BlockSpec((3, d), lambda b, i: (0, 0)),
            _resident((None, d, d), lambda b, i: (layer, 0, 0)),
        ],
        out_specs=pl.BlockSpec((None, tm, d), lambda b, i: (b, i, 0)),
        out_shape=jax.ShapeDtypeStruct(x.shape, F32),
        scratch_shapes=[pltpu.VMEM((tm + 8, d), F32)],
        compiler_params=_cparams(("arbitrary", "arbitrary")),
        name="conv_mixer",
    )(x, mod, g.reshape(1, d), w_in_all, conv_w, w_out_all)


def _head_ms(sq, p_ref):
    hi, lo = _split_bf16(sq)
    return _dot(hi, p_ref[...]) + _dot(lo, p_ref[...])


def _proj_kernel(x_ref, mod_ref, g_ref, w_ref, wT_ref, p_ref, qg_ref, kg_ref,
                 qT_ref, kvc_ref, ks_ref, vsT_ref, kw_ref, vwT_ref, gT_ref):
    tm = x_ref.shape[0]
    for hh in range(2):
        _proj_half(hh * (tm // 2), tm // 2, x_ref, mod_ref, g_ref, w_ref, wT_ref, p_ref, qg_ref, kg_ref,
                   qT_ref, kvc_ref, ks_ref, vsT_ref, kw_ref, vwT_ref, gT_ref)


def _proj_half(r0, hm, x_ref, mod_ref, g_ref, w_ref, wT_ref, p_ref, qg_ref, kg_ref,
               qT_ref, kvc_ref, ks_ref, vsT_ref, kw_ref, vwT_ref, gT_ref):
    tok = slice(r0, r0 + hm)
    hb = _adaln(x_ref[tok, :], g_ref[...], mod_ref, 1).astype(BF16)

    def col(i, width=KV_DIM):
        return _dot(hb, w_ref[:, i:i + width])

    def colT(i, width=KV_DIM):
        return lax.dot_general(wT_ref[i:i + width, :], hb, (((1,), (1,)), ((), ())),
                               preferred_element_type=F32)

    qg = jnp.concatenate([qg_ref[...]] * (hm // LANES), axis=1)
    for c in range(GROUPS):
        qc = colT(c * KV_DIM)
        ms = jnp.mean((qc * qc).reshape(HEADS_PER_GROUP, HEAD_DIM, hm), axis=1, keepdims=True)
        ms = jnp.broadcast_to(ms, (HEADS_PER_GROUP, HEAD_DIM, hm)).reshape(KV_DIM, hm)
        qn = qc * lax.rsqrt(ms + RMS_EPS) * qg
        qT_ref[c * KV_DIM:(c + 1) * KV_DIM, tok] = qn.astype(BF16)
    vsT = colT(HEADS * HEAD_DIM).astype(BF16)
    vwT = colT(HEADS * HEAD_DIM + KV_DIM).astype(BF16)
    rid = lax.broadcasted_iota(jnp.int32, (V_ROWS - HEAD_DIM, hm), 0)
    ones_rows = jnp.where(rid == 0, 1.0, 0.0).astype(BF16)
    for g in range(GROUPS):
        vsT_ref[g, :, tok] = jnp.concatenate([vsT[g * HEAD_DIM:(g + 1) * HEAD_DIM, :], ones_rows], axis=0)
        vwT_ref[g, :, tok] = jnp.concatenate([vwT[g * HEAD_DIM:(g + 1) * HEAD_DIM, :], ones_rows], axis=0)
    gT_ref[:, tok] = jax.nn.sigmoid(colT(HEADS * HEAD_DIM + 2 * KV_DIM, LANES))

    for kv in range(2):
        kvc = col(kv * KV_DIM)
        for half in range(2):
            kvc_ref[kv, half, tok, :] = kvc[:, half * LANES:(half + 1) * LANES]

    pos = pl.program_id(1) * x_ref.shape[0] + r0 + lax.broadcasted_iota(jnp.int32, (hm, HEAD_DIM), 0)
    lane = lax.broadcasted_iota(jnp.int32, (hm, HEAD_DIM), 1)
    onehot = jnp.where(lane == ((pos // SEL_BLOCK) % FAR_BLOCKS), 1.0, 0.0)
    zeros = jnp.zeros((hm, HEAD_DIM), F32)

    ks = col(2 * KV_DIM)
    ksn = ks * lax.rsqrt(_head_ms(ks * ks, p_ref) + RMS_EPS) * kg_ref[0:1, :]
    kw = col(3 * KV_DIM)
    kwn = kw * lax.rsqrt(_head_ms(kw * kw, p_ref) + RMS_EPS) * kg_ref[1:2, :]
    for g in range(GROUPS):
        sl = slice(g * HEAD_DIM, (g + 1) * HEAD_DIM)
        ks_ref[g, tok, :] = jnp.concatenate([ksn[:, sl], onehot], axis=1).astype(BF16)
        kw_ref[g, tok, :] = jnp.concatenate([kwn[:, sl], zeros], axis=1).astype(BF16)


def _nsa_proj(x, mod, g, w_in, q_gain, k_gain):
    bsz, seq, d = x.shape
    tm = 2 * TOKEN_TILE
    nq = HEADS * HEAD_DIM
    q, kc, vc, ks, vs, kw, vw, gl = jnp.split(w_in, [nq + i * KV_DIM for i in range(7)], axis=1)
    gl = gl.reshape(d, GROUPS, 3 * HEADS_PER_GROUP)
    gl = jnp.pad(gl, ((0, 0), (0, GROUPS), (0, 4))).reshape(d, LANES)
    w = jnp.concatenate([kc, vc, ks, kw], axis=1).astype(BF16)
    wT = jnp.concatenate([q, vs, vw, gl], axis=1).T.astype(BF16)
    hid = np.arange(KV_DIM) // HEAD_DIM
    pmat = jnp.asarray((hid[:, None] == hid[None, :]) / HEAD_DIM, BF16)
    qg = jnp.tile(q_gain, HEADS_PER_GROUP) * (HEAD_DIM ** -0.5 * LOG2E)
    qg = jnp.broadcast_to(qg[:, None], (KV_DIM, LANES))
    kg = jnp.stack([jnp.tile(k_gain[1], GROUPS), jnp.tile(k_gain[2], GROUPS)])
    kv_spec = pl.BlockSpec((None, GROUPS, tm, LANES), lambda b, i: (b, 0, i, 0))
    vT_spec = pl.BlockSpec((None, GROUPS, V_ROWS, tm), lambda b, i: (b, 0, 0, i))
    kv_shape = jax.ShapeDtypeStruct((bsz, GROUPS, seq, LANES), BF16)
    vT_shape = jax.ShapeDtypeStruct((bsz, GROUPS, V_ROWS, seq), BF16)
    return pl.pallas_call(
        _proj_kernel,
        grid=(bsz, seq // tm),
        in_specs=[
            pl.BlockSpec((None, tm, d), lambda b, i: (b, i, 0)),
            pl.BlockSpec((None, 9, d), lambda b, i: (b, 0, 0)),
            pl.BlockSpec((1, d), lambda b, i: (0, 0)),
            _resident(w.shape, lambda b, i: (0, 0)),
            _resident(wT.shape, lambda b, i: (0, 0)),
            pl.BlockSpec((KV_DIM, KV_DIM), lambda b, i: (0, 0)),
            pl.BlockSpec((KV_DIM, LANES), lambda b, i: (0, 0)),
            pl.BlockSpec((2, KV_DIM), lambda b, i: (0, 0)),
        ],
        out_specs=[
            pl.BlockSpec((None, nq, tm), lambda b, i: (b, 0, i)),
            pl.BlockSpec((2, 2, None, tm, LANES), lambda b, i: (0, 0, b, i, 0)),
            kv_spec, vT_spec, kv_spec, vT_spec,
            pl.BlockSpec((None, LANES, tm), lambda b, i: (b, 0, i)),
        ],
        out_shape=[
            jax.ShapeDtypeStruct((bsz, nq, seq), BF16),
            jax.ShapeDtypeStruct((2, 2, bsz, seq, LANES), F32),
            kv_shape, vT_shape, kv_shape, vT_shape,
            jax.ShapeDtypeStruct((bsz, LANES, seq), F32),
        ],
        compiler_params=_cparams(("arbitrary", "arbitrary")),
        name="nsa_proj",
    )(x, mod, g.reshape(1, d), w, wT, pmat, qg, kg)


def _gelu_tanh(x):
    return 0.5 * x * (1.0 + jnp.tanh(math.sqrt(2.0 / math.pi) * (x + 0.044715 * (x * x * x))))


def _compress_kernel(kv_ref, pos_ref, w1_ref, w2_ref, kg_ref, o_ref, acc_lo, acc_hi, *, is_key):
    nc = acc_lo.shape[1]
    half = CMP_BLOCK // 2
    acc_lo[...] = jnp.zeros_like(acc_lo)
    acc_hi[...] = jnp.zeros_like(acc_hi)
    for l in range(half):
        xl = [kv_ref[j, pl.ds(l, nc, stride=CMP_STRIDE), :] for j in range(2)]
        for g in range(GROUPS):
            xg = xl[g // 2][:, (g % 2) * HEAD_DIM:(g % 2 + 1) * HEAD_DIM]
            lo = (xg + pos_ref[l:l + 1, :]).astype(BF16)
            hi = (xg + pos_ref[half + l:half + l + 1, :]).astype(BF16)
            acc_lo[g] += _dot(lo, w1_ref[l * HEAD_DIM:(l + 1) * HEAD_DIM, :])
            acc_hi[g] += _dot(hi, w1_ref[(half + l) * HEAD_DIM:(half + l + 1) * HEAD_DIM, :])
    row = lax.broadcasted_iota(jnp.int32, (nc, CMP_HIDDEN), 0)
    for g in range(GROUPS):
        nxt = jnp.where(row < nc - 1, pltpu.roll(acc_hi[g], nc - 1, 0), 0.0)
        hdn = _gelu_tanh(acc_lo[g] + nxt).astype(BF16)
        if is_key:
            out = _dot(hdn, w2_ref[...])
            ms = jnp.sum(out * out, axis=-1, keepdims=True) * (1.0 / HEAD_DIM)
            o_ref[g] = (out * lax.rsqrt(ms + RMS_EPS) * kg_ref[...]).astype(BF16)
        else:
            outT = lax.dot_general(w2_ref[...], hdn, (((1,), (1,)), ((), ())),
                                   preferred_element_type=F32)
            rid = lax.broadcasted_iota(jnp.int32, (V_ROWS - HEAD_DIM, nc), 0)
            ones_rows = jnp.where(rid == 0, 1.0, 0.0).astype(BF16)
            o_ref[g] = jnp.concatenate([outT.astype(BF16), ones_rows], axis=0)


def _compress(kvc, pos, w1, w2, k_gain, is_key):
    _, _, bsz, seq, _ = kvc.shape
    which = 0 if is_key else 1
    nc = seq // CMP_STRIDE
    kg = jnp.pad(k_gain, (0, LANES - HEAD_DIM)).reshape(1, LANES)
    if is_key:
        w2p = jnp.pad(w2, ((0, 0), (0, LANES - HEAD_DIM))).astype(BF16)
    else:
        w2p = w2.T.astype(BF16)
    if is_key:
        out_spec = pl.BlockSpec((None, GROUPS, nc, LANES), lambda b: (b, 0, 0, 0))
        out_shape = jax.ShapeDtypeStruct((bsz, GROUPS, nc, LANES), BF16)
    else:
        out_spec = pl.BlockSpec((None, GROUPS, V_ROWS, nc), lambda b: (b, 0, 0, 0))
        out_shape = jax.ShapeDtypeStruct((bsz, GROUPS, V_ROWS, nc), BF16)
    return pl.pallas_call(
        functools.partial(_compress_kernel, is_key=is_key),
        grid=(bsz,),
        in_specs=[
            pl.BlockSpec((None, 2, None, seq, LANES), lambda b: (which, 0, b, 0, 0)),
            pl.BlockSpec((CMP_BLOCK, HEAD_DIM), lambda b: (0, 0)),
            pl.BlockSpec((CMP_BLOCK * HEAD_DIM, CMP_HIDDEN), lambda b: (0, 0)),
            pl.BlockSpec(w2p.shape, lambda b: (0, 0)),
            pl.BlockSpec((1, LANES), lambda b: (0, 0)),
        ],
        out_specs=out_spec,
        out_shape=out_shape,
        scratch_shapes=[pltpu.VMEM((GROUPS, nc, CMP_HIDDEN), F32),
                        pltpu.VMEM((GROUPS, nc, CMP_HIDDEN), F32)],
        compiler_params=_cparams(("arbitrary",)),
        name="compress_k" if is_key else "compress_v",
    )(kvc, pos, w1.astype(BF16), w2p, kg)


def _t5_bucket_np(dist):
    n = np.maximum(dist, 0)
    max_exact = REL_BUCKETS // 2
    nf = np.maximum(n, 1).astype(np.float32)
    large = max_exact + (np.log(nf / max_exact) / math.log(REL_MAX_DIST / max_exact)
                         * (REL_BUCKETS - max_exact)).astype(np.int32)
    large = np.minimum(large, REL_BUCKETS - 1)
    return np.where(n < max_exact, n, large)


def _bias_tables(rel_bias):
    rb = rel_bias.astype(F32).T
    rel = ((rb - rb[:, REL_BUCKETS - 1:]) * LOG2E).reshape(GROUPS, HEADS_PER_GROUP, REL_BUCKETS)
    ql = np.arange(Q_BLOCK)[None, :]
    kl = np.arange(Q_BLOCK)[:, None]
    d_lo, d_hi = -2 * Q_BLOCK, 3 * Q_BLOCK
    dvec = np.arange(d_lo, d_hi)
    onehot = jnp.asarray(np.eye(REL_BUCKETS, dtype=np.float32)[_t5_bucket_np(dvec)])
    vec = jnp.einsum("ghb,db->ghd", rel, onehot, precision=lax.Precision.HIGHEST)
    vec = jnp.where(jnp.asarray(dvec >= 0), vec, NEG)

    def table(dist):
        rows = [vec[:, :, int(d0) - d_lo:int(d0) - d_lo + Q_BLOCK] for d0 in dist[:, 0]]
        return jnp.stack(rows, axis=1).reshape(GROUPS, dist.shape[0], QL)

    near = jnp.stack([table(ql - kl + Q_BLOCK), table(ql - kl)], axis=1)
    win0 = np.where(ql - kl + WINDOW < WINDOW, 0.0, NEG).astype(np.float32)
    win0 = jnp.asarray(np.tile(win0, (1, HEADS_PER_GROUP)))
    r = np.arange(16)[:, None]
    cmp = table(ql - CMP_STRIDE * (r - 8) - (CMP_BLOCK - 1))
    cmp_first = jnp.concatenate([cmp[:, 8:], jnp.zeros_like(cmp[:, 8:])], axis=1)
    return near, win0, jnp.stack([cmp, cmp_first], axis=1)


def _attn_kernel(qT_ref, kc_ref, vcT_ref, ks_ref, vsT_ref, kw_ref, vwT_ref, gate_ref,
                 near_ref, win0_ref, cmpt_ref, ov_ref, o_ref,
                 rhs_z, rhs_f, rhs_n, sb_scr, sc_scr, sw_scr, sn_scr, s_nxt, p_pend, a_pend,
                 acc_scr, m_scr, *selb_far):
    nc = kc_ref.shape[0]
    ns = ov_ref.shape[0]
    n_win = WINDOW // Q_BLOCK + 1
    n_sub = FAR_KEYS // FAR_SUB
    last_tile = ks_ref.shape[0] // FAR_KEYS - 1
    tiles = range(Q_TILES)
    qis = [Q_TILES * pl.program_id(2) + t for t in tiles]
    t0s = [pl.multiple_of(qi * Q_BLOCK, Q_BLOCK) for qi in qis]
    tps = [pl.multiple_of(jnp.maximum(t0 - Q_BLOCK, 0), Q_BLOCK) for t0 in t0s]
    qcols = [slice(t * Q_BLOCK, (t + 1) * Q_BLOCK) for t in tiles]

    def tile4(v):
        return jnp.concatenate([v] * HEADS_PER_GROUP, axis=1)

    for t in tiles:
        q4 = qT_ref[:, qcols[t]]
        qt = jnp.concatenate([q4[h * HEAD_DIM:(h + 1) * HEAD_DIM, :] for h in range(HEADS_PER_GROUP)], axis=1)
        for rhs in (rhs_z, rhs_f, rhs_n):
            rhs[t, 0:HEAD_DIM, :] = qt
            rhs[t, HEAD_DIM:, :] = jnp.zeros((LANES - HEAD_DIM, QL), BF16)

    for t in tiles:
        sc_scr[t] = _dot(kc_ref[...], rhs_z[t])
    starts = []
    for t in tiles:
        starts.append([])
        for a in range(n_win):
            start = t0s[t] - Q_BLOCK * (n_win - 1 - a)
            st = pl.multiple_of(jnp.maximum(start, 0), Q_BLOCK)
            starts[t].append(st)
            s = _dot(kw_ref[pl.ds(st, Q_BLOCK), :], rhs_z[t])
            if a == 0:
                s = s + win0_ref[...]
            elif a == n_win - 2:
                s = s + near_ref[0]
            elif a == n_win - 1:
                s = s + near_ref[1]
            if a < n_win - 1:
                s = jnp.where(start >= 0, s, NEG)
            sw_scr[t, a * Q_BLOCK:(a + 1) * Q_BLOCK, :] = s
        sn_scr[t, 0:Q_BLOCK, :] = _dot(ks_ref[pl.ds(tps[t], Q_BLOCK), :], rhs_z[t]) + near_ref[0]
        sn_scr[t, Q_BLOCK:, :] = _dot(ks_ref[pl.ds(t0s[t], Q_BLOCK), :], rhs_z[t]) + near_ref[1]
        s_nxt[t] = _dot(ks_ref[0:FAR_SUB, :], rhs_z[t])

    o_c, imp = [], []
    row = lax.broadcasted_iota(jnp.int32, (nc, QL), 0)
    for t in tiles:
        qi = qis[t]
        st = pl.multiple_of(jnp.maximum(8 * qi - 8, 0), 8)
        sc_scr[t, pl.ds(st, 16), :] += jnp.where(qi == 0, cmpt_ref[1], cmpt_ref[0])
        s = jnp.where(row < 8 * qi + 8, sc_scr[t], NEG)
        m = jnp.max(s, axis=0, keepdims=True)
        p = jnp.exp2(s - m)
        pv = _dot(vcT_ref[...], p.astype(BF16))
        inv = jnp.where(m > 0.1 * NEG, 1.0 / pv[HEAD_DIM:HEAD_DIM + 1, :], 0.0)
        o_c.append(pv[0:HEAD_DIM, :] * inv)
        pn = p * inv
        psum = (pn[:, 0:Q_BLOCK] + pn[:, Q_BLOCK:2 * Q_BLOCK]
                + pn[:, 2 * Q_BLOCK:3 * Q_BLOCK] + pn[:, 3 * Q_BLOCK:4 * Q_BLOCK])
        phi, plo = _split_bf16(psum)
        imp.append(_dot(ov_ref[...], phi) + _dot(ov_ref[...], plo))

    jrow = lax.broadcasted_iota(jnp.int32, (ns, Q_BLOCK), 0)
    qlane = lax.broadcasted_iota(jnp.int32, (ns, Q_BLOCK), 1)
    rowf = jrow.astype(F32)
    future, vals = [], []
    for t in tiles:
        cur = 2 * qis[t] + jnp.where(qlane >= SEL_BLOCK, 1, 0)
        fut = jrow > cur
        forced = (jrow == 0) | (jrow == cur) | (jrow == cur - 1)
        future.append(fut)
        vals.append(jnp.where(forced, -jnp.inf, jnp.where(fut, NEG, imp[t])))
    win_m = [None] * Q_TILES
    win_o = [None] * Q_TILES

    def win_max(t, a):
        ma = jnp.max(sw_scr[t, a * Q_BLOCK:(a + 1) * Q_BLOCK, :], axis=0, keepdims=True)
        win_m[t] = ma if win_m[t] is None else jnp.maximum(win_m[t], ma)

    def win_pv(t, a):
        pb = jnp.exp2(sw_scr[t, a * Q_BLOCK:(a + 1) * Q_BLOCK, :] - win_m[t]).astype(BF16)
        pv = _dot(vwT_ref[:, pl.ds(starts[t][a], Q_BLOCK)], pb)
        win_o[t] = pv if win_o[t] is None else win_o[t] + pv

    filler = [functools.partial(win_max, t, a) for a in range(n_win) for t in tiles]
    filler += [functools.partial(win_pv, t, a) for a in range(n_win) for t in tiles]
    per_round = -(-len(filler) // (SEL_TOP_N - 3))
    for r in range(SEL_TOP_N - 3):
        for t in tiles:
            v = vals[t]
            mx = jnp.max(v, axis=0, keepdims=True)
            idx = jnp.min(jnp.where(v == mx, rowf, float(ns)), axis=0, keepdims=True)
            vals[t] = jnp.where(rowf == idx, -jnp.inf, v)
        for f in filler[r * per_round:(r + 1) * per_round]:
            f()

    o_w = []
    for t in tiles:
        o_w.append(win_o[t][0:HEAD_DIM, :] / win_o[t][HEAD_DIM:HEAD_DIM + 1, :])
        sb = jnp.where((vals[t] == -jnp.inf) & jnp.logical_not(future[t]), 0.0, NEG)
        sb_scr[t] = sb
        sb_far = jnp.where(jrow < 2 * qis[t] - 2, sb, NEG)
        selb_far[t][...] = tile4(sb_far).astype(BF16)
        s_nxt[t] += jnp.concatenate(
            [jnp.broadcast_to(tile4(sb_far[b:b + 1, :]), (SEL_BLOCK, QL)) for b in range(FAR_SUB // SEL_BLOCK)],
            axis=0)
        m_scr[t] = jnp.full((1, QL), NEG, F32)
        acc_scr[t] = jnp.zeros((V_ROWS, QL), F32)
        p_pend[t] = jnp.zeros((FAR_SUB, QL), BF16)
        a_pend[t] = jnp.ones((1, QL), F32)

    def far_scores(t, jt, rhs_ref, u):
        k0 = pl.multiple_of(jt * FAR_KEYS + u * FAR_SUB, FAR_SUB)
        return _dot(ks_ref[pl.ds(k0, FAR_SUB), :], rhs_ref[t])

    def far_rhs(t, jt, rhs_ref):
        r0 = pl.multiple_of(jt * FAR_BLOCKS, FAR_BLOCKS)
        rhs_ref[t, HEAD_DIM:HEAD_DIM + FAR_BLOCKS, :] = selb_far[t][pl.ds(r0, FAR_BLOCKS), :]

    def far_pv(jt, u, p_bf16):
        k0 = pl.multiple_of(jt * FAR_KEYS + u * FAR_SUB, FAR_SUB)
        return _dot(vsT_ref[:, pl.ds(k0, FAR_SUB)], p_bf16)

    for t in tiles:
        far_rhs(t, 0, rhs_f)

    def far_body(t, jt, carry):
        jn = jnp.minimum(jt + 1, last_tile)
        jp = jnp.maximum(jt - 1, 0)
        far_rhs(t, jn, rhs_n)
        m, acc, s_cur = m_scr[t], acc_scr[t], s_nxt[t]
        pend = (jp, n_sub - 1, p_pend[t], a_pend[t])
        for u in range(n_sub):
            if u + 1 < n_sub:
                s_new = far_scores(t, jt, rhs_f, u + 1)
            else:
                s_nxt[t] = far_scores(t, jn, rhs_n, 0)
            acc = pend[3] * acc + far_pv(pend[0], pend[1], pend[2])
            m_new = jnp.maximum(m, jnp.max(s_cur, axis=0, keepdims=True))
            alpha = jnp.exp2(m - m_new)
            p = jnp.exp2(s_cur - m_new)
            m = m_new
            pend = (jt, u, p.astype(BF16), alpha)
            if u + 1 < n_sub:
                s_cur = s_new
        p_pend[t] = pend[2]
        a_pend[t] = pend[3]
        far_rhs(t, jn, rhs_f)
        m_scr[t] = m
        acc_scr[t] = acc
        return carry

    n_fars = [jnp.maximum(2 * qi - 2 + FAR_BLOCKS - 1, 0) // FAR_BLOCKS for qi in qis]
    for t in tiles:
        lax.fori_loop(0, n_fars[t], functools.partial(far_body, t), 0)

    krow = lax.broadcasted_iota(jnp.int32, (Q_BLOCK, QL), 0)
    for t in tiles:
        qi = qis[t]
        acc = a_pend[t] * acc_scr[t] + far_pv(jnp.maximum(n_fars[t] - 1, 0), n_sub - 1, p_pend[t])
        sel_a = tile4(sb_scr[t, pl.ds(jnp.maximum(2 * qi - 2, 0), 1), :])
        sel_b = tile4(sb_scr[t, pl.ds(jnp.maximum(2 * qi - 1, 0), 1), :])
        s_p = sn_scr[t, 0:Q_BLOCK, :] + jnp.where(krow < SEL_BLOCK, sel_a, sel_b)
        s_p = jnp.where(qi >= 1, s_p, NEG)
        s_d = sn_scr[t, Q_BLOCK:, :]
        m = m_scr[t]
        m_new = jnp.maximum(m, jnp.maximum(jnp.max(s_p, axis=0, keepdims=True),
                                           jnp.max(s_d, axis=0, keepdims=True)))
        acc = (jnp.exp2(m - m_new) * acc
               + _dot(vsT_ref[:, pl.ds(tps[t], Q_BLOCK)], jnp.exp2(s_p - m_new).astype(BF16))
               + _dot(vsT_ref[:, pl.ds(t0s[t], Q_BLOCK)], jnp.exp2(s_d - m_new).astype(BF16)))
        o_s = acc[0:HEAD_DIM, :] / acc[HEAD_DIM:HEAD_DIM + 1, :]

        heads = []
        for h in range(HEADS_PER_GROUP):
            sl = slice(h * Q_BLOCK, (h + 1) * Q_BLOCK)
            heads.append(gate_ref[3 * h:3 * h + 1, qcols[t]] * o_c[t][:, sl]
                         + gate_ref[3 * h + 1:3 * h + 2, qcols[t]] * o_s[:, sl]
                         + gate_ref[3 * h + 2:3 * h + 3, qcols[t]] * o_w[t][:, sl])
        o_ref[qcols[t], :] = jnp.concatenate(heads, axis=0).T.astype(BF16)


def _nsa_attention(qT, kcmp, vcmpT, ks, vsT, kw, vwT, gT, tables):
    bsz, _, seq = qT.shape
    assert seq % FAR_KEYS == 0
    nq = seq // Q_BLOCK
    nc = seq // CMP_STRIDE
    ns = seq // SEL_BLOCK
    near, win0, cmpt = tables
    cs = np.arange(nc) * CMP_STRIDE
    ss = np.arange(ns) * SEL_BLOCK
    ov = (cs[None, :] <= ss[:, None] + SEL_BLOCK - 1) & (cs[None, :] + CMP_BLOCK - 1 >= ss[:, None])
    ov = jnp.asarray(ov, BF16)
    gates = gT.reshape(bsz, 2 * GROUPS, 16, seq)
    per_bg = lambda b, g, i: (b, g, 0, 0)
    qw = Q_TILES * Q_BLOCK
    assert nq % Q_TILES == 0
    return pl.pallas_call(
        _attn_kernel,
        grid=(bsz, GROUPS, nq // Q_TILES),
        in_specs=[
            pl.BlockSpec((None, KV_DIM, qw), lambda b, g, i: (b, g, i)),
            pl.BlockSpec((None, None, nc, LANES), per_bg),
            pl.BlockSpec((None, None, V_ROWS, nc), per_bg),
            pl.BlockSpec((None, None, seq, LANES), per_bg),
            pl.BlockSpec((None, None, V_ROWS, seq), per_bg),
            pl.BlockSpec((None, None, seq, LANES), per_bg),
            pl.BlockSpec((None, None, V_ROWS, seq), per_bg),
            pl.BlockSpec((None, None, 16, qw), lambda b, g, i: (b, g, 0, i)),
            pl.BlockSpec((None, 2, Q_BLOCK, QL), lambda b, g, i: (g, 0, 0, 0)),
            pl.BlockSpec((Q_BLOCK, QL), lambda b, g, i: (0, 0)),
            pl.BlockSpec((None, 2, 16, QL), lambda b, g, i: (g, 0, 0, 0)),
            pl.BlockSpec((ns, nc), lambda b, g, i: (0, 0)),
        ],
        out_specs=pl.BlockSpec((None, qw, KV_DIM), lambda b, g, i: (b, i, g)),
        out_shape=jax.ShapeDtypeStruct((bsz, seq, HEADS * HEAD_DIM), BF16),
        scratch_shapes=[
            pltpu.VMEM((Q_TILES, LANES, QL), BF16),
            pltpu.VMEM((Q_TILES, LANES, QL), BF16),
            pltpu.VMEM((Q_TILES, LANES, QL), BF16),
            pltpu.VMEM((Q_TILES, ns, Q_BLOCK), F32),
            pltpu.VMEM((Q_TILES, nc, QL), F32),
            pltpu.VMEM((Q_TILES, WINDOW + Q_BLOCK, QL), F32),
            pltpu.VMEM((Q_TILES, 2 * Q_BLOCK, QL), F32),
            pltpu.VMEM((Q_TILES, FAR_SUB, QL), F32),
            pltpu.VMEM((Q_TILES, FAR_SUB, QL), BF16),
            pltpu.VMEM((Q_TILES, 1, QL), F32),
            pltpu.VMEM((Q_TILES, V_ROWS, QL), F32),
            pltpu.VMEM((Q_TILES, 1, QL), F32),
        ] + [pltpu.VMEM((ns, QL), BF16)] * Q_TILES,
        compiler_params=_cparams(("arbitrary", "arbitrary", "arbitrary")),
        name="nsa_attention",
    )(qT, kcmp, vcmpT, ks, vsT, kw, vwT, gates, near, win0, cmpt, ov)


def _oproj_kernel(x_ref, mod_ref, a_ref, w_ref, o_ref):
    o_ref[...] = x_ref[...] + mod_ref[5:6, :] * _dot(a_ref[...], w_ref[...])


def _nsa_out(x, mod, attn, w_out):
    bsz, seq, d = x.shape
    tm = 2 * TOKEN_TILE
    return pl.pallas_call(
        _oproj_kernel,
        grid=(bsz, seq // tm),
        in_specs=[
            pl.BlockSpec((None, tm, d), lambda b, i: (b, i, 0)),
            pl.BlockSpec((None, 9, d), lambda b, i: (b, 0, 0)),
            pl.BlockSpec((None, tm, d), lambda b, i: (b, i, 0)),
            _resident((d, d), lambda b, i: (0, 0)),
        ],
        out_specs=pl.BlockSpec((None, tm, d), lambda b, i: (b, i, 0)),
        out_shape=jax.ShapeDtypeStruct(x.shape, F32),
        compiler_params=_cparams(("arbitrary", "arbitrary")),
        name="nsa_out",
    )(x, mod, attn, w_out.astype(BF16))


def _nsa_mixer(x, mod, g, w_in, cmp_pos, cmp_w1, cmp_w2, q_gain, k_gain, w_out, tables):
    qT, kvc, ks, vsT, kw, vwT, gT = _nsa_proj(x, mod, g, w_in, q_gain, k_gain)
    kcmp = _compress(kvc, cmp_pos, cmp_w1[0], cmp_w2[0], k_gain[0], True)
    vcmpT = _compress(kvc, cmp_pos, cmp_w1[1], cmp_w2[1], k_gain[0], False)
    attn = _nsa_attention(qT, kcmp, vcmpT, ks, vsT, kw, vwT, gT, tables)
    return _nsa_out(x, mod, attn, w_out)


def kernel(x, c, norm_g, ada_w, ada_b, ffn_w_in, ffn_w_out, conv_w_in, conv_w, conv_w_out,
           nsa_w_in, nsa_cmp_pos, nsa_cmp_w1, nsa_cmp_w2, nsa_q_gain, nsa_k_gain, nsa_w_out,
           rel_bias):
    depth = ada_w.shape[0]
    mods = _ada_mod(c, ada_w, ada_b)
    tables = _bias_tables(rel_bias)
    ffn_in = ffn_w_in.astype(BF16)
    ffn_out = ffn_w_out.astype(BF16)
    conv_in = conv_w_in.astype(BF16)
    conv_out = conv_w_out.astype(BF16)
    for i in range(depth):
        mod = mods[i]
        x = _ffn(x, mod, norm_g[i, 0], ffn_in, ffn_out, i, 0, 0)
        j = i // 2
        if i % 2 == 0:
            x = _conv_mixer(x, mod, norm_g[i, 1], conv_in, conv_w[j], conv_out, j)
        else:
            x = _nsa_mixer(x, mod, norm_g[i, 1], nsa_w_in[j], nsa_cmp_pos[j], nsa_cmp_w1[j],
                           nsa_cmp_w2[j], nsa_q_gain[j], nsa_k_gain[j], nsa_w_out[j], tables)
        x = _ffn(x, mod, norm_g[i, 2], ffn_in, ffn_out, i, 1, 2)
    return x
```

```python
import functools
import math

import numpy as np
import jax
import jax.numpy as jnp
from jax import lax
from jax.experimental import pallas as pl
from jax.experimental.pallas import tpu as pltpu

F32 = jnp.float32
BF16 = jnp.bfloat16

D_MODEL = 1024
D_FF = 2816
HEADS = 16
HEAD_DIM = 64
GROUPS = 4
HEADS_PER_GROUP = 4
KV_DIM = GROUPS * HEAD_DIM
CMP_BLOCK = 32
CMP_STRIDE = 16
CMP_HIDDEN = 256
SEL_BLOCK = 64
SEL_TOP_N = 16
WINDOW = 512
Q_BLOCK = 128
REL_BUCKETS = 32
REL_MAX_DIST = 128
RMS_EPS = 1e-6
NEG = -1e30
FORCE = 1e9

LANES = 128
QL = HEADS_PER_GROUP * Q_BLOCK
FAR_KEYS = 1024
FAR_BLOCKS = FAR_KEYS // SEL_BLOCK
FAR_SUB = 256
Q_TILES = 2
CMP_ROWS = 128
V_ROWS = HEAD_DIM + 16
LOG2E = math.log2(math.e)
FF_CHUNK = 256
TOKEN_TILE = 512
FFN_TOKEN_TILE = 1024
VMEM_LIMIT = 56 * 1024 * 1024


def _dot(a, b):
    return jnp.dot(a, b, preferred_element_type=F32)


def _cparams(sem):
    return pltpu.CompilerParams(dimension_semantics=sem, vmem_limit_bytes=VMEM_LIMIT)


def _resident(shape, index_map):
    return pl.BlockSpec(shape, index_map, pipeline_mode=pl.Buffered(1))


def _adaln(x, g, mod_ref, sub):
    ms = jnp.mean(x * x, axis=-1, keepdims=True)
    shift = mod_ref[3 * sub:3 * sub + 1, :]
    scale = mod_ref[3 * sub + 1:3 * sub + 2, :]
    return (x * lax.rsqrt(ms + RMS_EPS) * g) * (1.0 + scale) + shift


def _split_bf16(v):
    hi = v.astype(BF16)
    lo = (v - hi.astype(F32)).astype(BF16)
    return hi, lo


def _ada_kernel(c_ref, w_ref, b_ref, o_ref):
    c = c_ref[...]
    cond = c * jax.nn.sigmoid(c)
    chi, clo = _split_bf16(cond)
    whi, wlo = _split_bf16(w_ref[...])
    o_ref[...] = _dot(chi, whi) + _dot(chi, wlo) + _dot(clo, whi) + b_ref[...]


def _ada_mod(c, ada_w, ada_b):
    depth, d, n = ada_w.shape
    bsz = c.shape[0]
    rows = 8
    tn = 1024
    c_pad = jnp.zeros((rows, d), F32).at[:bsz].set(c)
    out = pl.pallas_call(
        _ada_kernel,
        grid=(depth, n // tn),
        in_specs=[
            pl.BlockSpec((rows, d), lambda l, j: (0, 0)),
            pl.BlockSpec((None, d, tn), lambda l, j: (l, 0, j)),
            pl.BlockSpec((None, 1, tn), lambda l, j: (l, 0, j)),
        ],
        out_specs=pl.BlockSpec((None, rows, tn), lambda l, j: (l, 0, j)),
        out_shape=jax.ShapeDtypeStruct((depth, rows, n), F32),
        compiler_params=_cparams(("arbitrary", "arbitrary")),
        name="ada_mod",
    )(c_pad, ada_w, ada_b.reshape(depth, 1, n))
    return out[:, :bsz].reshape(depth, bsz, 9, d)


def _ffn_kernel(x_ref, mod_ref, g_ref, win_ref, wout_ref, o_ref, h_scr, acc_scr, *, sub):
    n_half, hm, _ = h_scr.shape
    nchunk = D_FF // FF_CHUNK
    gate = 0.5 * mod_ref[3 * sub + 2:3 * sub + 3, :]
    for hh in range(n_half):
        rows = slice(hh * hm, (hh + 1) * hm)
        h_scr[hh] = _adaln(x_ref[rows, :], g_ref[...], mod_ref, sub).astype(BF16)

        def up(c):
            h = h_scr[hh]
            lo = c * FF_CHUNK
            return (_dot(h, win_ref[:, lo:lo + FF_CHUNK]),
                    _dot(h, win_ref[:, D_FF + lo:D_FF + lo + FF_CHUNK]))

        nxt = up(0)
        for c in range(nchunk):
            gg, uu = nxt
            if c + 1 < nchunk:
                nxt = up(c + 1)
            a = (gg * jax.nn.sigmoid(gg) * uu).astype(BF16)
            down = _dot(a, wout_ref[c * FF_CHUNK:(c + 1) * FF_CHUNK, :])
            if c == 0:
                acc_scr[hh] = down
            else:
                acc_scr[hh] += down
        o_ref[rows, :] = x_ref[rows, :] + gate * acc_scr[hh]


def _ffn(x, mod, g, w_in_all, w_out_all, layer, slot, sub):
    bsz, seq, d = x.shape
    tm = FFN_TOKEN_TILE
    hm = tm // 2
    return pl.pallas_call(
        functools.partial(_ffn_kernel, sub=sub),
        grid=(bsz, seq // tm),
        in_specs=[
            pl.BlockSpec((None, tm, d), lambda b, i: (b, i, 0)),
            pl.BlockSpec((None, 9, d), lambda b, i: (b, 0, 0)),
            pl.BlockSpec((1, d), lambda b, i: (0, 0)),
            _resident((None, None, d, 2 * D_FF), lambda b, i: (layer, slot, 0, 0)),
            _resident((None, None, D_FF, d), lambda b, i: (layer, slot, 0, 0)),
        ],
        out_specs=pl.BlockSpec((None, tm, d), lambda b, i: (b, i, 0)),
        out_shape=jax.ShapeDtypeStruct(x.shape, F32),
        scratch_shapes=[pltpu.VMEM((2, hm, d), BF16), pltpu.VMEM((2, hm, d), F32)],
        compiler_params=_cparams(("arbitrary", "arbitrary")),
        name="ffn",
    )(x, mod, g.reshape(1, d), w_in_all, w_out_all)


def _conv_kernel(x_ref, mod_ref, g_ref, win_ref, cw_ref, wout_ref, o_ref, ubuf):
    tm, d = x_ref.shape
    hm = tm // 2

    @pl.when(pl.program_id(1) == 0)
    def _():
        ubuf[0:8, :] = jnp.zeros((8, d), F32)

    for hh in range(2):
        r0 = hh * hm
        x = x_ref[r0:r0 + hm, :]
        hb = _adaln(x, g_ref[...], mod_ref, 1).astype(BF16)
        bg = _dot(hb, win_ref[:, 0:d])
        u = _dot(hb, win_ref[:, d:2 * d]) * _dot(hb, win_ref[:, 2 * d:3 * d])
        ubuf[8 + r0:8 + r0 + hm, :] = u
        y = (cw_ref[2:3, :] * u + cw_ref[1:2, :] * ubuf[7 + r0:7 + r0 + hm, :]
             + cw_ref[0:1, :] * ubuf[6 + r0:6 + r0 + hm, :])
        out = _dot((bg * y).astype(BF16), wout_ref[...])
        o_ref[r0:r0 + hm, :] = x + mod_ref[5:6, :] * out
    ubuf[0:8, :] = ubuf[tm:tm + 8, :]


def _conv_mixer(x, mod, g, w_in_all, conv_w, w_out_all, layer):
    bsz, seq, d = x.shape
    tm = 2 * TOKEN_TILE
    return pl.pallas_call(
        _conv_kernel,
        grid=(bsz, seq // tm),
        in_specs=[
            pl.BlockSpec((None, tm, d), lambda b, i: (b, i, 0)),
            pl.BlockSpec((None, 9, d), lambda b, i: (b, 0, 0)),
            pl.BlockSpec((1, d), lambda b, i: (0, 0)),
            _resident((None, d, 3 * d), lambda b, i: (layer, 0, 0)),
            pl.BlockSpec((3, d), lambda b, i: (0, 0)),
            _resident((None, d, d), lambda b, i: (layer, 0, 0)),
        ],
        out_specs=pl.BlockSpec((None, tm, d), lambda b, i: (b, i, 0)),
        out_shape=jax.ShapeDtypeStruct(x.shape, F32),
        scratch_shapes=[pltpu.VMEM((tm + 8, d), F32)],
        compiler_params=_cparams(("arbitrary", "arbitrary")),
        name="conv_mixer",
    )(x, mod, g.reshape(1, d), w_in_all, conv_w, w_out_all)


def _head_ms(sq, p_ref):
    hi, lo = _split_bf16(sq)
    return _dot(hi, p_ref[...]) + _dot(lo, p_ref[...])


def _proj_kernel(x_ref, mod_ref, g_ref, w_ref, wT_ref, p_ref, qg_ref, kg_ref,
                 qT_ref, kvc_ref, ks_ref, vsT_ref, kw_ref, vwT_ref, gT_ref):
    tm = x_ref.shape[0]
    for hh in range(2):
        _proj_half(hh * (tm // 2), tm // 2, x_ref, mod_ref, g_ref, w_ref, wT_ref, p_ref, qg_ref, kg_ref,
                   qT_ref, kvc_ref, ks_ref, vsT_ref, kw_ref, vwT_ref, gT_ref)


def _proj_half(r0, hm, x_ref, mod_ref, g_ref, w_ref, wT_ref, p_ref, qg_ref, kg_ref,
               qT_ref, kvc_ref, ks_ref, vsT_ref, kw_ref, vwT_ref, gT_ref):
    tok = slice(r0, r0 + hm)
    hb = _adaln(x_ref[tok, :], g_ref[...], mod_ref, 1).astype(BF16)

    def col(i, width=KV_DIM):
        return _dot(hb, w_ref[:, i:i + width])

    def colT(i, width=KV_DIM):
        return lax.dot_general(wT_ref[i:i + width, :], hb, (((1,), (1,)), ((), ())),
                               preferred_element_type=F32)

    qg = jnp.concatenate([qg_ref[...]] * (hm // LANES), axis=1)
    for c in range(GROUPS):
        qc = colT(c * KV_DIM)
        ms = jnp.mean((qc * qc).reshape(HEADS_PER_GROUP, HEAD_DIM, hm), axis=1, keepdims=True)
        ms = jnp.broadcast_to(ms, (HEADS_PER_GROUP, HEAD_DIM, hm)).reshape(KV_DIM, hm)
        qn = qc * lax.rsqrt(ms + RMS_EPS) * qg
        qT_ref[c * KV_DIM:(c + 1) * KV_DIM, tok] = qn.astype(BF16)
    vsT = colT(HEADS * HEAD_DIM).astype(BF16)
    vwT = colT(HEADS * HEAD_DIM + KV_DIM).astype(BF16)
    rid = lax.broadcasted_iota(jnp.int32, (V_ROWS - HEAD_DIM, hm), 0)
    ones_rows = jnp.where(rid == 0, 1.0, 0.0).astype(BF16)
    for g in range(GROUPS):
        vsT_ref[g, :, tok] = jnp.concatenate([vsT[g * HEAD_DIM:(g + 1) * HEAD_DIM, :], ones_rows], axis=0)
        vwT_ref[g, :, tok] = jnp.concatenate([vwT[g * HEAD_DIM:(g + 1) * HEAD_DIM, :], ones_rows], axis=0)
    gT_ref[:, tok] = jax.nn.sigmoid(colT(HEADS * HEAD_DIM + 2 * KV_DIM, LANES))

    for kv in range(2):
        kvc = col(kv * KV_DIM)
        for half in range(2):
            kvc_ref[kv, half, tok, :] = kvc[:, half * LANES:(half + 1) * LANES]

    pos = pl.program_id(1) * x_ref.shape[0] + r0 + lax.broadcasted_iota(jnp.int32, (hm, HEAD_DIM), 0)
    lane = lax.broadcasted_iota(jnp.int32, (hm, HEAD_DIM), 1)
    onehot = jnp.where(lane == ((pos // SEL_BLOCK) % FAR_BLOCKS), 1.0, 0.0)
    zeros = jnp.zeros((hm, HEAD_DIM), F32)

    ks = col(2 * KV_DIM)
    ksn = ks * lax.rsqrt(_head_ms(ks * ks, p_ref) + RMS_EPS) * kg_ref[0:1, :]
    kw = col(3 * KV_DIM)
    kwn = kw * lax.rsqrt(_head_ms(kw * kw, p_ref) + RMS_EPS) * kg_ref[1:2, :]
    for g in range(GROUPS):
        sl = slice(g * HEAD_DIM, (g + 1) * HEAD_DIM)
        ks_ref[g, tok, :] = jnp.concatenate([ksn[:, sl], onehot], axis=1).astype(BF16)
        kw_ref[g, tok, :] = jnp.concatenate([kwn[:, sl], zeros], axis=1).astype(BF16)


def _nsa_proj(x, mod, g, w_in, q_gain, k_gain):
    bsz, seq, d = x.shape
    tm = 2 * TOKEN_TILE
    nq = HEADS * HEAD_DIM
    q, kc, vc, ks, vs, kw, vw, gl = jnp.split(w_in, [nq + i * KV_DIM for i in range(7)], axis=1)
    gl = gl.reshape(d, GROUPS, 3 * HEADS_PER_GROUP)
    gl = jnp.pad(gl, ((0, 0), (0, GROUPS), (0, 4))).reshape(d, LANES)
    w = jnp.concatenate([kc, vc, ks, kw], axis=1).astype(BF16)
    wT = jnp.concatenate([q, vs, vw, gl], axis=1).T.astype(BF16)
    hid = np.arange(KV_DIM) // HEAD_DIM
    pmat = jnp.asarray((hid[:, None] == hid[None, :]) / HEAD_DIM, BF16)
    qg = jnp.tile(q_gain, HEADS_PER_GROUP) * (HEAD_DIM ** -0.5 * LOG2E)
    qg = jnp.broadcast_to(qg[:, None], (KV_DIM, LANES))
    kg = jnp.stack([jnp.tile(k_gain[1], GROUPS), jnp.tile(k_gain[2], GROUPS)])
    kv_spec = pl.BlockSpec((None, GROUPS, tm, LANES), lambda b, i: (b, 0, i, 0))
    vT_spec = pl.BlockSpec((None, GROUPS, V_ROWS, tm), lambda b, i: (b, 0, 0, i))
    kv_shape = jax.ShapeDtypeStruct((bsz, GROUPS, seq, LANES), BF16)
    vT_shape = jax.ShapeDtypeStruct((bsz, GROUPS, V_ROWS, seq), BF16)
    return pl.pallas_call(
        _proj_kernel,
        grid=(bsz, seq // tm),
        in_specs=[
            pl.BlockSpec((None, tm, d), lambda b, i: (b, i, 0)),
            pl.BlockSpec((None, 9, d), lambda b, i: (b, 0, 0)),
            pl.BlockSpec((1, d), lambda b, i: (0, 0)),
            _resident(w.shape, lambda b, i: (0, 0)),
            _resident(wT.shape, lambda b, i: (0, 0)),
            pl.BlockSpec((KV_DIM, KV_DIM), lambda b, i: (0, 0)),
            pl.BlockSpec((KV_DIM, LANES), lambda b, i: (0, 0)),
            pl.BlockSpec((2, KV_DIM), lambda b, i: (0, 0)),
        ],
        out_specs=[
            pl.BlockSpec((None, nq, tm), lambda b, i: (b, 0, i)),
            pl.BlockSpec((2, 2, None, tm, LANES), lambda b, i: (0, 0, b, i, 0)),
            kv_spec, vT_spec, kv_spec, vT_spec,
            pl.BlockSpec((None, LANES, tm), lambda b, i: (b, 0, i)),
        ],
        out_shape=[
            jax.ShapeDtypeStruct((bsz, nq, seq), BF16),
            jax.ShapeDtypeStruct((2, 2, bsz, seq, LANES), F32),
            kv_shape, vT_shape, kv_shape, vT_shape,
            jax.ShapeDtypeStruct((bsz, LANES, seq), F32),
        ],
        compiler_params=_cparams(("arbitrary", "arbitrary")),
        name="nsa_proj",
    )(x, mod, g.reshape(1, d), w, wT, pmat, qg, kg)


def _gelu_tanh(x):
    return 0.5 * x * (1.0 + jnp.tanh(math.sqrt(2.0 / math.pi) * (x + 0.044715 * (x * x * x))))


def _compress_kernel(kv_ref, pos_ref, w1_ref, w2_ref, kg_ref, o_ref, acc_lo, acc_hi, *, is_key):
    nc = acc_lo.shape[1]
    half = CMP_BLOCK // 2
    acc_lo[...] = jnp.zeros_like(acc_lo)
    acc_hi[...] = jnp.zeros_like(acc_hi)
    for l in range(half):
        xl = [kv_ref[j, pl.ds(l, nc, stride=CMP_STRIDE), :] for j in range(2)]
        for g in range(GROUPS):
            xg = xl[g // 2][:, (g % 2) * HEAD_DIM:(g % 2 + 1) * HEAD_DIM]
            lo = (xg + pos_ref[l:l + 1, :]).astype(BF16)
            hi = (xg + pos_ref[half + l:half + l + 1, :]).astype(BF16)
            acc_lo[g] += _dot(lo, w1_ref[l * HEAD_DIM:(l + 1) * HEAD_DIM, :])
            acc_hi[g] += _dot(hi, w1_ref[(half + l) * HEAD_DIM:(half + l + 1) * HEAD_DIM, :])
    row = lax.broadcasted_iota(jnp.int32, (nc, CMP_HIDDEN), 0)
    for g in range(GROUPS):
        nxt = jnp.where(row < nc - 1, pltpu.roll(acc_hi[g], nc - 1, 0), 0.0)
        hdn = _gelu_tanh(acc_lo[g] + nxt).astype(BF16)
        if is_key:
            out = _dot(hdn, w2_ref[...])
            ms = jnp.sum(out * out, axis=-1, keepdims=True) * (1.0 / HEAD_DIM)
            o_ref[g] = (out * lax.rsqrt(ms + RMS_EPS) * kg_ref[...]).astype(BF16)
        else:
            outT = lax.dot_general(w2_ref[...], hdn, (((1,), (1,)), ((), ())),
                                   preferred_element_type=F32)
            rid = lax.broadcasted_iota(jnp.int32, (V_ROWS - HEAD_DIM, nc), 0)
            ones_rows = jnp.where(rid == 0, 1.0, 0.0).astype(BF16)
            o_ref[g] = jnp.concatenate([outT.astype(BF16), ones_rows], axis=0)


def _compress(kvc, pos, w1, w2, k_gain, is_key):
    _, _, bsz, seq, _ = kvc.shape
    which = 0 if is_key else 1
    nc = seq // CMP_STRIDE
    kg = jnp.pad(k_gain, (0, LANES - HEAD_DIM)).reshape(1, LANES)
    if is_key:
        w2p = jnp.pad(w2, ((0, 0), (0, LANES - HEAD_DIM))).astype(BF16)
    else:
        w2p = w2.T.astype(BF16)
    if is_key:
        out_spec = pl.BlockSpec((None, GROUPS, nc, LANES), lambda b: (b, 0, 0, 0))
        out_shape = jax.ShapeDtypeStruct((bsz, GROUPS, nc, LANES), BF16)
    else:
        out_spec = pl.BlockSpec((None, GROUPS, V_ROWS, nc), lambda b: (b, 0, 0, 0))
        out_shape = jax.ShapeDtypeStruct((bsz, GROUPS, V_ROWS, nc), BF16)
    return pl.pallas_call(
        functools.partial(_compress_kernel, is_key=is_key),
        grid=(bsz,),
        in_specs=[
            pl.BlockSpec((None, 2, None, seq, LANES), lambda b: (which, 0, b, 0, 0)),
            pl.BlockSpec((CMP_BLOCK, HEAD_DIM), lambda b: (0, 0)),
            pl.BlockSpec((CMP_BLOCK * HEAD_DIM, CMP_HIDDEN), lambda b: (0, 0)),
            pl.BlockSpec(w2p.shape, lambda b: (0, 0)),
            pl.BlockSpec((1, LANES), lambda b: (0, 0)),
        ],
        out_specs=out_spec,
        out_shape=out_shape,
        scratch_shapes=[pltpu.VMEM((GROUPS, nc, CMP_HIDDEN), F32),
                        pltpu.VMEM((GROUPS, nc, CMP_HIDDEN), F32)],
        compiler_params=_cparams(("arbitrary",)),
        name="compress_k" if is_key else "compress_v",
    )(kvc, pos, w1.astype(BF16), w2p, kg)


def _t5_bucket_np(dist):
    n = np.maximum(dist, 0)
    max_exact = REL_BUCKETS // 2
    nf = np.maximum(n, 1).astype(np.float32)
    large = max_exact + (np.log(nf / max_exact) / math.log(REL_MAX_DIST / max_exact)
                         * (REL_BUCKETS - max_exact)).astype(np.int32)
    large = np.minimum(large, REL_BUCKETS - 1)
    return np.where(n < max_exact, n, large)


def _bias_tables(rel_bias):
    rb = rel_bias.astype(F32).T
    rel = ((rb - rb[:, REL_BUCKETS - 1:]) * LOG2E).reshape(GROUPS, HEADS_PER_GROUP, REL_BUCKETS)
    ql = np.arange(Q_BLOCK)[None, :]
    kl = np.arange(Q_BLOCK)[:, None]
    d_lo, d_hi = -2 * Q_BLOCK, 3 * Q_BLOCK
    dvec = np.arange(d_lo, d_hi)
    onehot = jnp.asarray(np.eye(REL_BUCKETS, dtype=np.float32)[_t5_bucket_np(dvec)])
    vec = jnp.einsum("ghb,db->ghd", rel, onehot, precision=lax.Precision.HIGHEST)
    vec = jnp.where(jnp.asarray(dvec >= 0), vec, NEG)

    def table(dist):
        rows = [vec[:, :, int(d0) - d_lo:int(d0) - d_lo + Q_BLOCK] for d0 in dist[:, 0]]
        return jnp.stack(rows, axis=1).reshape(GROUPS, dist.shape[0], QL)

    near = jnp.stack([table(ql - kl + Q_BLOCK), table(ql - kl)], axis=1)
    win0 = np.where(ql - kl + WINDOW < WINDOW, 0.0, NEG).astype(np.float32)
    win0 = jnp.asarray(np.tile(win0, (1, HEADS_PER_GROUP)))
    r = np.arange(16)[:, None]
    cmp = table(ql - CMP_STRIDE * (r - 8) - (CMP_BLOCK - 1))
    cmp_first = jnp.concatenate([cmp[:, 8:], jnp.zeros_like(cmp[:, 8:])], axis=1)
    return near, win0, jnp.stack([cmp, cmp_first], axis=1)


def _attn_kernel(qT_ref, kc_ref, vcT_ref, ks_ref, vsT_ref, kw_ref, vwT_ref, gate_ref,
                 near_ref, win0_ref, cmpt_ref, ov_ref, o_ref,
                 rhs_z, sb_scr, sc_scr, sw_scr, sn_scr, s_nxt, p_pend, a_pend,
                 acc_scr, m_scr, oc_scr, imp_scr, *selb_far):
    nc = kc_ref.shape[0]
    ns = ov_ref.shape[0]
    n_win = WINDOW // Q_BLOCK + 1
    n_sub = FAR_KEYS // FAR_SUB
    last_tile = ks_ref.shape[0] // FAR_KEYS - 1
    tiles = range(Q_TILES)
    qis = [Q_TILES * pl.program_id(2) + t for t in tiles]
    t0s = [pl.multiple_of(qi * Q_BLOCK, Q_BLOCK) for qi in qis]
    tps = [pl.multiple_of(jnp.maximum(t0 - Q_BLOCK, 0), Q_BLOCK) for t0 in t0s]
    qcols = [slice(t * Q_BLOCK, (t + 1) * Q_BLOCK) for t in tiles]

    def tile4(v):
        return jnp.concatenate([v] * HEADS_PER_GROUP, axis=1)

    for t in tiles:
        q4 = qT_ref[:, qcols[t]]
        qt = jnp.concatenate([q4[h * HEAD_DIM:(h + 1) * HEAD_DIM, :] for h in range(HEADS_PER_GROUP)], axis=1)
        rhs_z[t, 0:HEAD_DIM, :] = qt
        rhs_z[t, HEAD_DIM:, :] = jnp.zeros((LANES - HEAD_DIM, QL), BF16)

    starts = [[t0s[t] - Q_BLOCK * (n_win - 1 - a) for a in range(n_win)] for t in tiles]
    starts_c = [[pl.multiple_of(jnp.maximum(st, 0), Q_BLOCK) for st in starts[t]] for t in tiles]

    def other_scores():
        for t in tiles:
            for a in range(n_win):
                s = _dot(kw_ref[pl.ds(starts_c[t][a], Q_BLOCK), :], rhs_z[t])
                if a == 0:
                    s = s + win0_ref[...]
                elif a == n_win - 2:
                    s = s + near_ref[0]
                elif a == n_win - 1:
                    s = s + near_ref[1]
                if a < n_win - 1:
                    s = jnp.where(starts[t][a] >= 0, s, NEG)
                sw_scr[t, a * Q_BLOCK:(a + 1) * Q_BLOCK, :] = s
            sn_scr[t, 0:Q_BLOCK, :] = _dot(ks_ref[pl.ds(tps[t], Q_BLOCK), :], rhs_z[t]) + near_ref[0]
            sn_scr[t, Q_BLOCK:, :] = _dot(ks_ref[pl.ds(t0s[t], Q_BLOCK), :], rhs_z[t]) + near_ref[1]
            s_nxt[t] = _dot(ks_ref[0:FAR_SUB, :], rhs_z[t])

    def cmp_stage(rows):
        row = lax.broadcasted_iota(jnp.int32, (rows, QL), 0)
        for t in tiles:
            sc_scr[t, 0:rows, :] = _dot(kc_ref[0:rows, :], rhs_z[t])
        other_scores()
        for t in tiles:
            qi = qis[t]
            st = pl.multiple_of(jnp.maximum(8 * qi - 8, 0), 8)
            sc_scr[t, pl.ds(st, 16), :] += jnp.where(qi == 0, cmpt_ref[1], cmpt_ref[0])
            s = jnp.where(row < 8 * qi + 8, sc_scr[t, 0:rows, :], NEG)
            m = jnp.max(s, axis=0, keepdims=True)
            p = jnp.exp2(s - m)
            pv = _dot(vcT_ref[:, 0:rows], p.astype(BF16))
            inv = jnp.where(m > 0.1 * NEG, 1.0 / pv[HEAD_DIM:HEAD_DIM + 1, :], 0.0)
            oc_scr[t] = pv[0:HEAD_DIM, :] * inv
            pn = p * inv
            psum = (pn[:, 0:Q_BLOCK] + pn[:, Q_BLOCK:2 * Q_BLOCK]
                    + pn[:, 2 * Q_BLOCK:3 * Q_BLOCK] + pn[:, 3 * Q_BLOCK:4 * Q_BLOCK])
            phi, plo = _split_bf16(psum)
            imp_scr[t] = _dot(ov_ref[:, 0:rows], phi) + _dot(ov_ref[:, 0:rows], plo)

    cmp_rows = min(CMP_ROWS, nc)
    need = (8 * qis[-1] + 8 + cmp_rows - 1) // cmp_rows
    for k in range(1, nc // cmp_rows + 1):
        pl.when(need == k)(functools.partial(cmp_stage, k * cmp_rows))
    imp = [imp_scr[t] for t in tiles]


    jrow = lax.broadcasted_iota(jnp.int32, (ns, Q_BLOCK), 0)
    qlane = lax.broadcasted_iota(jnp.int32, (ns, Q_BLOCK), 1)
    rowf = jrow.astype(F32)
    future, vals = [], []
    for t in tiles:
        cur = 2 * qis[t] + jnp.where(qlane >= SEL_BLOCK, 1, 0)
        fut = jrow > cur
        forced = (jrow == 0) | (jrow == cur) | (jrow == cur - 1)
        future.append(fut)
        vals.append(jnp.where(forced, -jnp.inf, jnp.where(fut, NEG, imp[t])))
    win_m = [None] * Q_TILES
    win_o = [None] * Q_TILES

    def win_max(t, a):
        ma = jnp.max(sw_scr[t, a * Q_BLOCK:(a + 1) * Q_BLOCK, :], axis=0, keepdims=True)
        win_m[t] = ma if win_m[t] is None else jnp.maximum(win_m[t], ma)

    def win_pv(t, a):
        pb = jnp.exp2(sw_scr[t, a * Q_BLOCK:(a + 1) * Q_BLOCK, :] - win_m[t]).astype(BF16)
        pv = _dot(vwT_ref[:, pl.ds(starts_c[t][a], Q_BLOCK)], pb)
        win_o[t] = pv if win_o[t] is None else win_o[t] + pv

    filler = [functools.partial(win_max, t, a) for a in range(n_win) for t in tiles]
    filler += [functools.partial(win_pv, t, a) for a in range(n_win) for t in tiles]
    per_round = -(-len(filler) // (SEL_TOP_N - 3))
    for r in range(SEL_TOP_N - 3):
        for t in tiles:
            v = vals[t]
            mx = jnp.max(v, axis=0, keepdims=True)
            idx = jnp.min(jnp.where(v == mx, rowf, float(ns)), axis=0, keepdims=True)
            vals[t] = jnp.where(rowf == idx, -jnp.inf, v)
        for f in filler[r * per_round:(r + 1) * per_round]:
            f()

    o_w = []
    for t in tiles:
        o_w.append(win_o[t][0:HEAD_DIM, :] / win_o[t][HEAD_DIM:HEAD_DIM + 1, :])
        sb = jnp.where((vals[t] == -jnp.inf) & jnp.logical_not(future[t]), 0.0, NEG)
        sb_scr[t] = sb
        sb_far = jnp.where(jrow < 2 * qis[t] - 2, sb, NEG)
        selb_far[t][...] = tile4(sb_far).astype(BF16)
        s_nxt[t] += jnp.concatenate(
            [jnp.broadcast_to(tile4(sb_far[b:b + 1, :]), (SEL_BLOCK, QL)) for b in range(FAR_SUB // SEL_BLOCK)],
            axis=0)
        m_scr[t] = jnp.full((1, QL), NEG, F32)
        acc_scr[t] = jnp.zeros((V_ROWS, QL), F32)
        p_pend[t] = jnp.zeros((FAR_SUB, QL), BF16)
        a_pend[t] = jnp.ones((1, QL), F32)

    def far_scores(jt, rhs, u):
        k0 = pl.multiple_of(jt * FAR_KEYS + u * FAR_SUB, FAR_SUB)
        return _dot(ks_ref[pl.ds(k0, FAR_SUB), :], rhs)

    def far_rhs(t, jt):
        r0 = pl.multiple_of(jt * FAR_BLOCKS, FAR_BLOCKS)
        return jnp.concatenate([rhs_z[t, 0:HEAD_DIM, :], selb_far[t][pl.ds(r0, FAR_BLOCKS), :],
                                jnp.zeros((LANES - HEAD_DIM - FAR_BLOCKS, QL), BF16)], axis=0)

    def far_pv(jt, u, p_bf16):
        k0 = pl.multiple_of(jt * FAR_KEYS + u * FAR_SUB, FAR_SUB)
        return _dot(vsT_ref[:, pl.ds(k0, FAR_SUB)], p_bf16)

    def far_body(t, jt, carry):
        jn = jnp.minimum(jt + 1, last_tile)
        jp = jnp.maximum(jt - 1, 0)
        rhs_cur, rhs_nxt = far_rhs(t, jt), far_rhs(t, jn)
        m, acc, s_cur = m_scr[t], acc_scr[t], s_nxt[t]
        pend = (jp, n_sub - 1, p_pend[t], a_pend[t])
        for u in range(n_sub):
            if u + 1 < n_sub:
                s_new = far_scores(jt, rhs_cur, u + 1)
            else:
                s_nxt[t] = far_scores(jn, rhs_nxt, 0)
            acc = pend[3] * acc + far_pv(pend[0], pend[1], pend[2])
            m_new = jnp.maximum(m, jnp.max(s_cur, axis=0, keepdims=True))
            alpha = jnp.exp2(m - m_new)
            p = jnp.exp2(s_cur - m_new)
            m = m_new
            pend = (jt, u, p.astype(BF16), alpha)
            if u + 1 < n_sub:
                s_cur = s_new
        p_pend[t] = pend[2]
        a_pend[t] = pend[3]
        m_scr[t] = m
        acc_scr[t] = acc
        return carry

    def far_body_pair(t, jp, carry):
        far_body(t, 2 * jp, carry)
        return far_body(t, 2 * jp + 1, carry)

    n_fars = [jnp.maximum(2 * qi - 2 + FAR_BLOCKS - 1, 0) // FAR_BLOCKS for qi in qis]
    for t in tiles:
        lax.fori_loop(0, n_fars[t] // 2, functools.partial(far_body_pair, t), 0)
        lax.fori_loop(2 * (n_fars[t] // 2), n_fars[t], functools.partial(far_body, t), 0)

    krow = lax.broadcasted_iota(jnp.int32, (Q_BLOCK, QL), 0)
    for t in tiles:
        qi = qis[t]
        acc = a_pend[t] * acc_scr[t] + far_pv(jnp.maximum(n_fars[t] - 1, 0), n_sub - 1, p_pend[t])
        sel_a = tile4(sb_scr[t, pl.ds(jnp.maximum(2 * qi - 2, 0), 1), :])
        sel_b = tile4(sb_scr[t, pl.ds(jnp.maximum(2 * qi - 1, 0), 1), :])
        s_p = sn_scr[t, 0:Q_BLOCK, :] + jnp.where(krow < SEL_BLOCK, sel_a, sel_b)
        s_p = jnp.where(qi >= 1, s_p, NEG)
        s_d = sn_scr[t, Q_BLOCK:, :]
        m = m_scr[t]
        m_new = jnp.maximum(m, jnp.maximum(jnp.max(s_p, axis=0, keepdims=True),
                                           jnp.max(s_d, axis=0, keepdims=True)))
        acc = (jnp.exp2(m - m_new) * acc
               + _dot(vsT_ref[:, pl.ds(tps[t], Q_BLOCK)], jnp.exp2(s_p - m_new).astype(BF16))
               + _dot(vsT_ref[:, pl.ds(t0s[t], Q_BLOCK)], jnp.exp2(s_d - m_new).astype(BF16)))
        o_s = acc[0:HEAD_DIM, :] / acc[HEAD_DIM:HEAD_DIM + 1, :]

        heads = []
        for h in range(HEADS_PER_GROUP):
            sl = slice(h * Q_BLOCK, (h + 1) * Q_BLOCK)
            heads.append(gate_ref[3 * h:3 * h + 1, qcols[t]] * oc_scr[t, :, sl]
                         + gate_ref[3 * h + 1:3 * h + 2, qcols[t]] * o_s[:, sl]
                         + gate_ref[3 * h + 2:3 * h + 3, qcols[t]] * o_w[t][:, sl])
        o_ref[qcols[t], :] = jnp.concatenate(heads, axis=0).T.astype(BF16)


def _nsa_attention(qT, kcmp, vcmpT, ks, vsT, kw, vwT, gT, tables):
    bsz, _, seq = qT.shape
    assert seq % FAR_KEYS == 0
    nq = seq // Q_BLOCK
    nc = seq // CMP_STRIDE
    ns = seq // SEL_BLOCK
    near, win0, cmpt = tables
    cs = np.arange(nc) * CMP_STRIDE
    ss = np.arange(ns) * SEL_BLOCK
    ov = (cs[None, :] <= ss[:, None] + SEL_BLOCK - 1) & (cs[None, :] + CMP_BLOCK - 1 >= ss[:, None])
    ov = jnp.asarray(ov, BF16)
    gates = gT.reshape(bsz, 2 * GROUPS, 16, seq)
    per_bg = lambda b, g, i: (b, g, 0, 0)
    qw = Q_TILES * Q_BLOCK
    assert nq % Q_TILES == 0
    return pl.pallas_call(
        _attn_kernel,
        grid=(bsz, GROUPS, nq // Q_TILES),
        in_specs=[
            pl.BlockSpec((None, KV_DIM, qw), lambda b, g, i: (b, g, i)),
            pl.BlockSpec((None, None, nc, LANES), per_bg),
            pl.BlockSpec((None, None, V_ROWS, nc), per_bg),
            pl.BlockSpec((None, None, seq, LANES), per_bg),
            pl.BlockSpec((None, None, V_ROWS, seq), per_bg),
            pl.BlockSpec((None, None, seq, LANES), per_bg),
            pl.BlockSpec((None, None, V_ROWS, seq), per_bg),
            pl.BlockSpec((None, None, 16, qw), lambda b, g, i: (b, g, 0, i)),
            pl.BlockSpec((None, 2, Q_BLOCK, QL), lambda b, g, i: (g, 0, 0, 0)),
            pl.BlockSpec((Q_BLOCK, QL), lambda b, g, i: (0, 0)),
            pl.BlockSpec((None, 2, 16, QL), lambda b, g, i: (g, 0, 0, 0)),
            pl.BlockSpec((ns, nc), lambda b, g, i: (0, 0)),
        ],
        out_specs=pl.BlockSpec((None, qw, KV_DIM), lambda b, g, i: (b, i, g)),
        out_shape=jax.ShapeDtypeStruct((bsz, seq, HEADS * HEAD_DIM), BF16),
        scratch_shapes=[
            pltpu.VMEM((Q_TILES, LANES, QL), BF16),
            pltpu.VMEM((Q_TILES, ns, Q_BLOCK), F32),
            pltpu.VMEM((Q_TILES, nc, QL), F32),
            pltpu.VMEM((Q_TILES, WINDOW + Q_BLOCK, QL), F32),
            pltpu.VMEM((Q_TILES, 2 * Q_BLOCK, QL), F32),
            pltpu.VMEM((Q_TILES, FAR_SUB, QL), F32),
            pltpu.VMEM((Q_TILES, FAR_SUB, QL), BF16),
            pltpu.VMEM((Q_TILES, 1, QL), F32),
            pltpu.VMEM((Q_TILES, V_ROWS, QL), F32),
            pltpu.VMEM((Q_TILES, 1, QL), F32),
            pltpu.VMEM((Q_TILES, HEAD_DIM, QL), F32),
            pltpu.VMEM((Q_TILES, ns, Q_BLOCK), F32),
        ] +[pltpu.VMEM((ns, QL), BF16)] * Q_TILES,
        compiler_params=_cparams(("arbitrary", "arbitrary", "arbitrary")),
        name="nsa_attention",
    )(qT, kcmp, vcmpT, ks, vsT, kw, vwT, gates, near, win0, cmpt, ov)


def _oproj_kernel(x_ref, mod_ref, a_ref, w_ref, o_ref):
    o_ref[...] = x_ref[...] + mod_ref[5:6, :] * _dot(a_ref[...], w_ref[...])


def _nsa_out(x, mod, attn, w_out):
    bsz, seq, d = x.shape
    tm = 2 * TOKEN_TILE
    return pl.pallas_call(
        _oproj_kernel,
        grid=(bsz, seq // tm),
        in_specs=[
            pl.BlockSpec((None, tm, d), lambda b, i: (b, i, 0)),
            pl.BlockSpec((None, 9, d), lambda b, i: (b, 0, 0)),
            pl.BlockSpec((None, tm, d), lambda b, i: (b, i, 0)),
            _resident((d, d), lambda b, i: (0, 0)),
        ],
        out_specs=pl.BlockSpec((None, tm, d), lambda b, i: (b, i, 0)),
        out_shape=jax.ShapeDtypeStruct(x.shape, F32),
        compiler_params=_cparams(("arbitrary", "arbitrary")),
        name="nsa_out",
    )(x, mod, attn, w_out.astype(BF16))


def _nsa_mixer(x, mod, g, w_in, cmp_pos, cmp_w1, cmp_w2, q_gain, k_gain, w_out, tables):
    qT, kvc, ks, vsT, kw, vwT, gT = _nsa_proj(x, mod, g, w_in, q_gain, k_gain)
    kcmp = _compress(kvc, cmp_pos, cmp_w1[0], cmp_w2[0], k_gain[0], True)
    vcmpT = _compress(kvc, cmp_pos, cmp_w1[1], cmp_w2[1], k_gain[0], False)
    attn = _nsa_attention(qT, kcmp, vcmpT, ks, vsT, kw, vwT, gT, tables)
    return _nsa_out(x, mod, attn, w_out)


def kernel(x, c, norm_g, ada_w, ada_b, ffn_w_in, ffn_w_out, conv_w_in, conv_w, conv_w_out,
           nsa_w_in, nsa_cmp_pos, nsa_cmp_w1, nsa_cmp_w2, nsa_q_gain, nsa_k_gain, nsa_w_out,
           rel_bias):
    depth = ada_w.shape[0]
    mods = _ada_mod(c, ada_w, ada_b)
    tables = _bias_tables(rel_bias)
    ffn_in = ffn_w_in.astype(BF16)
    ffn_out = ffn_w_out.astype(BF16)
    conv_in = conv_w_in.astype(BF16)
    conv_out = conv_w_out.astype(BF16)
    for i in range(depth):
        mod = mods[i]
        x = _ffn(x, mod, norm_g[i, 0], ffn_in, ffn_out, i, 0, 0)
        j = i // 2
        if i % 2 == 0:
            x = _conv_mixer(x, mod, norm_g[i, 1], conv_in, conv_w[j], conv_out, j)
        else:
            x = _nsa_mixer(x, mod, norm_g[i, 1], nsa_w_in[j], nsa_cmp_pos[j], nsa_cmp_w1[j],
                           nsa_cmp_w2[j], nsa_q_gain[j], nsa_k_gain[j], nsa_w_out[j], tables)
        x = _ffn(x, mod, norm_g[i, 2], ffn_in, ffn_out, i, 1, 2)
    return x
```

```python
import functools
import math

import numpy as np
import jax
import jax.numpy as jnp
from jax import lax
from jax.experimental import pallas as pl
from jax.experimental.pallas import tpu as pltpu

F32 = jnp.float32
BF16 = jnp.bfloat16

D_FF = 2816
HEADS = 16
HEAD_DIM = 64
GROUPS = 4
HEADS_PER_GROUP = 4
KV_DIM = GROUPS * HEAD_DIM
CMP_BLOCK = 32
CMP_STRIDE = 16
CMP_HIDDEN = 256
SEL_BLOCK = 64
SEL_TOP_N = 16
WINDOW = 512
Q_BLOCK = 128
REL_BUCKETS = 32
REL_MAX_DIST = 128
RMS_EPS = 1e-6
NEG = -1e30

LANES = 128
QL = HEADS_PER_GROUP * Q_BLOCK
FAR_KEYS = 1024
FAR_BLOCKS = FAR_KEYS // SEL_BLOCK
FAR_SUB = 256
Q_TILES = 2
CMP_ROWS = 128
V_ROWS = HEAD_DIM + 16
LOG2E = math.log2(math.e)
FF_CHUNK = 256
TOKEN_TILE = 512
FFN_TOKEN_TILE = 1024
VMEM_LIMIT = 56 * 1024 * 1024


def _dot(a, b):
    return jnp.dot(a, b, preferred_element_type=F32)


def _cparams(sem):
    return pltpu.CompilerParams(dimension_semantics=sem, vmem_limit_bytes=VMEM_LIMIT)


def _resident(shape, index_map):
    return pl.BlockSpec(shape, index_map, pipeline_mode=pl.Buffered(1))


def _adaln(x, g, mod_ref, sub):
    ms = jnp.mean(x * x, axis=-1, keepdims=True)
    shift = mod_ref[3 * sub:3 * sub + 1, :]
    scale = mod_ref[3 * sub + 1:3 * sub + 2, :]
    return (x * lax.rsqrt(ms + RMS_EPS) * g) * (1.0 + scale) + shift


def _split_bf16(v):
    hi = v.astype(BF16)
    lo = (v - hi.astype(F32)).astype(BF16)
    return hi, lo


def _ada_kernel(c_ref, w_ref, b_ref, o_ref):
    c = c_ref[...]
    cond = c * jax.nn.sigmoid(c)
    chi, clo = _split_bf16(cond)
    whi, wlo = _split_bf16(w_ref[...])
    o_ref[...] = _dot(chi, whi) + _dot(chi, wlo) + _dot(clo, whi) + b_ref[...]


def _ada_mod(c, ada_w, ada_b):
    depth, d, n = ada_w.shape
    bsz = c.shape[0]
    rows = 8
    tn = n // 4
    c_pad = jnp.zeros((rows, d), F32).at[:bsz].set(c)
    out = pl.pallas_call(
        _ada_kernel,
        grid=(depth, n // tn),
        in_specs=[
            pl.BlockSpec((rows, d), lambda l, j: (0, 0)),
            pl.BlockSpec((None, d, tn), lambda l, j: (l, 0, j)),
            pl.BlockSpec((None, 1, tn), lambda l, j: (l, 0, j)),
        ],
        out_specs=pl.BlockSpec((None, rows, tn), lambda l, j: (l, 0, j)),
        out_shape=jax.ShapeDtypeStruct((depth, rows, n), F32),
        compiler_params=_cparams(("arbitrary", "arbitrary")),
        name="ada_mod",
    )(c_pad, ada_w, ada_b.reshape(depth, 1, n))
    return out[:, :bsz].reshape(depth, bsz, 9, d)


def _ffn_kernel(x_ref, mod_ref, g_ref, win_ref, wout_ref, o_ref, h_scr, acc_scr, *, sub):
    n_half, hm, _ = h_scr.shape
    nchunk = D_FF // FF_CHUNK
    gate = 0.5 * mod_ref[3 * sub + 2:3 * sub + 3, :]
    for hh in range(n_half):
        rows = slice(hh * hm, (hh + 1) * hm)
        h_scr[hh] = _adaln(x_ref[rows, :], g_ref[...], mod_ref, sub).astype(BF16)

        def up(c):
            h = h_scr[hh]
            lo = c * FF_CHUNK
            return (_dot(h, win_ref[:, lo:lo + FF_CHUNK]),
                    _dot(h, win_ref[:, D_FF + lo:D_FF + lo + FF_CHUNK]))

        nxt = up(0)
        for c in range(nchunk):
            gg, uu = nxt
            if c + 1 < nchunk:
                nxt = up(c + 1)
            a = (gg * jax.nn.sigmoid(gg) * uu).astype(BF16)
            down = _dot(a, wout_ref[c * FF_CHUNK:(c + 1) * FF_CHUNK, :])
            if c == 0:
                acc_scr[hh] = down
            else:
                acc_scr[hh] += down
        o_ref[rows, :] = x_ref[rows, :] + gate * acc_scr[hh]


def _ffn(x, mod, g, w_in_all, w_out_all, layer, slot, sub):
    bsz, seq, d = x.shape
    tm = FFN_TOKEN_TILE
    hm = tm // 2
    return pl.pallas_call(
        functools.partial(_ffn_kernel, sub=sub),
        grid=(bsz, seq // tm),
        in_specs=[
            pl.BlockSpec((None, tm, d), lambda b, i: (b, i, 0)),
            pl.BlockSpec((None, 9, d), lambda b, i: (b, 0, 0)),
            pl.BlockSpec((1, d), lambda b, i: (0, 0)),
            _resident((None, None, d, 2 * D_FF), lambda b, i: (layer, slot, 0, 0)),
            _resident((None, None, D_FF, d), lambda b, i: (layer, slot, 0, 0)),
        ],
        out_specs=pl.BlockSpec((None, tm, d), lambda b, i: (b, i, 0)),
        out_shape=jax.ShapeDtypeStruct(x.shape, F32),
        scratch_shapes=[pltpu.VMEM((2, hm, d), BF16), pltpu.VMEM((2, hm, d), F32)],
        compiler_params=_cparams(("arbitrary", "arbitrary")),
        name="ffn",
    )(x, mod, g.reshape(1, d), w_in_all, w_out_all)


def _conv_kernel(x_ref, mod_ref, g_ref, win_ref, cw_ref, wout_ref, o_ref, ubuf):
    tm, d = x_ref.shape
    hm = tm // 2

    @pl.when(pl.program_id(1) == 0)
    def _():
        ubuf[0:8, :] = jnp.zeros((8, d), F32)

    for hh in range(2):
        r0 = hh * hm
        x = x_ref[r0:r0 + hm, :]
        hb = _adaln(x, g_ref[...], mod_ref, 1).astype(BF16)
        bg = _dot(hb, win_ref[:, 0:d])
        u = _dot(hb, win_ref[:, d:2 * d]) * _dot(hb, win_ref[:, 2 * d:3 * d])
        ubuf[8 + r0:8 + r0 + hm, :] = u
        y = (cw_ref[2:3, :] * u + cw_ref[1:2, :] * ubuf[7 + r0:7 + r0 + hm, :]
             + cw_ref[0:1, :] * ubuf[6 + r0:6 + r0 + hm, :])
        out = _dot((bg * y).astype(BF16), wout_ref[...])
        o_ref[r0:r0 + hm, :] = x + mod_ref[5:6, :] * out
    ubuf[0:8, :] = ubuf[tm:tm + 8, :]


def _conv_mixer(x, mod, g, w_in_all, conv_w, w_out_all, layer):
    bsz, seq, d = x.shape
    tm = 2 * TOKEN_TILE
    return pl.pallas_call(
        _conv_kernel,
        grid=(bsz, seq // tm),
        in_specs=[
            pl.BlockSpec((None, tm, d), lambda b, i: (b, i, 0)),
            pl.BlockSpec((None, 9, d), lambda b, i: (b, 0, 0)),
            pl.BlockSpec((1, d), lambda b, i: (0, 0)),
            _resident((None, d, 3 * d), lambda b, i: (layer, 0, 0)),
            pl.BlockSpec((3, d), lambda b, i: (0, 0)),
            _resident((None, d, d), lambda b, i: (layer, 0, 0)),
        ],
        out_specs=pl.BlockSpec((None, tm, d), lambda b, i: (b, i, 0)),
        out_shape=jax.ShapeDtypeStruct(x.shape, F32),
        scratch_shapes=[pltpu.VMEM((tm + 8, d), F32)],
        compiler_params=_cparams(("arbitrary", "arbitrary")),
        name="conv_mixer",
    )(x, mod, g.reshape(1, d), w_in_all, conv_w, w_out_all)


def _head_ms(sq, p_ref):
    hi, lo = _split_bf16(sq)
    return _dot(hi, p_ref[...]) + _dot(lo, p_ref[...])


def _proj_kernel(x_ref, mod_ref, g_ref, w_ref, wT_ref, p_ref, qg_ref, kg_ref,
                 qT_ref, kvc_ref, ks_ref, vsT_ref, kw_ref, vwT_ref, gT_ref):
    tm = x_ref.shape[0]
    for hh in range(2):
        _proj_half(hh * (tm // 2), tm // 2, x_ref, mod_ref, g_ref, w_ref, wT_ref, p_ref, qg_ref, kg_ref,
                   qT_ref, kvc_ref, ks_ref, vsT_ref, kw_ref, vwT_ref, gT_ref)


def _proj_half(r0, hm, x_ref, mod_ref, g_ref, w_ref, wT_ref, p_ref, qg_ref, kg_ref,
               qT_ref, kvc_ref, ks_ref, vsT_ref, kw_ref, vwT_ref, gT_ref):
    tok = slice(r0, r0 + hm)
    hb = _adaln(x_ref[tok, :], g_ref[...], mod_ref, 1).astype(BF16)

    def col(i, width=KV_DIM):
        return _dot(hb, w_ref[:, i:i + width])

    def colT(i, width=KV_DIM):
        return lax.dot_general(wT_ref[i:i + width, :], hb, (((1,), (1,)), ((), ())),
                               preferred_element_type=F32)

    qg = jnp.concatenate([qg_ref[...]] * (hm // LANES), axis=1)
    for c in range(GROUPS):
        qc = colT(c * KV_DIM)
        ms = jnp.mean((qc * qc).reshape(HEADS_PER_GROUP, HEAD_DIM, hm), axis=1, keepdims=True)
        ms = jnp.broadcast_to(ms, (HEADS_PER_GROUP, HEAD_DIM, hm)).reshape(KV_DIM, hm)
        qn = qc * lax.rsqrt(ms + RMS_EPS) * qg
        qT_ref[c * KV_DIM:(c + 1) * KV_DIM, tok] = qn.astype(BF16)
    vsT = colT(HEADS * HEAD_DIM).astype(BF16)
    vwT = colT(HEADS * HEAD_DIM + KV_DIM).astype(BF16)
    rid = lax.broadcasted_iota(jnp.int32, (V_ROWS - HEAD_DIM, hm), 0)
    ones_rows = jnp.where(rid == 0, 1.0, 0.0).astype(BF16)
    for g in range(GROUPS):
        vsT_ref[g, :, tok] = jnp.concatenate([vsT[g * HEAD_DIM:(g + 1) * HEAD_DIM, :], ones_rows], axis=0)
        vwT_ref[g, :, tok] = jnp.concatenate([vwT[g * HEAD_DIM:(g + 1) * HEAD_DIM, :], ones_rows], axis=0)
    gT_ref[:, tok] = jax.nn.sigmoid(colT(HEADS * HEAD_DIM + 2 * KV_DIM, LANES))

    for kv in range(2):
        kvc = col(kv * KV_DIM)
        for half in range(2):
            kvc_ref[kv, half, tok, :] = kvc[:, half * LANES:(half + 1) * LANES]

    pos = pl.program_id(1) * x_ref.shape[0] + r0 + lax.broadcasted_iota(jnp.int32, (hm, HEAD_DIM), 0)
    lane = lax.broadcasted_iota(jnp.int32, (hm, HEAD_DIM), 1)
    onehot = jnp.where(lane == ((pos // SEL_BLOCK) % FAR_BLOCKS), 1.0, 0.0)
    zeros = jnp.zeros((hm, HEAD_DIM), F32)

    ks = col(2 * KV_DIM)
    ksn = ks * lax.rsqrt(_head_ms(ks * ks, p_ref) + RMS_EPS) * kg_ref[0:1, :]
    kw = col(3 * KV_DIM)
    kwn = kw * lax.rsqrt(_head_ms(kw * kw, p_ref) + RMS_EPS) * kg_ref[1:2, :]
    for g in range(GROUPS):
        sl = slice(g * HEAD_DIM, (g + 1) * HEAD_DIM)
        ks_ref[g, tok, :] = jnp.concatenate([ksn[:, sl], onehot], axis=1).astype(BF16)
        kw_ref[g, tok, :] = jnp.concatenate([kwn[:, sl], zeros], axis=1).astype(BF16)


def _nsa_proj(x, mod, g, w_in, q_gain, k_gain):
    bsz, seq, d = x.shape
    tm = 2 * TOKEN_TILE
    nq = HEADS * HEAD_DIM
    q, kc, vc, ks, vs, kw, vw, gl = jnp.split(w_in, [nq + i * KV_DIM for i in range(7)], axis=1)
    gl = gl.reshape(d, GROUPS, 3 * HEADS_PER_GROUP)
    gl = jnp.pad(gl, ((0, 0), (0, GROUPS), (0, 4))).reshape(d, LANES)
    w = jnp.concatenate([kc, vc, ks, kw], axis=1).astype(BF16)
    wT = jnp.concatenate([q, vs, vw, gl], axis=1).T.astype(BF16)
    hid = np.arange(KV_DIM) // HEAD_DIM
    pmat = jnp.asarray((hid[:, None] == hid[None, :]) / HEAD_DIM, BF16)
    qg = jnp.tile(q_gain, HEADS_PER_GROUP) * (HEAD_DIM ** -0.5 * LOG2E)
    qg = jnp.broadcast_to(qg[:, None], (KV_DIM, LANES))
    kg = jnp.stack([jnp.tile(k_gain[1], GROUPS), jnp.tile(k_gain[2], GROUPS)])
    kv_spec = pl.BlockSpec((None, GROUPS, tm, LANES), lambda b, i: (b, 0, i, 0))
    vT_spec = pl.BlockSpec((None, GROUPS, V_ROWS, tm), lambda b, i: (b, 0, 0, i))
    kv_shape = jax.ShapeDtypeStruct((bsz, GROUPS, seq, LANES), BF16)
    vT_shape = jax.ShapeDtypeStruct((bsz, GROUPS, V_ROWS, seq), BF16)
    return pl.pallas_call(
        _proj_kernel,
        grid=(bsz, seq // tm),
        in_specs=[
            pl.BlockSpec((None, tm, d), lambda b, i: (b, i, 0)),
            pl.BlockSpec((None, 9, d), lambda b, i: (b, 0, 0)),
            pl.BlockSpec((1, d), lambda b, i: (0, 0)),
            _resident(w.shape, lambda b, i: (0, 0)),
            _resident(wT.shape, lambda b, i: (0, 0)),
            pl.BlockSpec((KV_DIM, KV_DIM), lambda b, i: (0, 0)),
            pl.BlockSpec((KV_DIM, LANES), lambda b, i: (0, 0)),
            pl.BlockSpec((2, KV_DIM), lambda b, i: (0, 0)),
        ],
        out_specs=[
            pl.BlockSpec((None, nq, tm), lambda b, i: (b, 0, i)),
            pl.BlockSpec((2, 2, None, tm, LANES), lambda b, i: (0, 0, b, i, 0)),
            kv_spec, vT_spec, kv_spec, vT_spec,
            pl.BlockSpec((None, LANES, tm), lambda b, i: (b, 0, i)),
        ],
        out_shape=[
            jax.ShapeDtypeStruct((bsz, nq, seq), BF16),
            jax.ShapeDtypeStruct((2, 2, bsz, seq, LANES), F32),
            kv_shape, vT_shape, kv_shape, vT_shape,
            jax.ShapeDtypeStruct((bsz, LANES, seq), F32),
        ],
        compiler_params=_cparams(("arbitrary", "arbitrary")),
        name="nsa_proj",
    )(x, mod, g.reshape(1, d), w, wT, pmat, qg, kg)


def _gelu_tanh(x):
    return 0.5 * x * (1.0 + jnp.tanh(math.sqrt(2.0 / math.pi) * (x + 0.044715 * (x * x * x))))


def _compress_kernel(kv_ref, pos_ref, w1_ref, w2_ref, kg_ref, o_ref, acc_lo, acc_hi, *, is_key):
    nc = acc_lo.shape[1]
    half = CMP_BLOCK // 2
    acc_lo[...] = jnp.zeros_like(acc_lo)
    acc_hi[...] = jnp.zeros_like(acc_hi)
    for l in range(half):
        xl = [kv_ref[j, pl.ds(l, nc, stride=CMP_STRIDE), :] for j in range(2)]
        for g in range(GROUPS):
            xg = xl[g // 2][:, (g % 2) * HEAD_DIM:(g % 2 + 1) * HEAD_DIM]
            lo = (xg + pos_ref[l:l + 1, :]).astype(BF16)
            hi = (xg + pos_ref[half + l:half + l + 1, :]).astype(BF16)
            acc_lo[g] += _dot(lo, w1_ref[l * HEAD_DIM:(l + 1) * HEAD_DIM, :])
            acc_hi[g] += _dot(hi, w1_ref[(half + l) * HEAD_DIM:(half + l + 1) * HEAD_DIM, :])
    row = lax.broadcasted_iota(jnp.int32, (nc, CMP_HIDDEN), 0)
    for g in range(GROUPS):
        nxt = jnp.where(row < nc - 1, pltpu.roll(acc_hi[g], nc - 1, 0), 0.0)
        hdn = _gelu_tanh(acc_lo[g] + nxt).astype(BF16)
        if is_key:
            out = _dot(hdn, w2_ref[...])
            ms = jnp.sum(out * out, axis=-1, keepdims=True) * (1.0 / HEAD_DIM)
            o_ref[g] = (out * lax.rsqrt(ms + RMS_EPS) * kg_ref[...]).astype(BF16)
        else:
            outT = lax.dot_general(w2_ref[...], hdn, (((1,), (1,)), ((), ())),
                                   preferred_element_type=F32)
            rid = lax.broadcasted_iota(jnp.int32, (V_ROWS - HEAD_DIM, nc), 0)
            ones_rows = jnp.where(rid == 0, 1.0, 0.0).astype(BF16)
            o_ref[g] = jnp.concatenate([outT.astype(BF16), ones_rows], axis=0)


def _compress(kvc, pos, w1, w2, k_gain, is_key):
    _, _, bsz, seq, _ = kvc.shape
    which = 0 if is_key else 1
    nc = seq // CMP_STRIDE
    kg = jnp.pad(k_gain, (0, LANES - HEAD_DIM)).reshape(1, LANES)
    if is_key:
        w2p = jnp.pad(w2, ((0, 0), (0, LANES - HEAD_DIM))).astype(BF16)
    else:
        w2p = w2.T.astype(BF16)
    if is_key:
        out_spec = pl.BlockSpec((None, GROUPS, nc, LANES), lambda b: (b, 0, 0, 0))
        out_shape = jax.ShapeDtypeStruct((bsz, GROUPS, nc, LANES), BF16)
    else:
        out_spec = pl.BlockSpec((None, GROUPS, V_ROWS, nc), lambda b: (b, 0, 0, 0))
        out_shape = jax.ShapeDtypeStruct((bsz, GROUPS, V_ROWS, nc), BF16)
    return pl.pallas_call(
        functools.partial(_compress_kernel, is_key=is_key),
        grid=(bsz,),
        in_specs=[
            pl.BlockSpec((None, 2, None, seq, LANES), lambda b: (which, 0, b, 0, 0)),
            pl.BlockSpec((CMP_BLOCK, HEAD_DIM), lambda b: (0, 0)),
            pl.BlockSpec((CMP_BLOCK * HEAD_DIM, CMP_HIDDEN), lambda b: (0, 0)),
            pl.BlockSpec(w2p.shape, lambda b: (0, 0)),
            pl.BlockSpec((1, LANES), lambda b: (0, 0)),
        ],
        out_specs=out_spec,
        out_shape=out_shape,
        scratch_shapes=[pltpu.VMEM((GROUPS, nc, CMP_HIDDEN), F32),
                        pltpu.VMEM((GROUPS, nc, CMP_HIDDEN), F32)],
        compiler_params=_cparams(("arbitrary",)),
        name="compress_k" if is_key else "compress_v",
    )(kvc, pos, w1.astype(BF16), w2p, kg)


def _t5_bucket_np(dist):
    n = np.maximum(dist, 0)
    max_exact = REL_BUCKETS // 2
    nf = np.maximum(n, 1).astype(np.float32)
    large = max_exact + (np.log(nf / max_exact) / math.log(REL_MAX_DIST / max_exact)
                         * (REL_BUCKETS - max_exact)).astype(np.int32)
    large = np.minimum(large, REL_BUCKETS - 1)
    return np.where(n < max_exact, n, large)


def _bias_tables(rel_bias):
    rb = rel_bias.astype(F32).T
    rel = ((rb - rb[:, REL_BUCKETS - 1:]) * LOG2E).reshape(GROUPS, HEADS_PER_GROUP, REL_BUCKETS)
    ql = np.arange(Q_BLOCK)[None, :]
    kl = np.arange(Q_BLOCK)[:, None]
    d_lo, d_hi = -2 * Q_BLOCK, 3 * Q_BLOCK
    dvec = np.arange(d_lo, d_hi)
    onehot = jnp.asarray(np.eye(REL_BUCKETS, dtype=np.float32)[_t5_bucket_np(dvec)])
    vec = jnp.einsum("ghb,db->ghd", rel, onehot, precision=lax.Precision.HIGHEST)
    vec = jnp.where(jnp.asarray(dvec >= 0), vec, NEG)

    def table(dist):
        rows = [vec[:, :, int(d0) - d_lo:int(d0) - d_lo + Q_BLOCK] for d0 in dist[:, 0]]
        return jnp.stack(rows, axis=1).reshape(GROUPS, dist.shape[0], QL)

    near = jnp.stack([table(ql - kl + Q_BLOCK), table(ql - kl)], axis=1)
    win0 = np.where(ql - kl + WINDOW < WINDOW, 0.0, NEG).astype(np.float32)
    win0 = jnp.asarray(np.tile(win0, (1, HEADS_PER_GROUP)))
    r = np.arange(16)[:, None]
    cmp = table(ql - CMP_STRIDE * (r - 8) - (CMP_BLOCK - 1))
    cmp_first = jnp.concatenate([cmp[:, 8:], jnp.zeros_like(cmp[:, 8:])], axis=1)
    return near, win0, jnp.stack([cmp, cmp_first], axis=1)


def _attn_kernel(qT_ref, kc_ref, vcT_ref, ks_ref, vsT_ref, kw_ref, vwT_ref, gate_ref,
                 near_ref, win0_ref, cmpt_ref, ov_ref, o_ref,
                 rhs_z, sb_scr, sc_scr, sw_scr, sn_scr, s_nxt, p_pend, a_pend,
                 acc_scr, m_scr, oc_scr, imp_scr, *selb_far):
    nc = kc_ref.shape[0]
    ns = ov_ref.shape[0]
    n_win = WINDOW // Q_BLOCK + 1
    n_sub = FAR_KEYS // FAR_SUB
    last_tile = ks_ref.shape[0] // FAR_KEYS - 1
    tiles = range(Q_TILES)
    qis = [Q_TILES * pl.program_id(2) + t for t in tiles]
    t0s = [pl.multiple_of(qi * Q_BLOCK, Q_BLOCK) for qi in qis]
    tps = [pl.multiple_of(jnp.maximum(t0 - Q_BLOCK, 0), Q_BLOCK) for t0 in t0s]
    qcols = [slice(t * Q_BLOCK, (t + 1) * Q_BLOCK) for t in tiles]

    def tile4(v):
        return jnp.concatenate([v] * HEADS_PER_GROUP, axis=1)

    for t in tiles:
        q4 = qT_ref[:, qcols[t]]
        qt = jnp.concatenate([q4[h * HEAD_DIM:(h + 1) * HEAD_DIM, :] for h in range(HEADS_PER_GROUP)], axis=1)
        rhs_z[t, 0:HEAD_DIM, :] = qt
        rhs_z[t, HEAD_DIM:, :] = jnp.zeros((LANES - HEAD_DIM, QL), BF16)

    starts = [[t0s[t] - Q_BLOCK * (n_win - 1 - a) for a in range(n_win)] for t in tiles]
    starts_c = [[pl.multiple_of(jnp.maximum(st, 0), Q_BLOCK) for st in starts[t]] for t in tiles]

    def other_scores():
        for t in tiles:
            for a in range(n_win):
                s = _dot(kw_ref[pl.ds(starts_c[t][a], Q_BLOCK), :], rhs_z[t])
                if a == 0:
                    s = s + win0_ref[...]
                elif a == n_win - 2:
                    s = s + near_ref[0]
                elif a == n_win - 1:
                    s = s + near_ref[1]
                if a < n_win - 1:
                    s = jnp.where(starts[t][a] >= 0, s, NEG)
                sw_scr[t, a * Q_BLOCK:(a + 1) * Q_BLOCK, :] = s
            sn_scr[t, 0:Q_BLOCK, :] = _dot(ks_ref[pl.ds(tps[t], Q_BLOCK), :], rhs_z[t]) + near_ref[0]
            sn_scr[t, Q_BLOCK:, :] = _dot(ks_ref[pl.ds(t0s[t], Q_BLOCK), :], rhs_z[t]) + near_ref[1]
            s_nxt[t] = _dot(ks_ref[0:FAR_SUB, :], rhs_z[t])

    def cmp_stage(rows):
        row = lax.broadcasted_iota(jnp.int32, (rows, QL), 0)
        for t in tiles:
            sc_scr[t, 0:rows, :] = _dot(kc_ref[0:rows, :], rhs_z[t])
        other_scores()
        for t in tiles:
            qi = qis[t]
            st = pl.multiple_of(jnp.maximum(8 * qi - 8, 0), 8)
            sc_scr[t, pl.ds(st, 16), :] += jnp.where(qi == 0, cmpt_ref[1], cmpt_ref[0])
            s = jnp.where(row < 8 * qi + 8, sc_scr[t, 0:rows, :], NEG)
            m = jnp.max(s, axis=0, keepdims=True)
            p = jnp.exp2(s - m)
            pv = _dot(vcT_ref[:, 0:rows], p.astype(BF16))
            inv = jnp.where(m > 0.1 * NEG, 1.0 / pv[HEAD_DIM:HEAD_DIM + 1, :], 0.0)
            oc_scr[t] = pv[0:HEAD_DIM, :] * inv
            pn = p * inv
            psum = (pn[:, 0:Q_BLOCK] + pn[:, Q_BLOCK:2 * Q_BLOCK]
                    + pn[:, 2 * Q_BLOCK:3 * Q_BLOCK] + pn[:, 3 * Q_BLOCK:4 * Q_BLOCK])
            phi, plo = _split_bf16(psum)
            imp_scr[t] = _dot(ov_ref[:, 0:rows], phi) + _dot(ov_ref[:, 0:rows], plo)

    cmp_rows = min(CMP_ROWS, nc)
    need = (8 * qis[-1] + 8 + cmp_rows - 1) // cmp_rows
    for k in range(1, nc // cmp_rows + 1):
        pl.when(need == k)(functools.partial(cmp_stage, k * cmp_rows))
    imp = [imp_scr[t] for t in tiles]


    jrow = lax.broadcasted_iota(jnp.int32, (ns, Q_BLOCK), 0)
    qlane = lax.broadcasted_iota(jnp.int32, (ns, Q_BLOCK), 1)
    rowf = jrow.astype(F32)
    future, vals = [], []
    for t in tiles:
        cur = 2 * qis[t] + jnp.where(qlane >= SEL_BLOCK, 1, 0)
        fut = jrow > cur
        forced = (jrow == 0) | (jrow == cur) | (jrow == cur - 1)
        future.append(fut)
        vals.append(jnp.where(forced, -jnp.inf, jnp.where(fut, NEG, imp[t])))
    win_m = [None] * Q_TILES
    win_o = [None] * Q_TILES

    def win_max(t, a):
        ma = jnp.max(sw_scr[t, a * Q_BLOCK:(a + 1) * Q_BLOCK, :], axis=0, keepdims=True)
        win_m[t] = ma if win_m[t] is None else jnp.maximum(win_m[t], ma)

    def win_pv(t, a):
        pb = jnp.exp2(sw_scr[t, a * Q_BLOCK:(a + 1) * Q_BLOCK, :] - win_m[t]).astype(BF16)
        pv = _dot(vwT_ref[:, pl.ds(starts_c[t][a], Q_BLOCK)], pb)
        win_o[t] = pv if win_o[t] is None else win_o[t] + pv

    filler = [functools.partial(win_max, t, a) for a in range(n_win) for t in tiles]
    filler += [functools.partial(win_pv, t, a) for a in range(n_win) for t in tiles]
    per_round = -(-len(filler) // (SEL_TOP_N - 3))
    for r in range(SEL_TOP_N - 3):
        for t in tiles:
            v = vals[t]
            mx = jnp.max(v, axis=0, keepdims=True)
            idx = jnp.min(jnp.where(v == mx, rowf, float(ns)), axis=0, keepdims=True)
            vals[t] = jnp.where(rowf == idx, -jnp.inf, v)
        for f in filler[r * per_round:(r + 1) * per_round]:
            f()

    o_w = []
    for t in tiles:
        o_w.append(win_o[t][0:HEAD_DIM, :] / win_o[t][HEAD_DIM:HEAD_DIM + 1, :])
        sb = jnp.where((vals[t] == -jnp.inf) & jnp.logical_not(future[t]), 0.0, NEG)
        sb_scr[t] = sb
        sb_far = jnp.where(jrow < 2 * qis[t] - 2, sb, NEG)
        selb_far[t][...] = tile4(sb_far).astype(BF16)
        s_nxt[t] += jnp.concatenate(
            [jnp.broadcast_to(tile4(sb_far[b:b + 1, :]), (SEL_BLOCK, QL)) for b in range(FAR_SUB // SEL_BLOCK)],
            axis=0)
        m_scr[t] = jnp.full((1, QL), NEG, F32)
        acc_scr[t] = jnp.zeros((V_ROWS, QL), F32)
        p_pend[t] = jnp.zeros((FAR_SUB, QL), BF16)
        a_pend[t] = jnp.ones((1, QL), F32)

    def far_scores(jt, rhs, u):
        k0 = pl.multiple_of(jt * FAR_KEYS + u * FAR_SUB, FAR_SUB)
        return _dot(ks_ref[pl.ds(k0, FAR_SUB), :], rhs)

    def far_rhs(t, jt):
        r0 = pl.multiple_of(jt * FAR_BLOCKS, FAR_BLOCKS)
        return jnp.concatenate([rhs_z[t, 0:HEAD_DIM, :], selb_far[t][pl.ds(r0, FAR_BLOCKS), :],
                                jnp.zeros((LANES - HEAD_DIM - FAR_BLOCKS, QL), BF16)], axis=0)

    def far_pv(jt, u, p_bf16):
        k0 = pl.multiple_of(jt * FAR_KEYS + u * FAR_SUB, FAR_SUB)
        return _dot(vsT_ref[:, pl.ds(k0, FAR_SUB)], p_bf16)

    def far_body(t, jt, carry):
        jn = jnp.minimum(jt + 1, last_tile)
        jp = jnp.maximum(jt - 1, 0)
        rhs_cur, rhs_nxt = far_rhs(t, jt), far_rhs(t, jn)
        m, acc, s_cur = m_scr[t], acc_scr[t], s_nxt[t]
        pend = (jp, n_sub - 1, p_pend[t], a_pend[t])
        for u in range(n_sub):
            if u + 1 < n_sub:
                s_new = far_scores(jt, rhs_cur, u + 1)
            else:
                s_nxt[t] = far_scores(jn, rhs_nxt, 0)
            acc = pend[3] * acc + far_pv(pend[0], pend[1], pend[2])
            m_new = jnp.maximum(m, jnp.max(s_cur, axis=0, keepdims=True))
            alpha = jnp.exp2(m - m_new)
            p = jnp.exp2(s_cur - m_new)
            m = m_new
            pend = (jt, u, p.astype(BF16), alpha)
            if u + 1 < n_sub:
                s_cur = s_new
        p_pend[t] = pend[2]
        a_pend[t] = pend[3]
        m_scr[t] = m
        acc_scr[t] = acc
        return carry

    def far_body_pair(t, jp, carry):
        far_body(t, 2 * jp, carry)
        return far_body(t, 2 * jp + 1, carry)

    n_fars = [jnp.maximum(2 * qi - 2 + FAR_BLOCKS - 1, 0) // FAR_BLOCKS for qi in qis]
    for t in tiles:
        lax.fori_loop(0, n_fars[t] // 2, functools.partial(far_body_pair, t), 0)
        lax.fori_loop(2 * (n_fars[t] // 2), n_fars[t], functools.partial(far_body, t), 0)

    krow = lax.broadcasted_iota(jnp.int32, (Q_BLOCK, QL), 0)
    for t in tiles:
        qi = qis[t]
        acc = a_pend[t] * acc_scr[t] + far_pv(jnp.maximum(n_fars[t] - 1, 0), n_sub - 1, p_pend[t])
        sel_a = tile4(sb_scr[t, pl.ds(jnp.maximum(2 * qi - 2, 0), 1), :])
        sel_b = tile4(sb_scr[t, pl.ds(jnp.maximum(2 * qi - 1, 0), 1), :])
        s_p = sn_scr[t, 0:Q_BLOCK, :] + jnp.where(krow < SEL_BLOCK, sel_a, sel_b)
        s_p = jnp.where(qi >= 1, s_p, NEG)
        s_d = sn_scr[t, Q_BLOCK:, :]
        m = m_scr[t]
        m_new = jnp.maximum(m, jnp.maximum(jnp.max(s_p, axis=0, keepdims=True),
                                           jnp.max(s_d, axis=0, keepdims=True)))
        acc = (jnp.exp2(m - m_new) * acc
               + _dot(vsT_ref[:, pl.ds(tps[t], Q_BLOCK)], jnp.exp2(s_p - m_new).astype(BF16))
               + _dot(vsT_ref[:, pl.ds(t0s[t], Q_BLOCK)], jnp.exp2(s_d - m_new).astype(BF16)))
        o_s = acc[0:HEAD_DIM, :] / acc[HEAD_DIM:HEAD_DIM + 1, :]

        heads = []
        for h in range(HEADS_PER_GROUP):
            sl = slice(h * Q_BLOCK, (h + 1) * Q_BLOCK)
            heads.append(gate_ref[3 * h:3 * h + 1, qcols[t]] * oc_scr[t, :, sl]
                         + gate_ref[3 * h + 1:3 * h + 2, qcols[t]] * o_s[:, sl]
                         + gate_ref[3 * h + 2:3 * h + 3, qcols[t]] * o_w[t][:, sl])
        o_ref[qcols[t], :] = jnp.concatenate(heads, axis=0).T.astype(BF16)


def _nsa_attention(qT, kcmp, vcmpT, ks, vsT, kw, vwT, gT, tables):
    bsz, _, seq = qT.shape
    assert seq % FAR_KEYS == 0
    nq = seq // Q_BLOCK
    nc = seq // CMP_STRIDE
    ns = seq // SEL_BLOCK
    near, win0, cmpt = tables
    cs = np.arange(nc) * CMP_STRIDE
    ss = np.arange(ns) * SEL_BLOCK
    ov = (cs[None, :] <= ss[:, None] + SEL_BLOCK - 1) & (cs[None, :] + CMP_BLOCK - 1 >= ss[:, None])
    ov = jnp.asarray(ov, BF16)
    gates = gT.reshape(bsz, 2 * GROUPS, 16, seq)
    per_bg = lambda b, g, i: (b, g, 0, 0)
    qw = Q_TILES * Q_BLOCK
    assert nq % Q_TILES == 0
    return pl.pallas_call(
        _attn_kernel,
        grid=(bsz, GROUPS, nq // Q_TILES),
        in_specs=[
            pl.BlockSpec((None, KV_DIM, qw), lambda b, g, i: (b, g, i)),
            pl.BlockSpec((None, None, nc, LANES), per_bg),
            pl.BlockSpec((None, None, V_ROWS, nc), per_bg),
            pl.BlockSpec((None, None, seq, LANES), per_bg),
            pl.BlockSpec((None, None, V_ROWS, seq), per_bg),
            pl.BlockSpec((None, None, seq, LANES), per_bg),
            pl.BlockSpec((None, None, V_ROWS, seq), per_bg),
            pl.BlockSpec((None, None, 16, qw), lambda b, g, i: (b, g, 0, i)),
            pl.BlockSpec((None, 2, Q_BLOCK, QL), lambda b, g, i: (g, 0, 0, 0)),
            pl.BlockSpec((Q_BLOCK, QL), lambda b, g, i: (0, 0)),
            pl.BlockSpec((None, 2, 16, QL), lambda b, g, i: (g, 0, 0, 0)),
            pl.BlockSpec((ns, nc), lambda b, g, i: (0, 0)),
        ],
        out_specs=pl.BlockSpec((None, qw, KV_DIM), lambda b, g, i: (b, i, g)),
        out_shape=jax.ShapeDtypeStruct((bsz, seq, HEADS * HEAD_DIM), BF16),
        scratch_shapes=[
            pltpu.VMEM((Q_TILES, LANES, QL), BF16),
            pltpu.VMEM((Q_TILES, ns, Q_BLOCK), F32),
            pltpu.VMEM((Q_TILES, nc, QL), F32),
            pltpu.VMEM((Q_TILES, WINDOW + Q_BLOCK, QL), F32),
            pltpu.VMEM((Q_TILES, 2 * Q_BLOCK, QL), F32),
            pltpu.VMEM((Q_TILES, FAR_SUB, QL), F32),
            pltpu.VMEM((Q_TILES, FAR_SUB, QL), BF16),
            pltpu.VMEM((Q_TILES, 1, QL), F32),
            pltpu.VMEM((Q_TILES, V_ROWS, QL), F32),
            pltpu.VMEM((Q_TILES, 1, QL), F32),
            pltpu.VMEM((Q_TILES, HEAD_DIM, QL), F32),
            pltpu.VMEM((Q_TILES, ns, Q_BLOCK), F32),
        ] +[pltpu.VMEM((ns, QL), BF16)] * Q_TILES,
        compiler_params=_cparams(("arbitrary", "arbitrary", "arbitrary")),
        name="nsa_attention",
    )(qT, kcmp, vcmpT, ks, vsT, kw, vwT, gates, near, win0, cmpt, ov)


def _oproj_kernel(x_ref, mod_ref, a_ref, w_ref, o_ref):
    o_ref[...] = x_ref[...] + mod_ref[5:6, :] * _dot(a_ref[...], w_ref[...])


def _nsa_out(x, mod, attn, w_out):
    bsz, seq, d = x.shape
    tm = 2 * TOKEN_TILE
    return pl.pallas_call(
        _oproj_kernel,
        grid=(bsz, seq // tm),
        in_specs=[
            pl.BlockSpec((None, tm, d), lambda b, i: (b, i, 0)),
            pl.BlockSpec((None, 9, d), lambda b, i: (b, 0, 0)),
            pl.BlockSpec((None, tm, d), lambda b, i: (b, i, 0)),
            _resident((d, d), lambda b, i: (0, 0)),
        ],
        out_specs=pl.BlockSpec((None, tm, d), lambda b, i: (b, i, 0)),
        out_shape=jax.ShapeDtypeStruct(x.shape, F32),
        compiler_params=_cparams(("arbitrary", "arbitrary")),
        name="nsa_out",
    )(x, mod, attn, w_out.astype(BF16))


def _nsa_mixer(x, mod, g, w_in, cmp_pos, cmp_w1, cmp_w2, q_gain, k_gain, w_out, tables):
    qT, kvc, ks, vsT, kw, vwT, gT = _nsa_proj(x, mod, g, w_in, q_gain, k_gain)
    kcmp = _compress(kvc, cmp_pos, cmp_w1[0], cmp_w2[0], k_gain[0], True)
    vcmpT = _compress(kvc, cmp_pos, cmp_w1[1], cmp_w2[1], k_gain[0], False)
    attn = _nsa_attention(qT, kcmp, vcmpT, ks, vsT, kw, vwT, gT, tables)
    return _nsa_out(x, mod, attn, w_out)


def kernel(x, c, norm_g, ada_w, ada_b, ffn_w_in, ffn_w_out, conv_w_in, conv_w, conv_w_out,
           nsa_w_in, nsa_cmp_pos, nsa_cmp_w1, nsa_cmp_w2, nsa_q_gain, nsa_k_gain, nsa_w_out,
           rel_bias):
    depth = ada_w.shape[0]
    mods = _ada_mod(c, ada_w, ada_b)
    tables = _bias_tables(rel_bias)
    ffn_in = ffn_w_in.astype(BF16)
    ffn_out = ffn_w_out.astype(BF16)
    conv_in = conv_w_in.astype(BF16)
    conv_out = conv_w_out.astype(BF16)
    for i in range(depth):
        mod = mods[i]
        x = _ffn(x, mod, norm_g[i, 0], ffn_in, ffn_out, i, 0, 0)
        j = i // 2
        if i % 2 == 0:
            x = _conv_mixer(x, mod, norm_g[i, 1], conv_in, conv_w[j], conv_out, j)
        else:
            x = _nsa_mixer(x, mod, norm_g[i, 1], nsa_w_in[j], nsa_cmp_pos[j], nsa_cmp_w1[j],
                           nsa_cmp_w2[j], nsa_q_gain[j], nsa_k_gain[j], nsa_w_out[j], tables)
        x = _ffn(x, mod, norm_g[i, 2], ffn_in, ffn_out, i, 1, 2)
    return x
```

```python
import functools
import math

import numpy as np
import jax
import jax.numpy as jnp
from jax import lax
from jax.experimental import pallas as pl
from jax.experimental.pallas import tpu as pltpu

F32 = jnp.float32
BF16 = jnp.bfloat16

D_FF = 2816
HEADS = 16
HEAD_DIM = 64
GROUPS = 4
HEADS_PER_GROUP = 4
KV_DIM = GROUPS * HEAD_DIM
CMP_BLOCK = 32
CMP_STRIDE = 16
CMP_HIDDEN = 256
SEL_BLOCK = 64
SEL_TOP_N = 16
WINDOW = 512
Q_BLOCK = 128
REL_BUCKETS = 32
REL_MAX_DIST = 128
RMS_EPS = 1e-6
NEG = -1e30

LANES = 128
QL = HEADS_PER_GROUP * Q_BLOCK
FAR_KEYS = 1024
FAR_BLOCKS = FAR_KEYS // SEL_BLOCK
FAR_SUB = 256
FAR_GROUPS = (4, 2, 1)
Q_TILES = 2
CMP_ROWS = 128
V_ROWS = HEAD_DIM + 16
LOG2E = math.log2(math.e)
FF_CHUNK = 256
TOKEN_TILE = 512
FFN_TOKEN_TILE = 1024
VMEM_LIMIT = 56 * 1024 * 1024


def _dot(a, b):
    return jnp.dot(a, b, preferred_element_type=F32)


def _cparams(sem):
    return pltpu.CompilerParams(dimension_semantics=sem, vmem_limit_bytes=VMEM_LIMIT)


def _resident(shape, index_map):
    return pl.BlockSpec(shape, index_map, pipeline_mode=pl.Buffered(1))


def _adaln(x, g, mod_ref, sub):
    ms = jnp.mean(x * x, axis=-1, keepdims=True)
    shift = mod_ref[3 * sub:3 * sub + 1, :]
    scale = mod_ref[3 * sub + 1:3 * sub + 2, :]
    return (x * lax.rsqrt(ms + RMS_EPS) * g) * (1.0 + scale) + shift


def _split_bf16(v):
    hi = v.astype(BF16)
    lo = (v - hi.astype(F32)).astype(BF16)
    return hi, lo


def _ada_kernel(c_ref, w_ref, b_ref, o_ref):
    c = c_ref[...]
    cond = c * jax.nn.sigmoid(c)
    chi, clo = _split_bf16(cond)
    whi, wlo = _split_bf16(w_ref[...])
    o_ref[...] = _dot(chi, whi) + _dot(chi, wlo) + _dot(clo, whi) + b_ref[...]


def _ada_mod(c, ada_w, ada_b):
    depth, d, n = ada_w.shape
    bsz = c.shape[0]
    rows = 8
    tn = n // 4
    c_pad = jnp.zeros((rows, d), F32).at[:bsz].set(c)
    out = pl.pallas_call(
        _ada_kernel,
        grid=(depth, n // tn),
        in_specs=[
            pl.BlockSpec((rows, d), lambda l, j: (0, 0)),
            pl.BlockSpec((None, d, tn), lambda l, j: (l, 0, j)),
            pl.BlockSpec((None, 1, tn), lambda l, j: (l, 0, j)),
        ],
        out_specs=pl.BlockSpec((None, rows, tn), lambda l, j: (l, 0, j)),
        out_shape=jax.ShapeDtypeStruct((depth, rows, n), F32),
        compiler_params=_cparams(("arbitrary", "arbitrary")),
        name="ada_mod",
    )(c_pad, ada_w, ada_b.reshape(depth, 1, n))
    return out[:, :bsz].reshape(depth, bsz, 9, d)


def _ffn_kernel(x_ref, mod_ref, g_ref, win_ref, wout_ref, o_ref, h_scr, acc_scr, *, sub):
    n_half, hm, _ = h_scr.shape
    nchunk = D_FF // FF_CHUNK
    gate = 0.5 * mod_ref[3 * sub + 2:3 * sub + 3, :]
    for hh in range(n_half):
        rows = slice(hh * hm, (hh + 1) * hm)
        h_scr[hh] = _adaln(x_ref[rows, :], g_ref[...], mod_ref, sub).astype(BF16)

        def up(c):
            h = h_scr[hh]
            lo = c * FF_CHUNK
            return (_dot(h, win_ref[:, lo:lo + FF_CHUNK]),
                    _dot(h, win_ref[:, D_FF + lo:D_FF + lo + FF_CHUNK]))

        nxt = up(0)
        for c in range(nchunk):
            gg, uu = nxt
            if c + 1 < nchunk:
                nxt = up(c + 1)
            a = (gg * jax.nn.sigmoid(gg) * uu).astype(BF16)
            down = _dot(a, wout_ref[c * FF_CHUNK:(c + 1) * FF_CHUNK, :])
            if c == 0:
                acc_scr[hh] = down
            else:
                acc_scr[hh] += down
        o_ref[rows, :] = x_ref[rows, :] + gate * acc_scr[hh]


def _ffn(x, mod, g, w_in_all, w_out_all, layer, slot, sub):
    bsz, seq, d = x.shape
    tm = FFN_TOKEN_TILE
    hm = tm // 2
    return pl.pallas_call(
        functools.partial(_ffn_kernel, sub=sub),
        grid=(bsz, seq // tm),
        in_specs=[
            pl.BlockSpec((None, tm, d), lambda b, i: (b, i, 0)),
            pl.BlockSpec((None, 9, d), lambda b, i: (b, 0, 0)),
            pl.BlockSpec((1, d), lambda b, i: (0, 0)),
            _resident((None, None, d, 2 * D_FF), lambda b, i: (layer, slot, 0, 0)),
            _resident((None, None, D_FF, d), lambda b, i: (layer, slot, 0, 0)),
        ],
        out_specs=pl.BlockSpec((None, tm, d), lambda b, i: (b, i, 0)),
        out_shape=jax.ShapeDtypeStruct(x.shape, F32),
        scratch_shapes=[pltpu.VMEM((2, hm, d), BF16), pltpu.VMEM((2, hm, d), F32)],
        compiler_params=_cparams(("arbitrary", "arbitrary")),
        name="ffn",
    )(x, mod, g.reshape(1, d), w_in_all, w_out_all)


def _conv_kernel(x_ref, mod_ref, g_ref, win_ref, cw_ref, wout_ref, o_ref, ubuf):
    tm, d = x_ref.shape
    hm = tm // 2

    @pl.when(pl.program_id(1) == 0)
    def _():
        ubuf[0:8, :] = jnp.zeros((8, d), F32)

    for hh in range(2):
        r0 = hh * hm
        x = x_ref[r0:r0 + hm, :]
        hb = _adaln(x, g_ref[...], mod_ref, 1).astype(BF16)
        bg = _dot(hb, win_ref[:, 0:d])
        u = _dot(hb, win_ref[:, d:2 * d]) * _dot(hb, win_ref[:, 2 * d:3 * d])
        ubuf[8 + r0:8 + r0 + hm, :] = u
        y = (cw_ref[2:3, :] * u + cw_ref[1:2, :] * ubuf[7 + r0:7 + r0 + hm, :]
             + cw_ref[0:1, :] * ubuf[6 + r0:6 + r0 + hm, :])
        out = _dot((bg * y).astype(BF16), wout_ref[...])
        o_ref[r0:r0 + hm, :] = x + mod_ref[5:6, :] * out
    ubuf[0:8, :] = ubuf[tm:tm + 8, :]


def _conv_mixer(x, mod, g, w_in_all, conv_w, w_out_all, layer):
    bsz, seq, d = x.shape
    tm = 2 * TOKEN_TILE
    return pl.pallas_call(
        _conv_kernel,
        grid=(bsz, seq // tm),
        in_specs=[
            pl.BlockSpec((None, tm, d), lambda b, i: (b, i, 0)),
            pl.BlockSpec((None, 9, d), lambda b, i: (b, 0, 0)),
            pl.BlockSpec((1, d), lambda b, i: (0, 0)),
            _resident((None, d, 3 * d), lambda b, i: (layer, 0, 0)),
            pl.BlockSpec((3, d), lambda b, i: (0, 0)),
            _resident((None, d, d), lambda b, i: (layer, 0, 0)),
        ],
        out_specs=pl.BlockSpec((None, tm, d), lambda b, i: (b, i, 0)),
        out_shape=jax.ShapeDtypeStruct(x.shape, F32),
        scratch_shapes=[pltpu.VMEM((tm + 8, d), F32)],
        compiler_params=_cparams(("arbitrary", "arbitrary")),
        name="conv_mixer",
    )(x, mod, g.reshape(1, d), w_in_all, conv_w, w_out_all)


def _head_ms(sq, p_ref):
    hi, lo = _split_bf16(sq)
    return _dot(hi, p_ref[...]) + _dot(lo, p_ref[...])


def _proj_kernel(x_ref, mod_ref, g_ref, w_ref, wT_ref, p_ref, qg_ref, kg_ref,
                 qT_ref, kvc_ref, ks_ref, vsT_ref, kw_ref, vwT_ref, gT_ref):
    tm = x_ref.shape[0]
    for hh in range(2):
        _proj_half(hh * (tm // 2), tm // 2, x_ref, mod_ref, g_ref, w_ref, wT_ref, p_ref, qg_ref, kg_ref,
                   qT_ref, kvc_ref, ks_ref, vsT_ref, kw_ref, vwT_ref, gT_ref)


def _proj_half(r0, hm, x_ref, mod_ref, g_ref, w_ref, wT_ref, p_ref, qg_ref, kg_ref,
               qT_ref, kvc_ref, ks_ref, vsT_ref, kw_ref, vwT_ref, gT_ref):
    tok = slice(r0, r0 + hm)
    hb = _adaln(x_ref[tok, :], g_ref[...], mod_ref, 1).astype(BF16)

    def col(i, width=KV_DIM):
        return _dot(hb, w_ref[:, i:i + width])

    def colT(i, width=KV_DIM):
        return lax.dot_general(wT_ref[i:i + width, :], hb, (((1,), (1,)), ((), ())),
                               preferred_element_type=F32)

    qg = jnp.concatenate([qg_ref[...]] * (hm // LANES), axis=1)
    for c in range(GROUPS):
        qc = colT(c * KV_DIM)
        ms = jnp.mean((qc * qc).reshape(HEADS_PER_GROUP, HEAD_DIM, hm), axis=1, keepdims=True)
        ms = jnp.broadcast_to(ms, (HEADS_PER_GROUP, HEAD_DIM, hm)).reshape(KV_DIM, hm)
        qn = qc * lax.rsqrt(ms + RMS_EPS) * qg
        qT_ref[c * KV_DIM:(c + 1) * KV_DIM, tok] = qn.astype(BF16)
    vsT = colT(HEADS * HEAD_DIM).astype(BF16)
    vwT = colT(HEADS * HEAD_DIM + KV_DIM).astype(BF16)
    rid = lax.broadcasted_iota(jnp.int32, (V_ROWS - HEAD_DIM, hm), 0)
    ones_rows = jnp.where(rid == 0, 1.0, 0.0).astype(BF16)
    for g in range(GROUPS):
        vsT_ref[g, :, tok] = jnp.concatenate([vsT[g * HEAD_DIM:(g + 1) * HEAD_DIM, :], ones_rows], axis=0)
        vwT_ref[g, :, tok] = jnp.concatenate([vwT[g * HEAD_DIM:(g + 1) * HEAD_DIM, :], ones_rows], axis=0)
    gT_ref[:, tok] = jax.nn.sigmoid(colT(HEADS * HEAD_DIM + 2 * KV_DIM, LANES))

    for kv in range(2):
        kvc = col(kv * KV_DIM)
        for half in range(2):
            kvc_ref[kv, half, tok, :] = kvc[:, half * LANES:(half + 1) * LANES]

    pos = pl.program_id(1) * x_ref.shape[0] + r0 + lax.broadcasted_iota(jnp.int32, (hm, HEAD_DIM), 0)
    lane = lax.broadcasted_iota(jnp.int32, (hm, HEAD_DIM), 1)
    onehot = jnp.where(lane == ((pos // SEL_BLOCK) % FAR_BLOCKS), 1.0, 0.0)
    zeros = jnp.zeros((hm, HEAD_DIM), F32)

    ks = col(2 * KV_DIM)
    ksn = ks * lax.rsqrt(_head_ms(ks * ks, p_ref) + RMS_EPS) * kg_ref[0:1, :]
    kw = col(3 * KV_DIM)
    kwn = kw * lax.rsqrt(_head_ms(kw * kw, p_ref) + RMS_EPS) * kg_ref[1:2, :]
    for g in range(GROUPS):
        sl = slice(g * HEAD_DIM, (g + 1) * HEAD_DIM)
        ks_ref[g, tok, :] = jnp.concatenate([ksn[:, sl], onehot], axis=1).astype(BF16)
        kw_ref[g, tok, :] = jnp.concatenate([kwn[:, sl], zeros], axis=1).astype(BF16)


def _nsa_proj(x, mod, g, w_in, q_gain, k_gain):
    bsz, seq, d = x.shape
    tm = 2 * TOKEN_TILE
    nq = HEADS * HEAD_DIM
    q, kc, vc, ks, vs, kw, vw, gl = jnp.split(w_in, [nq + i * KV_DIM for i in range(7)], axis=1)
    gl = gl.reshape(d, GROUPS, 3 * HEADS_PER_GROUP)
    gl = jnp.pad(gl, ((0, 0), (0, GROUPS), (0, 4))).reshape(d, LANES)
    w = jnp.concatenate([kc, vc, ks, kw], axis=1).astype(BF16)
    wT = jnp.concatenate([q, vs, vw, gl], axis=1).T.astype(BF16)
    hid = np.arange(KV_DIM) // HEAD_DIM
    pmat = jnp.asarray((hid[:, None] == hid[None, :]) / HEAD_DIM, BF16)
    qg = jnp.tile(q_gain, HEADS_PER_GROUP) * (HEAD_DIM ** -0.5 * LOG2E)
    qg = jnp.broadcast_to(qg[:, None], (KV_DIM, LANES))
    kg = jnp.stack([jnp.tile(k_gain[1], GROUPS), jnp.tile(k_gain[2], GROUPS)])
    kv_spec = pl.BlockSpec((None, GROUPS, tm, LANES), lambda b, i: (b, 0, i, 0))
    vT_spec = pl.BlockSpec((None, GROUPS, V_ROWS, tm), lambda b, i: (b, 0, 0, i))
    kv_shape = jax.ShapeDtypeStruct((bsz, GROUPS, seq, LANES), BF16)
    vT_shape = jax.ShapeDtypeStruct((bsz, GROUPS, V_ROWS, seq), BF16)
    return pl.pallas_call(
        _proj_kernel,
        grid=(bsz, seq // tm),
        in_specs=[
            pl.BlockSpec((None, tm, d), lambda b, i: (b, i, 0)),
            pl.BlockSpec((None, 9, d), lambda b, i: (b, 0, 0)),
            pl.BlockSpec((1, d), lambda b, i: (0, 0)),
            _resident(w.shape, lambda b, i: (0, 0)),
            _resident(wT.shape, lambda b, i: (0, 0)),
            pl.BlockSpec((KV_DIM, KV_DIM), lambda b, i: (0, 0)),
            pl.BlockSpec((KV_DIM, LANES), lambda b, i: (0, 0)),
            pl.BlockSpec((2, KV_DIM), lambda b, i: (0, 0)),
        ],
        out_specs=[
            pl.BlockSpec((None, nq, tm), lambda b, i: (b, 0, i)),
            pl.BlockSpec((2, 2, None, tm, LANES), lambda b, i: (0, 0, b, i, 0)),
            kv_spec, vT_spec, kv_spec, vT_spec,
            pl.BlockSpec((None, LANES, tm), lambda b, i: (b, 0, i)),
        ],
        out_shape=[
            jax.ShapeDtypeStruct((bsz, nq, seq), BF16),
            jax.ShapeDtypeStruct((2, 2, bsz, seq, LANES), F32),
            kv_shape, vT_shape, kv_shape, vT_shape,
            jax.ShapeDtypeStruct((bsz, LANES, seq), F32),
        ],
        compiler_params=_cparams(("arbitrary", "arbitrary")),
        name="nsa_proj",
    )(x, mod, g.reshape(1, d), w, wT, pmat, qg, kg)


def _gelu_tanh(x):
    return 0.5 * x * (1.0 + jnp.tanh(math.sqrt(2.0 / math.pi) * (x + 0.044715 * (x * x * x))))


def _compress_kernel(kv_ref, pos_ref, w1_ref, w2_ref, kg_ref, o_ref, acc_lo, acc_hi, *, is_key):
    nc = acc_lo.shape[1]
    half = CMP_BLOCK // 2
    acc_lo[...] = jnp.zeros_like(acc_lo)
    acc_hi[...] = jnp.zeros_like(acc_hi)
    for l in range(half):
        xl = [kv_ref[j, pl.ds(l, nc, stride=CMP_STRIDE), :] for j in range(2)]
        for g in range(GROUPS):
            xg = xl[g // 2][:, (g % 2) * HEAD_DIM:(g % 2 + 1) * HEAD_DIM]
            lo = (xg + pos_ref[l:l + 1, :]).astype(BF16)
            hi = (xg + pos_ref[half + l:half + l + 1, :]).astype(BF16)
            acc_lo[g] += _dot(lo, w1_ref[l * HEAD_DIM:(l + 1) * HEAD_DIM, :])
            acc_hi[g] += _dot(hi, w1_ref[(half + l) * HEAD_DIM:(half + l + 1) * HEAD_DIM, :])
    row = lax.broadcasted_iota(jnp.int32, (nc, CMP_HIDDEN), 0)
    for g in range(GROUPS):
        nxt = jnp.where(row < nc - 1, pltpu.roll(acc_hi[g], nc - 1, 0), 0.0)
        hdn = _gelu_tanh(acc_lo[g] + nxt).astype(BF16)
        if is_key:
            out = _dot(hdn, w2_ref[...])
            ms = jnp.sum(out * out, axis=-1, keepdims=True) * (1.0 / HEAD_DIM)
            o_ref[g] = (out * lax.rsqrt(ms + RMS_EPS) * kg_ref[...]).astype(BF16)
        else:
            outT = lax.dot_general(w2_ref[...], hdn, (((1,), (1,)), ((), ())),
                                   preferred_element_type=F32)
            rid = lax.broadcasted_iota(jnp.int32, (V_ROWS - HEAD_DIM, nc), 0)
            ones_rows = jnp.where(rid == 0, 1.0, 0.0).astype(BF16)
            o_ref[g] = jnp.concatenate([outT.astype(BF16), ones_rows], axis=0)


def _compress(kvc, pos, w1, w2, k_gain, is_key):
    _, _, bsz, seq, _ = kvc.shape
    which = 0 if is_key else 1
    nc = seq // CMP_STRIDE
    kg = jnp.pad(k_gain, (0, LANES - HEAD_DIM)).reshape(1, LANES)
    if is_key:
        w2p = jnp.pad(w2, ((0, 0), (0, LANES - HEAD_DIM))).astype(BF16)
    else:
        w2p = w2.T.astype(BF16)
    if is_key:
        out_spec = pl.BlockSpec((None, GROUPS, nc, LANES), lambda b: (b, 0, 0, 0))
        out_shape = jax.ShapeDtypeStruct((bsz, GROUPS, nc, LANES), BF16)
    else:
        out_spec = pl.BlockSpec((None, GROUPS, V_ROWS, nc), lambda b: (b, 0, 0, 0))
        out_shape = jax.ShapeDtypeStruct((bsz, GROUPS, V_ROWS, nc), BF16)
    return pl.pallas_call(
        functools.partial(_compress_kernel, is_key=is_key),
        grid=(bsz,),
        in_specs=[
            pl.BlockSpec((None, 2, None, seq, LANES), lambda b: (which, 0, b, 0, 0)),
            pl.BlockSpec((CMP_BLOCK, HEAD_DIM), lambda b: (0, 0)),
            pl.BlockSpec((CMP_BLOCK * HEAD_DIM, CMP_HIDDEN), lambda b: (0, 0)),
            pl.BlockSpec(w2p.shape, lambda b: (0, 0)),
            pl.BlockSpec((1, LANES), lambda b: (0, 0)),
        ],
        out_specs=out_spec,
        out_shape=out_shape,
        scratch_shapes=[pltpu.VMEM((GROUPS, nc, CMP_HIDDEN), F32),
                        pltpu.VMEM((GROUPS, nc, CMP_HIDDEN), F32)],
        compiler_params=_cparams(("arbitrary",)),
        name="compress_k" if is_key else "compress_v",
    )(kvc, pos, w1.astype(BF16), w2p, kg)


def _t5_bucket_np(dist):
    n = np.maximum(dist, 0)
    max_exact = REL_BUCKETS // 2
    nf = np.maximum(n, 1).astype(np.float32)
    large = max_exact + (np.log(nf / max_exact) / math.log(REL_MAX_DIST / max_exact)
                         * (REL_BUCKETS - max_exact)).astype(np.int32)
    large = np.minimum(large, REL_BUCKETS - 1)
    return np.where(n < max_exact, n, large)


def _bias_tables(rel_bias):
    rb = rel_bias.astype(F32).T
    rel = ((rb - rb[:, REL_BUCKETS - 1:]) * LOG2E).reshape(GROUPS, HEADS_PER_GROUP, REL_BUCKETS)
    ql = np.arange(Q_BLOCK)[None, :]
    kl = np.arange(Q_BLOCK)[:, None]
    d_lo, d_hi = -2 * Q_BLOCK, 3 * Q_BLOCK
    dvec = np.arange(d_lo, d_hi)
    onehot = jnp.asarray(np.eye(REL_BUCKETS, dtype=np.float32)[_t5_bucket_np(dvec)])
    vec = jnp.einsum("ghb,db->ghd", rel, onehot, precision=lax.Precision.HIGHEST)
    vec = jnp.where(jnp.asarray(dvec >= 0), vec, NEG)

    def table(dist):
        rows = [vec[:, :, int(d0) - d_lo:int(d0) - d_lo + Q_BLOCK] for d0 in dist[:, 0]]
        return jnp.stack(rows, axis=1).reshape(GROUPS, dist.shape[0], QL)

    near = jnp.stack([table(ql - kl + Q_BLOCK), table(ql - kl)], axis=1)
    win0 = np.where(ql - kl + WINDOW < WINDOW, 0.0, NEG).astype(np.float32)
    win0 = jnp.asarray(np.tile(win0, (1, HEADS_PER_GROUP)))
    r = np.arange(16)[:, None]
    cmp = table(ql - CMP_STRIDE * (r - 8) - (CMP_BLOCK - 1))
    cmp_first = jnp.concatenate([cmp[:, 8:], jnp.zeros_like(cmp[:, 8:])], axis=1)
    return near, win0, jnp.stack([cmp, cmp_first], axis=1)


def _attn_kernel(qT_ref, kc_ref, vcT_ref, ks_ref, vsT_ref, kw_ref, vwT_ref, gate_ref,
                 near_ref, win0_ref, cmpt_ref, ov_ref, o_ref,
                 rhs_z, sb_scr, sc_scr, sw_scr, sn_scr, s_nxt, p_pend, a_pend,
                 acc_scr, m_scr, oc_scr, imp_scr, *selb_far):
    nc = kc_ref.shape[0]
    ns = ov_ref.shape[0]
    n_win = WINDOW // Q_BLOCK + 1
    n_sub = FAR_KEYS // FAR_SUB
    last_tile = ks_ref.shape[0] // FAR_KEYS - 1
    tiles = range(Q_TILES)
    qis = [Q_TILES * pl.program_id(2) + t for t in tiles]
    t0s = [pl.multiple_of(qi * Q_BLOCK, Q_BLOCK) for qi in qis]
    tps = [pl.multiple_of(jnp.maximum(t0 - Q_BLOCK, 0), Q_BLOCK) for t0 in t0s]
    qcols = [slice(t * Q_BLOCK, (t + 1) * Q_BLOCK) for t in tiles]

    def tile4(v):
        return jnp.concatenate([v] * HEADS_PER_GROUP, axis=1)

    for t in tiles:
        q4 = qT_ref[:, qcols[t]]
        qt = jnp.concatenate([q4[h * HEAD_DIM:(h + 1) * HEAD_DIM, :] for h in range(HEADS_PER_GROUP)], axis=1)
        rhs_z[t, 0:HEAD_DIM, :] = qt
        rhs_z[t, HEAD_DIM:, :] = jnp.zeros((LANES - HEAD_DIM, QL), BF16)

    starts = [[t0s[t] - Q_BLOCK * (n_win - 1 - a) for a in range(n_win)] for t in tiles]
    starts_c = [[pl.multiple_of(jnp.maximum(st, 0), Q_BLOCK) for st in starts[t]] for t in tiles]

    def other_scores():
        for t in tiles:
            for a in range(n_win):
                s = _dot(kw_ref[pl.ds(starts_c[t][a], Q_BLOCK), :], rhs_z[t])
                if a == 0:
                    s = s + win0_ref[...]
                elif a == n_win - 2:
                    s = s + near_ref[0]
                elif a == n_win - 1:
                    s = s + near_ref[1]
                if a < n_win - 1:
                    s = jnp.where(starts[t][a] >= 0, s, NEG)
                sw_scr[t, a * Q_BLOCK:(a + 1) * Q_BLOCK, :] = s
            sn_scr[t, 0:Q_BLOCK, :] = _dot(ks_ref[pl.ds(tps[t], Q_BLOCK), :], rhs_z[t]) + near_ref[0]
            sn_scr[t, Q_BLOCK:, :] = _dot(ks_ref[pl.ds(t0s[t], Q_BLOCK), :], rhs_z[t]) + near_ref[1]
            s_nxt[t] = _dot(ks_ref[0:FAR_SUB, :], rhs_z[t])

    def cmp_stage(rows):
        row = lax.broadcasted_iota(jnp.int32, (rows, QL), 0)
        for t in tiles:
            sc_scr[t, 0:rows, :] = _dot(kc_ref[0:rows, :], rhs_z[t])
        other_scores()
        for t in tiles:
            qi = qis[t]
            st = pl.multiple_of(jnp.maximum(8 * qi - 8, 0), 8)
            sc_scr[t, pl.ds(st, 16), :] += jnp.where(qi == 0, cmpt_ref[1], cmpt_ref[0])
            s = jnp.where(row < 8 * qi + 8, sc_scr[t, 0:rows, :], NEG)
            m = jnp.max(s, axis=0, keepdims=True)
            p = jnp.exp2(s - m)
            pv = _dot(vcT_ref[:, 0:rows], p.astype(BF16))
            inv = jnp.where(m > 0.1 * NEG, 1.0 / pv[HEAD_DIM:HEAD_DIM + 1, :], 0.0)
            oc_scr[t] = pv[0:HEAD_DIM, :] * inv
            pn = p * inv
            psum = (pn[:, 0:Q_BLOCK] + pn[:, Q_BLOCK:2 * Q_BLOCK]
                    + pn[:, 2 * Q_BLOCK:3 * Q_BLOCK] + pn[:, 3 * Q_BLOCK:4 * Q_BLOCK])
            phi, plo = _split_bf16(psum)
            imp_scr[t] = _dot(ov_ref[:, 0:rows], phi) + _dot(ov_ref[:, 0:rows], plo)

    cmp_rows = min(CMP_ROWS, nc)
    need = (8 * qis[-1] + 8 + cmp_rows - 1) // cmp_rows
    for k in range(1, nc // cmp_rows + 1):
        pl.when(need == k)(functools.partial(cmp_stage, k * cmp_rows))
    imp = [imp_scr[t] for t in tiles]


    jrow = lax.broadcasted_iota(jnp.int32, (ns, Q_BLOCK), 0)
    qlane = lax.broadcasted_iota(jnp.int32, (ns, Q_BLOCK), 1)
    rowf = jrow.astype(F32)
    future, vals = [], []
    for t in tiles:
        cur = 2 * qis[t] + jnp.where(qlane >= SEL_BLOCK, 1, 0)
        fut = jrow > cur
        forced = (jrow == 0) | (jrow == cur) | (jrow == cur - 1)
        future.append(fut)
        vals.append(jnp.where(forced, -jnp.inf, jnp.where(fut, NEG, imp[t])))
    win_m = [None] * Q_TILES
    win_o = [None] * Q_TILES

    def win_max(t, a):
        ma = jnp.max(sw_scr[t, a * Q_BLOCK:(a + 1) * Q_BLOCK, :], axis=0, keepdims=True)
        win_m[t] = ma if win_m[t] is None else jnp.maximum(win_m[t], ma)

    def win_pv(t, a):
        pb = jnp.exp2(sw_scr[t, a * Q_BLOCK:(a + 1) * Q_BLOCK, :] - win_m[t]).astype(BF16)
        pv = _dot(vwT_ref[:, pl.ds(starts_c[t][a], Q_BLOCK)], pb)
        win_o[t] = pv if win_o[t] is None else win_o[t] + pv

    filler = [functools.partial(win_max, t, a) for a in range(n_win) for t in tiles]
    filler += [functools.partial(win_pv, t, a) for a in range(n_win) for t in tiles]
    per_round = -(-len(filler) // (SEL_TOP_N - 3))
    for r in range(SEL_TOP_N - 3):
        for t in tiles:
            v = vals[t]
            mx = jnp.max(v, axis=0, keepdims=True)
            idx = jnp.min(jnp.where(v == mx, rowf, float(ns)), axis=0, keepdims=True)
            vals[t] = jnp.where(rowf == idx, -jnp.inf, v)
        for f in filler[r * per_round:(r + 1) * per_round]:
            f()

    o_w = []
    for t in tiles:
        o_w.append(win_o[t][0:HEAD_DIM, :] / win_o[t][HEAD_DIM:HEAD_DIM + 1, :])
        sb = jnp.where((vals[t] == -jnp.inf) & jnp.logical_not(future[t]), 0.0, NEG)
        sb_scr[t] = sb
        sb_far = jnp.where(jrow < 2 * qis[t] - 2, sb, NEG)
        selb_far[t][...] = tile4(sb_far).astype(BF16)
        s_nxt[t] += jnp.concatenate(
            [jnp.broadcast_to(tile4(sb_far[b:b + 1, :]), (SEL_BLOCK, QL)) for b in range(FAR_SUB // SEL_BLOCK)],
            axis=0)
        m_scr[t] = jnp.full((1, QL), NEG, F32)
        acc_scr[t] = jnp.zeros((V_ROWS, QL), F32)
        p_pend[t] = jnp.zeros((FAR_SUB, QL), BF16)
        a_pend[t] = jnp.ones((1, QL), F32)

    def far_scores(jt, rhs, u):
        k0 = pl.multiple_of(jt * FAR_KEYS + u * FAR_SUB, FAR_SUB)
        return _dot(ks_ref[pl.ds(k0, FAR_SUB), :], rhs)

    def far_rhs(t, jt):
        r0 = pl.multiple_of(jt * FAR_BLOCKS, FAR_BLOCKS)
        return jnp.concatenate([rhs_z[t, 0:HEAD_DIM, :], selb_far[t][pl.ds(r0, FAR_BLOCKS), :],
                                jnp.zeros((LANES - HEAD_DIM - FAR_BLOCKS, QL), BF16)], axis=0)

    def far_pv(jt, u, p_bf16):
        k0 = pl.multiple_of(jt * FAR_KEYS + u * FAR_SUB, FAR_SUB)
        return _dot(vsT_ref[:, pl.ds(k0, FAR_SUB)], p_bf16)

    def far_body(t, jt, carry):
        jn = jnp.minimum(jt + 1, last_tile)
        jp = jnp.maximum(jt - 1, 0)
        rhs_cur, rhs_nxt = far_rhs(t, jt), far_rhs(t, jn)
        m, acc, s_cur = m_scr[t], acc_scr[t], s_nxt[t]
        pend = (jp, n_sub - 1, p_pend[t], a_pend[t])
        for u in range(n_sub):
            if u + 1 < n_sub:
                s_new = far_scores(jt, rhs_cur, u + 1)
            else:
                s_nxt[t] = far_scores(jn, rhs_nxt, 0)
            acc = pend[3] * acc + far_pv(pend[0], pend[1], pend[2])
            m_new = jnp.maximum(m, jnp.max(s_cur, axis=0, keepdims=True))
            alpha = jnp.exp2(m - m_new)
            p = jnp.exp2(s_cur - m_new)
            m = m_new
            pend = (jt, u, p.astype(BF16), alpha)
            if u + 1 < n_sub:
                s_cur = s_new
        p_pend[t] = pend[2]
        a_pend[t] = pend[3]
        m_scr[t] = m
        acc_scr[t] = acc
        return carry

    def far_body_group(t, k, base, jg, carry):
        for i in range(k):
            carry = far_body(t, base + k * jg + i, carry)
        return carry

    n_fars = [jnp.maximum(2 * qi - 2 + FAR_BLOCKS - 1, 0) // FAR_BLOCKS for qi in qis]
    for t in tiles:
        base, rem = 0, n_fars[t]
        for k in FAR_GROUPS:
            trips = rem // k
            lax.fori_loop(0, trips, functools.partial(far_body_group, t, k, base), 0)
            base, rem = base + trips * k, rem - trips * k

    krow = lax.broadcasted_iota(jnp.int32, (Q_BLOCK, QL), 0)
    for t in tiles:
        qi = qis[t]
        acc = a_pend[t] * acc_scr[t] + far_pv(jnp.maximum(n_fars[t] - 1, 0), n_sub - 1, p_pend[t])
        sel_a = tile4(sb_scr[t, pl.ds(jnp.maximum(2 * qi - 2, 0), 1), :])
        sel_b = tile4(sb_scr[t, pl.ds(jnp.maximum(2 * qi - 1, 0), 1), :])
        s_p = sn_scr[t, 0:Q_BLOCK, :] + jnp.where(krow < SEL_BLOCK, sel_a, sel_b)
        s_p = jnp.where(qi >= 1, s_p, NEG)
        s_d = sn_scr[t, Q_BLOCK:, :]
        m = m_scr[t]
        m_new = jnp.maximum(m, jnp.maximum(jnp.max(s_p, axis=0, keepdims=True),
                                           jnp.max(s_d, axis=0, keepdims=True)))
        acc = (jnp.exp2(m - m_new) * acc
               + _dot(vsT_ref[:, pl.ds(tps[t], Q_BLOCK)], jnp.exp2(s_p - m_new).astype(BF16))
               + _dot(vsT_ref[:, pl.ds(t0s[t], Q_BLOCK)], jnp.exp2(s_d - m_new).astype(BF16)))
        o_s = acc[0:HEAD_DIM, :] / acc[HEAD_DIM:HEAD_DIM + 1, :]

        heads = []
        for h in range(HEADS_PER_GROUP):
            sl = slice(h * Q_BLOCK, (h + 1) * Q_BLOCK)
            heads.append(gate_ref[3 * h:3 * h + 1, qcols[t]] * oc_scr[t, :, sl]
                         + gate_ref[3 * h + 1:3 * h + 2, qcols[t]] * o_s[:, sl]
                         + gate_ref[3 * h + 2:3 * h + 3, qcols[t]] * o_w[t][:, sl])
        o_ref[qcols[t], :] = jnp.concatenate(heads, axis=0).T.astype(BF16)


def _nsa_attention(qT, kcmp, vcmpT, ks, vsT, kw, vwT, gT, tables):
    bsz, _, seq = qT.shape
    assert seq % FAR_KEYS == 0
    nq = seq // Q_BLOCK
    nc = seq // CMP_STRIDE
    ns = seq // SEL_BLOCK
    near, win0, cmpt = tables
    cs = np.arange(nc) * CMP_STRIDE
    ss = np.arange(ns) * SEL_BLOCK
    ov = (cs[None, :] <= ss[:, None] + SEL_BLOCK - 1) & (cs[None, :] + CMP_BLOCK - 1 >= ss[:, None])
    ov = jnp.asarray(ov, BF16)
    gates = gT.reshape(bsz, 2 * GROUPS, 16, seq)
    per_bg = lambda b, g, i: (b, g, 0, 0)
    qw = Q_TILES * Q_BLOCK
    assert nq % Q_TILES == 0
    return pl.pallas_call(
        _attn_kernel,
        grid=(bsz, GROUPS, nq // Q_TILES),
        in_specs=[
            pl.BlockSpec((None, KV_DIM, qw), lambda b, g, i: (b, g, i)),
            pl.BlockSpec((None, None, nc, LANES), per_bg),
            pl.BlockSpec((None, None, V_ROWS, nc), per_bg),
            pl.BlockSpec((None, None, seq, LANES), per_bg),
            pl.BlockSpec((None, None, V_ROWS, seq), per_bg),
            pl.BlockSpec((None, None, seq, LANES), per_bg),
            pl.BlockSpec((None, None, V_ROWS, seq), per_bg),
            pl.BlockSpec((None, None, 16, qw), lambda b, g, i: (b, g, 0, i)),
            pl.BlockSpec((None, 2, Q_BLOCK, QL), lambda b, g, i: (g, 0, 0, 0)),
            pl.BlockSpec((Q_BLOCK, QL), lambda b, g, i: (0, 0)),
            pl.BlockSpec((None, 2, 16, QL), lambda b, g, i: (g, 0, 0, 0)),
            pl.BlockSpec((ns, nc), lambda b, g, i: (0, 0)),
        ],
        out_specs=pl.BlockSpec((None, qw, KV_DIM), lambda b, g, i: (b, i, g)),
        out_shape=jax.ShapeDtypeStruct((bsz, seq, HEADS * HEAD_DIM), BF16),
        scratch_shapes=[
            pltpu.VMEM((Q_TILES, LANES, QL), BF16),
            pltpu.VMEM((Q_TILES, ns, Q_BLOCK), F32),
            pltpu.VMEM((Q_TILES, nc, QL), F32),
            pltpu.VMEM((Q_TILES, WINDOW + Q_BLOCK, QL), F32),
            pltpu.VMEM((Q_TILES, 2 * Q_BLOCK, QL), F32),
            pltpu.VMEM((Q_TILES, FAR_SUB, QL), F32),
            pltpu.VMEM((Q_TILES, FAR_SUB, QL), BF16),
            pltpu.VMEM((Q_TILES, 1, QL), F32),
            pltpu.VMEM((Q_TILES, V_ROWS, QL), F32),
            pltpu.VMEM((Q_TILES, 1, QL), F32),
            pltpu.VMEM((Q_TILES, HEAD_DIM, QL), F32),
            pltpu.VMEM((Q_TILES, ns, Q_BLOCK), F32),
        ] +[pltpu.VMEM((ns, QL), BF16)] * Q_TILES,
        compiler_params=_cparams(("arbitrary", "arbitrary", "arbitrary")),
        name="nsa_attention",
    )(qT, kcmp, vcmpT, ks, vsT, kw, vwT, gates, near, win0, cmpt, ov)


def _oproj_kernel(x_ref, mod_ref, a_ref, w_ref, o_ref):
    o_ref[...] = x_ref[...] + mod_ref[5:6, :] * _dot(a_ref[...], w_ref[...])


def _nsa_out(x, mod, attn, w_out):
    bsz, seq, d = x.shape
    tm = 2 * TOKEN_TILE
    return pl.pallas_call(
        _oproj_kernel,
        grid=(bsz, seq // tm),
        in_specs=[
            pl.BlockSpec((None, tm, d), lambda b, i: (b, i, 0)),
            pl.BlockSpec((None, 9, d), lambda b, i: (b, 0, 0)),
            pl.BlockSpec((None, tm, d), lambda b, i: (b, i, 0)),
            _resident((d, d), lambda b, i: (0, 0)),
        ],
        out_specs=pl.BlockSpec((None, tm, d), lambda b, i: (b, i, 0)),
        out_shape=jax.ShapeDtypeStruct(x.shape, F32),
        compiler_params=_cparams(("arbitrary", "arbitrary")),
        name="nsa_out",
    )(x, mod, attn, w_out.astype(BF16))


def _nsa_mixer(x, mod, g, w_in, cmp_pos, cmp_w1, cmp_w2, q_gain, k_gain, w_out, tables):
    qT, kvc, ks, vsT, kw, vwT, gT = _nsa_proj(x, mod, g, w_in, q_gain, k_gain)
    kcmp = _compress(kvc, cmp_pos, cmp_w1[0], cmp_w2[0], k_gain[0], True)
    vcmpT = _compress(kvc, cmp_pos, cmp_w1[1], cmp_w2[1], k_gain[0], False)
    attn = _nsa_attention(qT, kcmp, vcmpT, ks, vsT, kw, vwT, gT, tables)
    return _nsa_out(x, mod, attn, w_out)


def kernel(x, c, norm_g, ada_w, ada_b, ffn_w_in, ffn_w_out, conv_w_in, conv_w, conv_w_out,
           nsa_w_in, nsa_cmp_pos, nsa_cmp_w1, nsa_cmp_w2, nsa_q_gain, nsa_k_gain, nsa_w_out,
           rel_bias):
    depth = ada_w.shape[0]
    mods = _ada_mod(c, ada_w, ada_b)
    tables = _bias_tables(rel_bias)
    ffn_in = ffn_w_in.astype(BF16)
    ffn_out = ffn_w_out.astype(BF16)
    conv_in = conv_w_in.astype(BF16)
    conv_out = conv_w_out.astype(BF16)
    for i in range(depth):
        mod = mods[i]
        x = _ffn(x, mod, norm_g[i, 0], ffn_in, ffn_out, i, 0, 0)
        j = i // 2
        if i % 2 == 0:
            x = _conv_mixer(x, mod, norm_g[i, 1], conv_in, conv_w[j], conv_out, j)
        else:
            x = _nsa_mixer(x, mod, norm_g[i, 1], nsa_w_in[j], nsa_cmp_pos[j], nsa_cmp_w1[j],
                           nsa_cmp_w2[j], nsa_q_gain[j], nsa_k_gain[j], nsa_w_out[j], tables)
        x = _ffn(x, mod, norm_g[i, 2], ffn_in, ffn_out, i, 1, 2)
    return x
```

```python
import functools
import math

import numpy as np
import jax
import jax.numpy as jnp
from jax import lax
from jax.experimental import pallas as pl
from jax.experimental.pallas import tpu as pltpu

F32 = jnp.float32
BF16 = jnp.bfloat16

D_FF = 2816
HEADS = 16
HEAD_DIM = 64
GROUPS = 4
HEADS_PER_GROUP = 4
KV_DIM = GROUPS * HEAD_DIM
CMP_BLOCK = 32
CMP_STRIDE = 16
CMP_HIDDEN = 256
SEL_BLOCK = 64
SEL_TOP_N = 16
WINDOW = 512
Q_BLOCK = 128
REL_BUCKETS = 32
REL_MAX_DIST = 128
RMS_EPS = 1e-6
NEG = -1e30

LANES = 128
QL = HEADS_PER_GROUP * Q_BLOCK
FAR_KEYS = 1024
FAR_BLOCKS = FAR_KEYS // SEL_BLOCK
FAR_SUB = 256
FAR_GROUPS = (4, 2, 1)
Q_TILES = 2
CMP_ROWS = 128
V_ROWS = HEAD_DIM + 16
LOG2E = math.log2(math.e)
FF_CHUNK = 256
TOKEN_TILE = 512
FFN_TOKEN_TILE = 1024
VMEM_LIMIT = 56 * 1024 * 1024


def _dot(a, b):
    return jnp.dot(a, b, preferred_element_type=F32)


def _cparams(sem):
    return pltpu.CompilerParams(dimension_semantics=sem, vmem_limit_bytes=VMEM_LIMIT)


def _resident(shape, index_map):
    return pl.BlockSpec(shape, index_map, pipeline_mode=pl.Buffered(1))


def _adaln(x, g, mod_ref, sub):
    ms = jnp.mean(x * x, axis=-1, keepdims=True)
    shift = mod_ref[3 * sub:3 * sub + 1, :]
    scale = mod_ref[3 * sub + 1:3 * sub + 2, :]
    return (x * lax.rsqrt(ms + RMS_EPS) * g) * (1.0 + scale) + shift


def _split_bf16(v):
    hi = v.astype(BF16)
    lo = (v - hi.astype(F32)).astype(BF16)
    return hi, lo


def _ada_kernel(c_ref, w_ref, b_ref, o_ref):
    c = c_ref[...]
    cond = c * jax.nn.sigmoid(c)
    chi, clo = _split_bf16(cond)
    whi, wlo = _split_bf16(w_ref[...])
    o_ref[...] = _dot(chi, whi) + _dot(chi, wlo) + _dot(clo, whi) + b_ref[...]


def _ada_mod(c, ada_w, ada_b):
    depth, d, n = ada_w.shape
    bsz = c.shape[0]
    rows = 8
    tn = n // 4
    c_pad = jnp.zeros((rows, d), F32).at[:bsz].set(c)
    out = pl.pallas_call(
        _ada_kernel,
        grid=(depth, n // tn),
        in_specs=[
            pl.BlockSpec((rows, d), lambda l, j: (0, 0)),
            pl.BlockSpec((None, d, tn), lambda l, j: (l, 0, j)),
            pl.BlockSpec((None, 1, tn), lambda l, j: (l, 0, j)),
        ],
        out_specs=pl.BlockSpec((None, rows, tn), lambda l, j: (l, 0, j)),
        out_shape=jax.ShapeDtypeStruct((depth, rows, n), F32),
        compiler_params=_cparams(("arbitrary", "arbitrary")),
        name="ada_mod",
    )(c_pad, ada_w, ada_b.reshape(depth, 1, n))
    return out[:, :bsz].reshape(depth, bsz, 9, d)


def _ffn_kernel(x_ref, mod_ref, g_ref, win_ref, wout_ref, *rest, sub, fused_mixer_out):
    if fused_mixer_out:
        a_ref, wmix_ref, o_ref, h_scr, acc_scr = rest
    else:
        o_ref, h_scr, acc_scr = rest
    n_half, hm, _ = h_scr.shape
    nchunk = D_FF // FF_CHUNK
    gate = 0.5 * mod_ref[3 * sub + 2:3 * sub + 3, :]
    for hh in range(n_half):
        rows = slice(hh * hm, (hh + 1) * hm)
        x = x_ref[rows, :]
        if fused_mixer_out:
            x = x + mod_ref[5:6, :] * _dot(a_ref[rows, :], wmix_ref[...])
        o_ref[rows, :] = x
        h_scr[hh] = _adaln(x, g_ref[...], mod_ref, sub).astype(BF16)

        def up(c):
            h = h_scr[hh]
            lo = c * FF_CHUNK
            return (_dot(h, win_ref[:, lo:lo + FF_CHUNK]),
                    _dot(h, win_ref[:, D_FF + lo:D_FF + lo + FF_CHUNK]))

        nxt = up(0)
        for c in range(nchunk):
            gg, uu = nxt
            if c + 1 < nchunk:
                nxt = up(c + 1)
            a = (gg * jax.nn.sigmoid(gg) * uu).astype(BF16)
            down = _dot(a, wout_ref[c * FF_CHUNK:(c + 1) * FF_CHUNK, :])
            if c == 0:
                acc_scr[hh] = down
            else:
                acc_scr[hh] += down
        o_ref[rows, :] = o_ref[rows, :] + gate * acc_scr[hh]


def _ffn(x, mod, g, w_in_all, w_out_all, layer, slot, sub, mixer_out=None):
    bsz, seq, d = x.shape
    tm = FFN_TOKEN_TILE
    hm = tm // 2
    in_specs = [
        pl.BlockSpec((None, tm, d), lambda b, i: (b, i, 0)),
        pl.BlockSpec((None, 9, d), lambda b, i: (b, 0, 0)),
        pl.BlockSpec((1, d), lambda b, i: (0, 0)),
        _resident((None, None, d, 2 * D_FF), lambda b, i: (layer, slot, 0, 0)),
        _resident((None, None, D_FF, d), lambda b, i: (layer, slot, 0, 0)),
    ]
    args = [x, mod, g.reshape(1, d), w_in_all, w_out_all]
    if mixer_out is not None:
        in_specs += [pl.BlockSpec((None, tm, d), lambda b, i: (b, i, 0)), _resident((d, d), lambda b, i: (0, 0))]
        args += list(mixer_out)
    return pl.pallas_call(
        functools.partial(_ffn_kernel, sub=sub, fused_mixer_out=mixer_out is not None),
        grid=(bsz, seq // tm),
        in_specs=in_specs,
        out_specs=pl.BlockSpec((None, tm, d), lambda b, i: (b, i, 0)),
        out_shape=jax.ShapeDtypeStruct(x.shape, F32),
        scratch_shapes=[pltpu.VMEM((2, hm, d), BF16), pltpu.VMEM((2, hm, d), F32)],
        compiler_params=_cparams(("arbitrary", "arbitrary")),
        name="ffn",
    )(*args)


def _conv_kernel(x_ref, mod_ref, g_ref, win_ref, cw_ref, wout_ref, o_ref, ubuf):
    tm, d = x_ref.shape
    hm = tm // 2

    @pl.when(pl.program_id(1) == 0)
    def _():
        ubuf[0:8, :] = jnp.zeros((8, d), F32)

    for hh in range(2):
        r0 = hh * hm
        x = x_ref[r0:r0 + hm, :]
        hb = _adaln(x, g_ref[...], mod_ref, 1).astype(BF16)
        bg = _dot(hb, win_ref[:, 0:d])
        u = _dot(hb, win_ref[:, d:2 * d]) * _dot(hb, win_ref[:, 2 * d:3 * d])
        ubuf[8 + r0:8 + r0 + hm, :] = u
        y = (cw_ref[2:3, :] * u + cw_ref[1:2, :] * ubuf[7 + r0:7 + r0 + hm, :]
             + cw_ref[0:1, :] * ubuf[6 + r0:6 + r0 + hm, :])
        out = _dot((bg * y).astype(BF16), wout_ref[...])
        o_ref[r0:r0 + hm, :] = x + mod_ref[5:6, :] * out
    ubuf[0:8, :] = ubuf[tm:tm + 8, :]


def _conv_mixer(x, mod, g, w_in_all, conv_w, w_out_all, layer):
    bsz, seq, d = x.shape
    tm = 2 * TOKEN_TILE
    return pl.pallas_call(
        _conv_kernel,
        grid=(bsz, seq // tm),
        in_specs=[
            pl.BlockSpec((None, tm, d), lambda b, i: (b, i, 0)),
            pl.BlockSpec((None, 9, d), lambda b, i: (b, 0, 0)),
            pl.BlockSpec((1, d), lambda b, i: (0, 0)),
            _resident((None, d, 3 * d), lambda b, i: (layer, 0, 0)),
            pl.BlockSpec((3, d), lambda b, i: (0, 0)),
            _resident((None, d, d), lambda b, i: (layer, 0, 0)),
        ],
        out_specs=pl.BlockSpec((None, tm, d), lambda b, i: (b, i, 0)),
        out_shape=jax.ShapeDtypeStruct(x.shape, F32),
        scratch_shapes=[pltpu.VMEM((tm + 8, d), F32)],
        compiler_params=_cparams(("arbitrary", "arbitrary")),
        name="conv_mixer",
    )(x, mod, g.reshape(1, d), w_in_all, conv_w, w_out_all)


def _head_ms(sq, p_ref):
    hi, lo = _split_bf16(sq)
    return _dot(hi, p_ref[...]) + _dot(lo, p_ref[...])


def _proj_kernel(x_ref, mod_ref, g_ref, w_ref, wT_ref, p_ref, qg_ref, kg_ref,
                 qT_ref, kvc_ref, ks_ref, vsT_ref, kw_ref, vwT_ref, gT_ref):
    tm = x_ref.shape[0]
    for hh in range(2):
        _proj_half(hh * (tm // 2), tm // 2, x_ref, mod_ref, g_ref, w_ref, wT_ref, p_ref, qg_ref, kg_ref,
                   qT_ref, kvc_ref, ks_ref, vsT_ref, kw_ref, vwT_ref, gT_ref)


def _proj_half(r0, hm, x_ref, mod_ref, g_ref, w_ref, wT_ref, p_ref, qg_ref, kg_ref,
               qT_ref, kvc_ref, ks_ref, vsT_ref, kw_ref, vwT_ref, gT_ref):
    tok = slice(r0, r0 + hm)
    hb = _adaln(x_ref[tok, :], g_ref[...], mod_ref, 1).astype(BF16)

    def col(i, width=KV_DIM):
        return _dot(hb, w_ref[:, i:i + width])

    def colT(i, width=KV_DIM):
        return lax.dot_general(wT_ref[i:i + width, :], hb, (((1,), (1,)), ((), ())),
                               preferred_element_type=F32)

    qg = jnp.concatenate([qg_ref[...]] * (hm // LANES), axis=1)
    for c in range(GROUPS):
        qc = colT(c * KV_DIM)
        ms = jnp.mean((qc * qc).reshape(HEADS_PER_GROUP, HEAD_DIM, hm), axis=1, keepdims=True)
        ms = jnp.broadcast_to(ms, (HEADS_PER_GROUP, HEAD_DIM, hm)).reshape(KV_DIM, hm)
        qn = qc * lax.rsqrt(ms + RMS_EPS) * qg
        qT_ref[c * KV_DIM:(c + 1) * KV_DIM, tok] = qn.astype(BF16)
    vsT = colT(HEADS * HEAD_DIM).astype(BF16)
    vwT = colT(HEADS * HEAD_DIM + KV_DIM).astype(BF16)
    rid = lax.broadcasted_iota(jnp.int32, (V_ROWS - HEAD_DIM, hm), 0)
    ones_rows = jnp.where(rid == 0, 1.0, 0.0).astype(BF16)
    for g in range(GROUPS):
        vsT_ref[g, :, tok] = jnp.concatenate([vsT[g * HEAD_DIM:(g + 1) * HEAD_DIM, :], ones_rows], axis=0)
        vwT_ref[g, :, tok] = jnp.concatenate([vwT[g * HEAD_DIM:(g + 1) * HEAD_DIM, :], ones_rows], axis=0)
    gT_ref[:, tok] = jax.nn.sigmoid(colT(HEADS * HEAD_DIM + 2 * KV_DIM, LANES))

    for kv in range(2):
        kvc = col(kv * KV_DIM)
        for half in range(2):
            kvc_ref[kv, half, tok, :] = kvc[:, half * LANES:(half + 1) * LANES]

    pos = pl.program_id(1) * x_ref.shape[0] + r0 + lax.broadcasted_iota(jnp.int32, (hm, HEAD_DIM), 0)
    lane = lax.broadcasted_iota(jnp.int32, (hm, HEAD_DIM), 1)
    onehot = jnp.where(lane == ((pos // SEL_BLOCK) % FAR_BLOCKS), 1.0, 0.0)
    zeros = jnp.zeros((hm, HEAD_DIM), F32)

    ks = col(2 * KV_DIM)
    ksn = ks * lax.rsqrt(_head_ms(ks * ks, p_ref) + RMS_EPS) * kg_ref[0:1, :]
    kw = col(3 * KV_DIM)
    kwn = kw * lax.rsqrt(_head_ms(kw * kw, p_ref) + RMS_EPS) * kg_ref[1:2, :]
    for g in range(GROUPS):
        sl = slice(g * HEAD_DIM, (g + 1) * HEAD_DIM)
        ks_ref[g, tok, :] = jnp.concatenate([ksn[:, sl], onehot], axis=1).astype(BF16)
        kw_ref[g, tok, :] = jnp.concatenate([kwn[:, sl], zeros], axis=1).astype(BF16)


def _nsa_proj(x, mod, g, w_in, q_gain, k_gain):
    bsz, seq, d = x.shape
    tm = 2 * TOKEN_TILE
    nq = HEADS * HEAD_DIM
    q, kc, vc, ks, vs, kw, vw, gl = jnp.split(w_in, [nq + i * KV_DIM for i in range(7)], axis=1)
    gl = gl.reshape(d, GROUPS, 3 * HEADS_PER_GROUP)
    gl = jnp.pad(gl, ((0, 0), (0, GROUPS), (0, 4))).reshape(d, LANES)
    w = jnp.concatenate([kc, vc, ks, kw], axis=1).astype(BF16)
    wT = jnp.concatenate([q, vs, vw, gl], axis=1).T.astype(BF16)
    hid = np.arange(KV_DIM) // HEAD_DIM
    pmat = jnp.asarray((hid[:, None] == hid[None, :]) / HEAD_DIM, BF16)
    qg = jnp.tile(q_gain, HEADS_PER_GROUP) * (HEAD_DIM ** -0.5 * LOG2E)
    qg = jnp.broadcast_to(qg[:, None], (KV_DIM, LANES))
    kg = jnp.stack([jnp.tile(k_gain[1], GROUPS), jnp.tile(k_gain[2], GROUPS)])
    kv_spec = pl.BlockSpec((None, GROUPS, tm, LANES), lambda b, i: (b, 0, i, 0))
    vT_spec = pl.BlockSpec((None, GROUPS, V_ROWS, tm), lambda b, i: (b, 0, 0, i))
    kv_shape = jax.ShapeDtypeStruct((bsz, GROUPS, seq, LANES), BF16)
    vT_shape = jax.ShapeDtypeStruct((bsz, GROUPS, V_ROWS, seq), BF16)
    return pl.pallas_call(
        _proj_kernel,
        grid=(bsz, seq // tm),
        in_specs=[
            pl.BlockSpec((None, tm, d), lambda b, i: (b, i, 0)),
            pl.BlockSpec((None, 9, d), lambda b, i: (b, 0, 0)),
            pl.BlockSpec((1, d), lambda b, i: (0, 0)),
            _resident(w.shape, lambda b, i: (0, 0)),
            _resident(wT.shape, lambda b, i: (0, 0)),
            pl.BlockSpec((KV_DIM, KV_DIM), lambda b, i: (0, 0)),
            pl.BlockSpec((KV_DIM, LANES), lambda b, i: (0, 0)),
            pl.BlockSpec((2, KV_DIM), lambda b, i: (0, 0)),
        ],
        out_specs=[
            pl.BlockSpec((None, nq, tm), lambda b, i: (b, 0, i)),
            pl.BlockSpec((2, 2, None, tm, LANES), lambda b, i: (0, 0, b, i, 0)),
            kv_spec, vT_spec, kv_spec, vT_spec,
            pl.BlockSpec((None, LANES, tm), lambda b, i: (b, 0, i)),
        ],
        out_shape=[
            jax.ShapeDtypeStruct((bsz, nq, seq), BF16),
            jax.ShapeDtypeStruct((2, 2, bsz, seq, LANES), F32),
            kv_shape, vT_shape, kv_shape, vT_shape,
            jax.ShapeDtypeStruct((bsz, LANES, seq), F32),
        ],
        compiler_params=_cparams(("arbitrary", "arbitrary")),
        name="nsa_proj",
    )(x, mod, g.reshape(1, d), w, wT, pmat, qg, kg)


def _gelu_tanh(x):
    return 0.5 * x * (1.0 + jnp.tanh(math.sqrt(2.0 / math.pi) * (x + 0.044715 * (x * x * x))))


def _compress_kernel(kv_ref, pos_ref, w1_ref, w2_ref, kg_ref, o_ref, acc_lo, acc_hi, *, is_key):
    nc = acc_lo.shape[1]
    half = CMP_BLOCK // 2
    acc_lo[...] = jnp.zeros_like(acc_lo)
    acc_hi[...] = jnp.zeros_like(acc_hi)
    for l in range(half):
        xl = [kv_ref[j, pl.ds(l, nc, stride=CMP_STRIDE), :] for j in range(2)]
        for g in range(GROUPS):
            xg = xl[g // 2][:, (g % 2) * HEAD_DIM:(g % 2 + 1) * HEAD_DIM]
            lo = (xg + pos_ref[l:l + 1, :]).astype(BF16)
            hi = (xg + pos_ref[half + l:half + l + 1, :]).astype(BF16)
            acc_lo[g] += _dot(lo, w1_ref[l * HEAD_DIM:(l + 1) * HEAD_DIM, :])
            acc_hi[g] += _dot(hi, w1_ref[(half + l) * HEAD_DIM:(half + l + 1) * HEAD_DIM, :])
    row = lax.broadcasted_iota(jnp.int32, (nc, CMP_HIDDEN), 0)
    for g in range(GROUPS):
        nxt = jnp.where(row < nc - 1, pltpu.roll(acc_hi[g], nc - 1, 0), 0.0)
        hdn = _gelu_tanh(acc_lo[g] + nxt).astype(BF16)
        if is_key:
            out = _dot(hdn, w2_ref[...])
            ms = jnp.sum(out * out, axis=-1, keepdims=True) * (1.0 / HEAD_DIM)
            o_ref[g] = (out * lax.rsqrt(ms + RMS_EPS) * kg_ref[...]).astype(BF16)
        else:
            outT = lax.dot_general(w2_ref[...], hdn, (((1,), (1,)), ((), ())),
                                   preferred_element_type=F32)
            rid = lax.broadcasted_iota(jnp.int32, (V_ROWS - HEAD_DIM, nc), 0)
            ones_rows = jnp.where(rid == 0, 1.0, 0.0).astype(BF16)
            o_ref[g] = jnp.concatenate([outT.astype(BF16), ones_rows], axis=0)


def _compress(kvc, pos, w1, w2, k_gain, is_key):
    _, _, bsz, seq, _ = kvc.shape
    which = 0 if is_key else 1
    nc = seq // CMP_STRIDE
    kg = jnp.pad(k_gain, (0, LANES - HEAD_DIM)).reshape(1, LANES)
    if is_key:
        w2p = jnp.pad(w2, ((0, 0), (0, LANES - HEAD_DIM))).astype(BF16)
    else:
        w2p = w2.T.astype(BF16)
    if is_key:
        out_spec = pl.BlockSpec((None, GROUPS, nc, LANES), lambda b: (b, 0, 0, 0))
        out_shape = jax.ShapeDtypeStruct((bsz, GROUPS, nc, LANES), BF16)
    else:
        out_spec = pl.BlockSpec((None, GROUPS, V_ROWS, nc), lambda b: (b, 0, 0, 0))
        out_shape = jax.ShapeDtypeStruct((bsz, GROUPS, V_ROWS, nc), BF16)
    return pl.pallas_call(
        functools.partial(_compress_kernel, is_key=is_key),
        grid=(bsz,),
        in_specs=[
            pl.BlockSpec((None, 2, None, seq, LANES), lambda b: (which, 0, b, 0, 0)),
            pl.BlockSpec((CMP_BLOCK, HEAD_DIM), lambda b: (0, 0)),
            pl.BlockSpec((CMP_BLOCK * HEAD_DIM, CMP_HIDDEN), lambda b: (0, 0)),
            pl.BlockSpec(w2p.shape, lambda b: (0, 0)),
            pl.BlockSpec((1, LANES), lambda b: (0, 0)),
        ],
        out_specs=out_spec,
        out_shape=out_shape,
        scratch_shapes=[pltpu.VMEM((GROUPS, nc, CMP_HIDDEN), F32),
                        pltpu.VMEM((GROUPS, nc, CMP_HIDDEN), F32)],
        compiler_params=_cparams(("arbitrary",)),
        name="compress_k" if is_key else "compress_v",
    )(kvc, pos, w1.astype(BF16), w2p, kg)


def _t5_bucket_np(dist):
    n = np.maximum(dist, 0)
    max_exact = REL_BUCKETS // 2
    nf = np.maximum(n, 1).astype(np.float32)
    large = max_exact + (np.log(nf / max_exact) / math.log(REL_MAX_DIST / max_exact)
                         * (REL_BUCKETS - max_exact)).astype(np.int32)
    large = np.minimum(large, REL_BUCKETS - 1)
    return np.where(n < max_exact, n, large)


def _bias_tables(rel_bias):
    rb = rel_bias.astype(F32).T
    rel = ((rb - rb[:, REL_BUCKETS - 1:]) * LOG2E).reshape(GROUPS, HEADS_PER_GROUP, REL_BUCKETS)
    ql = np.arange(Q_BLOCK)[None, :]
    kl = np.arange(Q_BLOCK)[:, None]
    d_lo, d_hi = -2 * Q_BLOCK, 3 * Q_BLOCK
    dvec = np.arange(d_lo, d_hi)
    onehot = jnp.asarray(np.eye(REL_BUCKETS, dtype=np.float32)[_t5_bucket_np(dvec)])
    vec = jnp.einsum("ghb,db->ghd", rel, onehot, precision=lax.Precision.HIGHEST)
    vec = jnp.where(jnp.asarray(dvec >= 0), vec, NEG)

    def table(dist):
        rows = [vec[:, :, int(d0) - d_lo:int(d0) - d_lo + Q_BLOCK] for d0 in dist[:, 0]]
        return jnp.stack(rows, axis=1).reshape(GROUPS, dist.shape[0], QL)

    near = jnp.stack([table(ql - kl + Q_BLOCK), table(ql - kl)], axis=1)
    win0 = np.where(ql - kl + WINDOW < WINDOW, 0.0, NEG).astype(np.float32)
    win0 = jnp.asarray(np.tile(win0, (1, HEADS_PER_GROUP)))
    r = np.arange(16)[:, None]
    cmp = table(ql - CMP_STRIDE * (r - 8) - (CMP_BLOCK - 1))
    cmp_first = jnp.concatenate([cmp[:, 8:], jnp.zeros_like(cmp[:, 8:])], axis=1)
    return near, win0, jnp.stack([cmp, cmp_first], axis=1)


def _attn_kernel(qT_ref, kc_ref, vcT_ref, ks_ref, vsT_ref, kw_ref, vwT_ref, gate_ref,
                 near_ref, win0_ref, cmpt_ref, ov_ref, o_ref,
                 rhs_z, sb_scr, sc_scr, sw_scr, sn_scr, s_nxt, p_pend, a_pend,
                 acc_scr, m_scr, oc_scr, imp_scr, *selb_far):
    nc = kc_ref.shape[0]
    ns = ov_ref.shape[0]
    n_win = WINDOW // Q_BLOCK + 1
    n_sub = FAR_KEYS // FAR_SUB
    last_tile = ks_ref.shape[0] // FAR_KEYS - 1
    tiles = range(Q_TILES)
    qis = [Q_TILES * pl.program_id(2) + t for t in tiles]
    t0s = [pl.multiple_of(qi * Q_BLOCK, Q_BLOCK) for qi in qis]
    tps = [pl.multiple_of(jnp.maximum(t0 - Q_BLOCK, 0), Q_BLOCK) for t0 in t0s]
    qcols = [slice(t * Q_BLOCK, (t + 1) * Q_BLOCK) for t in tiles]

    def tile4(v):
        return jnp.concatenate([v] * HEADS_PER_GROUP, axis=1)

    for t in tiles:
        q4 = qT_ref[:, qcols[t]]
        qt = jnp.concatenate([q4[h * HEAD_DIM:(h + 1) * HEAD_DIM, :] for h in range(HEADS_PER_GROUP)], axis=1)
        rhs_z[t, 0:HEAD_DIM, :] = qt
        rhs_z[t, HEAD_DIM:, :] = jnp.zeros((LANES - HEAD_DIM, QL), BF16)

    starts = [[t0s[t] - Q_BLOCK * (n_win - 1 - a) for a in range(n_win)] for t in tiles]
    starts_c = [[pl.multiple_of(jnp.maximum(st, 0), Q_BLOCK) for st in starts[t]] for t in tiles]

    def other_scores():
        for t in tiles:
            for a in range(n_win):
                s = _dot(kw_ref[pl.ds(starts_c[t][a], Q_BLOCK), :], rhs_z[t])
                if a == 0:
                    s = s + win0_ref[...]
                elif a == n_win - 2:
                    s = s + near_ref[0]
                elif a == n_win - 1:
                    s = s + near_ref[1]
                if a < n_win - 1:
                    s = jnp.where(starts[t][a] >= 0, s, NEG)
                sw_scr[t, a * Q_BLOCK:(a + 1) * Q_BLOCK, :] = s
            sn_scr[t, 0:Q_BLOCK, :] = _dot(ks_ref[pl.ds(tps[t], Q_BLOCK), :], rhs_z[t]) + near_ref[0]
            sn_scr[t, Q_BLOCK:, :] = _dot(ks_ref[pl.ds(t0s[t], Q_BLOCK), :], rhs_z[t]) + near_ref[1]
            s_nxt[t] = _dot(ks_ref[0:FAR_SUB, :], rhs_z[t])

    def cmp_stage(rows):
        row = lax.broadcasted_iota(jnp.int32, (rows, QL), 0)
        for t in tiles:
            sc_scr[t, 0:rows, :] = _dot(kc_ref[0:rows, :], rhs_z[t])
        other_scores()
        for t in tiles:
            qi = qis[t]
            st = pl.multiple_of(jnp.maximum(8 * qi - 8, 0), 8)
            sc_scr[t, pl.ds(st, 16), :] += jnp.where(qi == 0, cmpt_ref[1], cmpt_ref[0])
            s = jnp.where(row < 8 * qi + 8, sc_scr[t, 0:rows, :], NEG)
            m = jnp.max(s, axis=0, keepdims=True)
            p = jnp.exp2(s - m)
            pv = _dot(vcT_ref[:, 0:rows], p.astype(BF16))
            inv = jnp.where(m > 0.1 * NEG, 1.0 / pv[HEAD_DIM:HEAD_DIM + 1, :], 0.0)
            oc_scr[t] = pv[0:HEAD_DIM, :] * inv
            pn = p * inv
            psum = (pn[:, 0:Q_BLOCK] + pn[:, Q_BLOCK:2 * Q_BLOCK]
                    + pn[:, 2 * Q_BLOCK:3 * Q_BLOCK] + pn[:, 3 * Q_BLOCK:4 * Q_BLOCK])
            phi, plo = _split_bf16(psum)
            imp_scr[t] = _dot(ov_ref[:, 0:rows], phi) + _dot(ov_ref[:, 0:rows], plo)

    cmp_rows = min(CMP_ROWS, nc)
    need = (8 * qis[-1] + 8 + cmp_rows - 1) // cmp_rows
    for k in range(1, nc // cmp_rows + 1):
        pl.when(need == k)(functools.partial(cmp_stage, k * cmp_rows))
    imp = [imp_scr[t] for t in tiles]


    jrow = lax.broadcasted_iota(jnp.int32, (ns, Q_BLOCK), 0)
    qlane = lax.broadcasted_iota(jnp.int32, (ns, Q_BLOCK), 1)
    rowf = jrow.astype(F32)
    future, vals = [], []
    for t in tiles:
        cur = 2 * qis[t] + jnp.where(qlane >= SEL_BLOCK, 1, 0)
        fut = jrow > cur
        forced = (jrow == 0) | (jrow == cur) | (jrow == cur - 1)
        future.append(fut)
        vals.append(jnp.where(forced, -jnp.inf, jnp.where(fut, NEG, imp[t])))
    win_m = [None] * Q_TILES
    win_o = [None] * Q_TILES

    def win_max(t, a):
        ma = jnp.max(sw_scr[t, a * Q_BLOCK:(a + 1) * Q_BLOCK, :], axis=0, keepdims=True)
        win_m[t] = ma if win_m[t] is None else jnp.maximum(win_m[t], ma)

    def win_pv(t, a):
        pb = jnp.exp2(sw_scr[t, a * Q_BLOCK:(a + 1) * Q_BLOCK, :] - win_m[t]).astype(BF16)
        pv = _dot(vwT_ref[:, pl.ds(starts_c[t][a], Q_BLOCK)], pb)
        win_o[t] = pv if win_o[t] is None else win_o[t] + pv

    filler = [functools.partial(win_max, t, a) for a in range(n_win) for t in tiles]
    filler += [functools.partial(win_pv, t, a) for a in range(n_win) for t in tiles]
    per_round = -(-len(filler) // (SEL_TOP_N - 3))
    for r in range(SEL_TOP_N - 3):
        for t in tiles:
            v = vals[t]
            mx = jnp.max(v, axis=0, keepdims=True)
            idx = jnp.min(jnp.where(v == mx, rowf, float(ns)), axis=0, keepdims=True)
            vals[t] = jnp.where(rowf == idx, -jnp.inf, v)
        for f in filler[r * per_round:(r + 1) * per_round]:
            f()

    o_w = []
    for t in tiles:
        o_w.append(win_o[t][0:HEAD_DIM, :] / win_o[t][HEAD_DIM:HEAD_DIM + 1, :])
        sb = jnp.where((vals[t] == -jnp.inf) & jnp.logical_not(future[t]), 0.0, NEG)
        sb_scr[t] = sb
        sb_far = jnp.where(jrow < 2 * qis[t] - 2, sb, NEG)
        selb_far[t][...] = tile4(sb_far).astype(BF16)
        s_nxt[t] += jnp.concatenate(
            [jnp.broadcast_to(tile4(sb_far[b:b + 1, :]), (SEL_BLOCK, QL)) for b in range(FAR_SUB // SEL_BLOCK)],
            axis=0)
        m_scr[t] = jnp.full((1, QL), NEG, F32)
        acc_scr[t] = jnp.zeros((V_ROWS, QL), F32)
        p_pend[t] = jnp.zeros((FAR_SUB, QL), BF16)
        a_pend[t] = jnp.ones((1, QL), F32)

    def far_scores(jt, rhs, u):
        k0 = pl.multiple_of(jt * FAR_KEYS + u * FAR_SUB, FAR_SUB)
        return _dot(ks_ref[pl.ds(k0, FAR_SUB), :], rhs)

    def far_rhs(t, jt):
        r0 = pl.multiple_of(jt * FAR_BLOCKS, FAR_BLOCKS)
        return jnp.concatenate([rhs_z[t, 0:HEAD_DIM, :], selb_far[t][pl.ds(r0, FAR_BLOCKS), :],
                                jnp.zeros((LANES - HEAD_DIM - FAR_BLOCKS, QL), BF16)], axis=0)

    def far_pv(jt, u, p_bf16):
        k0 = pl.multiple_of(jt * FAR_KEYS + u * FAR_SUB, FAR_SUB)
        return _dot(vsT_ref[:, pl.ds(k0, FAR_SUB)], p_bf16)

    def far_body(t, jt, carry):
        jn = jnp.minimum(jt + 1, last_tile)
        jp = jnp.maximum(jt - 1, 0)
        rhs_cur, rhs_nxt = far_rhs(t, jt), far_rhs(t, jn)
        m, acc, s_cur = m_scr[t], acc_scr[t], s_nxt[t]
        pend = (jp, n_sub - 1, p_pend[t], a_pend[t])
        for u in range(n_sub):
            if u + 1 < n_sub:
                s_new = far_scores(jt, rhs_cur, u + 1)
            else:
                s_nxt[t] = far_scores(jn, rhs_nxt, 0)
            acc = pend[3] * acc + far_pv(pend[0], pend[1], pend[2])
            m_new = jnp.maximum(m, jnp.max(s_cur, axis=0, keepdims=True))
            alpha = jnp.exp2(m - m_new)
            p = jnp.exp2(s_cur - m_new)
            m = m_new
            pend = (jt, u, p.astype(BF16), alpha)
            if u + 1 < n_sub:
                s_cur = s_new
        p_pend[t] = pend[2]
        a_pend[t] = pend[3]
        m_scr[t] = m
        acc_scr[t] = acc
        return carry

    def far_body_group(t, k, base, jg, carry):
        for i in range(k):
            carry = far_body(t, base + k * jg + i, carry)
        return carry

    n_fars = [jnp.maximum(2 * qi - 2 + FAR_BLOCKS - 1, 0) // FAR_BLOCKS for qi in qis]
    for t in tiles:
        base, rem = 0, n_fars[t]
        for k in FAR_GROUPS:
            trips = rem // k
            lax.fori_loop(0, trips, functools.partial(far_body_group, t, k, base), 0)
            base, rem = base + trips * k, rem - trips * k

    krow = lax.broadcasted_iota(jnp.int32, (Q_BLOCK, QL), 0)
    for t in tiles:
        qi = qis[t]
        acc = a_pend[t] * acc_scr[t] + far_pv(jnp.maximum(n_fars[t] - 1, 0), n_sub - 1, p_pend[t])
        sel_a = tile4(sb_scr[t, pl.ds(jnp.maximum(2 * qi - 2, 0), 1), :])
        sel_b = tile4(sb_scr[t, pl.ds(jnp.maximum(2 * qi - 1, 0), 1), :])
        s_p = sn_scr[t, 0:Q_BLOCK, :] + jnp.where(krow < SEL_BLOCK, sel_a, sel_b)
        s_p = jnp.where(qi >= 1, s_p, NEG)
        s_d = sn_scr[t, Q_BLOCK:, :]
        m = m_scr[t]
        m_new = jnp.maximum(m, jnp.maximum(jnp.max(s_p, axis=0, keepdims=True),
                                           jnp.max(s_d, axis=0, keepdims=True)))
        acc = (jnp.exp2(m - m_new) * acc
               + _dot(vsT_ref[:, pl.ds(tps[t], Q_BLOCK)], jnp.exp2(s_p - m_new).astype(BF16))
               + _dot(vsT_ref[:, pl.ds(t0s[t], Q_BLOCK)], jnp.exp2(s_d - m_new).astype(BF16)))
        o_s = acc[0:HEAD_DIM, :] / acc[HEAD_DIM:HEAD_DIM + 1, :]

        heads = []
        for h in range(HEADS_PER_GROUP):
            sl = slice(h * Q_BLOCK, (h + 1) * Q_BLOCK)
            heads.append(gate_ref[3 * h:3 * h + 1, qcols[t]] * oc_scr[t, :, sl]
                         + gate_ref[3 * h + 1:3 * h + 2, qcols[t]] * o_s[:, sl]
                         + gate_ref[3 * h + 2:3 * h + 3, qcols[t]] * o_w[t][:, sl])
        o_ref[qcols[t], :] = jnp.concatenate(heads, axis=0).T.astype(BF16)


def _nsa_attention(qT, kcmp, vcmpT, ks, vsT, kw, vwT, gT, tables):
    bsz, _, seq = qT.shape
    assert seq % FAR_KEYS == 0
    nq = seq // Q_BLOCK
    nc = seq // CMP_STRIDE
    ns = seq // SEL_BLOCK
    near, win0, cmpt = tables
    cs = np.arange(nc) * CMP_STRIDE
    ss = np.arange(ns) * SEL_BLOCK
    ov = (cs[None, :] <= ss[:, None] + SEL_BLOCK - 1) & (cs[None, :] + CMP_BLOCK - 1 >= ss[:, None])
    ov = jnp.asarray(ov, BF16)
    gates = gT.reshape(bsz, 2 * GROUPS, 16, seq)
    per_bg = lambda b, g, i: (b, g, 0, 0)
    qw = Q_TILES * Q_BLOCK
    assert nq % Q_TILES == 0
    return pl.pallas_call(
        _attn_kernel,
        grid=(bsz, GROUPS, nq // Q_TILES),
        in_specs=[
            pl.BlockSpec((None, KV_DIM, qw), lambda b, g, i: (b, g, i)),
            pl.BlockSpec((None, None, nc, LANES), per_bg),
            pl.BlockSpec((None, None, V_ROWS, nc), per_bg),
            pl.BlockSpec((None, None, seq, LANES), per_bg),
            pl.BlockSpec((None, None, V_ROWS, seq), per_bg),
            pl.BlockSpec((None, None, seq, LANES), per_bg),
            pl.BlockSpec((None, None, V_ROWS, seq), per_bg),
            pl.BlockSpec((None, None, 16, qw), lambda b, g, i: (b, g, 0, i)),
            pl.BlockSpec((None, 2, Q_BLOCK, QL), lambda b, g, i: (g, 0, 0, 0)),
            pl.BlockSpec((Q_BLOCK, QL), lambda b, g, i: (0, 0)),
            pl.BlockSpec((None, 2, 16, QL), lambda b, g, i: (g, 0, 0, 0)),
            pl.BlockSpec((ns, nc), lambda b, g, i: (0, 0)),
        ],
        out_specs=pl.BlockSpec((None, qw, KV_DIM), lambda b, g, i: (b, i, g)),
        out_shape=jax.ShapeDtypeStruct((bsz, seq, HEADS * HEAD_DIM), BF16),
        scratch_shapes=[
            pltpu.VMEM((Q_TILES, LANES, QL), BF16),
            pltpu.VMEM((Q_TILES, ns, Q_BLOCK), F32),
            pltpu.VMEM((Q_TILES, nc, QL), F32),
            pltpu.VMEM((Q_TILES, WINDOW + Q_BLOCK, QL), F32),
            pltpu.VMEM((Q_TILES, 2 * Q_BLOCK, QL), F32),
            pltpu.VMEM((Q_TILES, FAR_SUB, QL), F32),
            pltpu.VMEM((Q_TILES, FAR_SUB, QL), BF16),
            pltpu.VMEM((Q_TILES, 1, QL), F32),
            pltpu.VMEM((Q_TILES, V_ROWS, QL), F32),
            pltpu.VMEM((Q_TILES, 1, QL), F32),
            pltpu.VMEM((Q_TILES, HEAD_DIM, QL), F32),
            pltpu.VMEM((Q_TILES, ns, Q_BLOCK), F32),
        ] +[pltpu.VMEM((ns, QL), BF16)] * Q_TILES,
        compiler_params=_cparams(("arbitrary", "arbitrary", "arbitrary")),
        name="nsa_attention",
    )(qT, kcmp, vcmpT, ks, vsT, kw, vwT, gates, near, win0, cmpt, ov)


def _nsa_mixer(x, mod, g, w_in, cmp_pos, cmp_w1, cmp_w2, q_gain, k_gain, w_out, tables):
    qT, kvc, ks, vsT, kw, vwT, gT = _nsa_proj(x, mod, g, w_in, q_gain, k_gain)
    kcmp = _compress(kvc, cmp_pos, cmp_w1[0], cmp_w2[0], k_gain[0], True)
    vcmpT = _compress(kvc, cmp_pos, cmp_w1[1], cmp_w2[1], k_gain[0], False)
    return _nsa_attention(qT, kcmp, vcmpT, ks, vsT, kw, vwT, gT, tables), w_out.astype(BF16)


def kernel(x, c, norm_g, ada_w, ada_b, ffn_w_in, ffn_w_out, conv_w_in, conv_w, conv_w_out,
           nsa_w_in, nsa_cmp_pos, nsa_cmp_w1, nsa_cmp_w2, nsa_q_gain, nsa_k_gain, nsa_w_out,
           rel_bias):
    depth = ada_w.shape[0]
    mods = _ada_mod(c, ada_w, ada_b)
    tables = _bias_tables(rel_bias)
    ffn_in = ffn_w_in.astype(BF16)
    ffn_out = ffn_w_out.astype(BF16)
    conv_in = conv_w_in.astype(BF16)
    conv_out = conv_w_out.astype(BF16)
    for i in range(depth):
        mod = mods[i]
        x = _ffn(x, mod, norm_g[i, 0], ffn_in, ffn_out, i, 0, 0)
        j = i // 2
        if i % 2 == 0:
            x = _conv_mixer(x, mod, norm_g[i, 1], conv_in, conv_w[j], conv_out, j)
            mixer_out = None
        else:
            mixer_out = _nsa_mixer(x, mod, norm_g[i, 1], nsa_w_in[j], nsa_cmp_pos[j], nsa_cmp_w1[j],
                                   nsa_cmp_w2[j], nsa_q_gain[j], nsa_k_gain[j], nsa_w_out[j], tables)
        x = _ffn(x, mod, norm_g[i, 2], ffn_in, ffn_out, i, 1, 2, mixer_out)
    return x
```
